```python
import jax, jax.numpy as jnp
from jax import lax
import numpy as np

D_MODEL = 2048
BATCH = 8
SEQ = 2048
DEPTH = 2

GRID_W = 64
CTX_LEN = 256
MIX_WIDTH = D_MODEL
ATTN_WIDTH = MIX_WIDTH // 2
FOURIER_WIDTH = MIX_WIDTH - ATTN_WIDTH
HEAD_DIM = 128
N_Q_HEADS = ATTN_WIDTH // HEAD_DIM
N_KV_HEADS = max(1, N_Q_HEADS // 4)
Q_PER_KV = N_Q_HEADS // N_KV_HEADS
FOURIER_GROUP = 128
N_FOURIER_GROUPS = FOURIER_WIDTH // FOURIER_GROUP
FFN_DIM = ((8 * D_MODEL // 3 + 255) // 256) * 256
ROPE_THETA = 10000.0
Q_BLOCK = 128
EPS = 1e-6
N_MOD = 6
Q_END = N_Q_HEADS * HEAD_DIM
K_END = Q_END + N_KV_HEADS * HEAD_DIM
V_END = K_END + N_KV_HEADS * HEAD_DIM
IN_WIDTH = V_END + FOURIER_WIDTH

kernel_name = "hymba_style_attn_fnet_convglu_dit"


def rms_norm(x, g):
    xf = x.astype(jnp.float32)
    y = xf * lax.rsqrt(jnp.mean(xf * xf, axis=-1, keepdims=True) + EPS)
    return (y * g.astype(jnp.float32)).astype(x.dtype)


def modulate(x, g_pre, shift, scale):
    return rms_norm(x, g_pre) * (1 + scale) + shift


def rope_tables(n):
    rows = n // GRID_W
    row = jnp.repeat(jnp.arange(rows), GRID_W).astype(jnp.float32)
    col = jnp.tile(jnp.arange(GRID_W), rows).astype(jnp.float32)
    n_pairs_axis = HEAD_DIM // 4
    freqs = ROPE_THETA ** (-jnp.arange(n_pairs_axis, dtype=jnp.float32) / n_pairs_axis)
    ang = jnp.concatenate([row[:, None] * freqs, col[:, None] * freqs], axis=-1)
    return jnp.cos(ang), jnp.sin(ang)


def apply_rope(x, cos, sin):
    n = x.shape[1]
    bshape = (1, n) + (1,) * (x.ndim - 3) + (HEAD_DIM // 2,)
    c, s = cos.reshape(bshape), sin.reshape(bshape)
    xf = x.astype(jnp.float32).reshape(x.shape[:-1] + (HEAD_DIM // 2, 2))
    e, o = xf[..., 0], xf[..., 1]
    out = jnp.stack([e * c - o * s, e * s + o * c], axis=-1)
    return out.reshape(x.shape).astype(x.dtype)


def attend(q, k, v):
    s = jnp.einsum('bqkgd,bskd->bkgqs', q, k).astype(jnp.float32) * (HEAD_DIM ** -0.5)
    p = jax.nn.softmax(s, axis=-1).astype(v.dtype)
    return jnp.einsum('bkgqs,bskd->bqkgd', p, v)


def latent_attention(q, k_lat, v_lat, k_ctx, v_ctx):
    b, n = q.shape[:2]
    k_all = jnp.concatenate([k_lat, k_ctx], axis=1)
    v_all = jnp.concatenate([v_lat, v_ctx], axis=1)
    nb = n // Q_BLOCK
    qb = jnp.moveaxis(q.reshape((b, nb, Q_BLOCK) + q.shape[2:]), 1, 0)
    out = lax.map(lambda blk: attend(blk, k_all, v_all), qb)
    return jnp.moveaxis(out, 0, 1).reshape(b, n, ATTN_WIDTH)


def split_proj(p):
    b, t = p.shape[:2]
    q = p[..., :Q_END].reshape(b, t, N_KV_HEADS, Q_PER_KV, HEAD_DIM)
    k = p[..., Q_END:K_END].reshape(b, t, N_KV_HEADS, HEAD_DIM)
    v = p[..., K_END:V_END].reshape(b, t, N_KV_HEADS, HEAD_DIM)
    f = p[..., V_END:]
    return q, k, v, f


def fourier_mix(u, w_four):
    b, t = u.shape[:2]
    g = u.reshape(b, t, N_FOURIER_GROUPS, FOURIER_GROUP).astype(jnp.float32)
    f = jnp.fft.fft2(g, axes=(1, 3), norm='ortho').real.astype(u.dtype)
    return jnp.einsum('btgc,gcd->btgd', f, w_four).reshape(b, t, FOURIER_WIDTH)


def merge_groups(attn_out, four_out, g_attn, g_four, w_out):
    y = jnp.concatenate([rms_norm(attn_out, g_attn), rms_norm(four_out, g_four)], axis=-1)
    return y @ w_out


def dwconv3(u, w, b):
    up = jnp.pad(u, ((0, 0), (1, 1), (0, 0)))
    return up[:, :-2] * w[0] + up[:, 1:-1] * w[1] + up[:, 2:] * w[2] + b


def conv_glu(h, w_up, conv_w, conv_b, w_down):
    u = h @ w_up
    gate, val = u[..., :FFN_DIM], u[..., FFN_DIM:]
    return (jax.nn.gelu(dwconv3(gate, conv_w, conv_b)) * val) @ w_down


def setup_inputs(seed: int = 0) -> dict:
    key = jax.random.key(seed)
    ks = jax.random.split(key, 21)
    f32 = jnp.float32

    def nrm(k, shape, s):
        return s * jax.random.normal(k, shape, f32)

    def gain(k, shape):
        return 1.0 + 0.1 * jax.random.normal(k, shape, f32)

    return {
        "x": nrm(ks[0], (BATCH, SEQ, D_MODEL), 1.0),
        "c": nrm(ks[1], (BATCH, D_MODEL), 1.0),
        "ctx": nrm(ks[2], (BATCH, CTX_LEN, D_MODEL), 1.0),
        "c_ctx": nrm(ks[3], (D_MODEL,), 1.0),
        "w_mod": nrm(ks[4], (DEPTH, D_MODEL, N_MOD * D_MODEL), 0.5 * D_MODEL ** -0.5),
        "b_mod": nrm(ks[5], (DEPTH, N_MOD * D_MODEL), 0.02),
        "g_pre_mix": gain(ks[6], (DEPTH, D_MODEL)),
        "g_post_mix": gain(ks[7], (DEPTH, D_MODEL)),
        "g_pre_ffn": gain(ks[8], (DEPTH, D_MODEL)),
        "g_post_ffn": gain(ks[9], (DEPTH, D_MODEL)),
        "w_in": nrm(ks[10], (DEPTH, D_MODEL, IN_WIDTH), D_MODEL ** -0.5),
        "q_norm": gain(ks[11], (DEPTH, HEAD_DIM)),
        "k_norm": gain(ks[12], (DEPTH, HEAD_DIM)),
        "w_four": nrm(ks[13], (DEPTH, N_FOURIER_GROUPS, FOURIER_GROUP, FOURIER_GROUP), FOURIER_GROUP ** -0.5),
        "g_attn_out": gain(ks[14], (DEPTH, ATTN_WIDTH)),
        "g_four_out": gain(ks[15], (DEPTH, FOURIER_WIDTH)),
        "w_out": nrm(ks[16], (DEPTH, MIX_WIDTH, D_MODEL), MIX_WIDTH ** -0.5),
        "w_up": nrm(ks[17], (DEPTH, D_MODEL, 2 * FFN_DIM), D_MODEL ** -0.5),
        "conv_w": nrm(ks[18], (DEPTH, 3, FFN_DIM), 3 ** -0.5),
        "conv_b": nrm(ks[19], (DEPTH, FFN_DIM), 0.02),
        "w_down": nrm(ks[20], (DEPTH, FFN_DIM, D_MODEL), FFN_DIM ** -0.5),
    }


def reference(x, c, ctx, c_ctx, w_mod, b_mod, g_pre_mix, g_post_mix, g_pre_ffn, g_post_ffn,
              w_in, q_norm, k_norm, w_four, g_attn_out, g_four_out, w_out,
              w_up, conv_w, conv_b, w_down):
    b, n, d = x.shape
    ROWS = n // GRID_W
    cos, sin = rope_tables(ROWS * GRID_W)
    xc = ctx
    silu_c = jax.nn.silu(c)
    silu_cc = jax.nn.silu(c_ctx)

    for i in range(DEPTH):
        last = i == DEPTH - 1
        mod_l = (silu_c @ w_mod[i] + b_mod[i]).reshape(b, N_MOD, 1, d)
        mod_c = (silu_cc @ w_mod[i] + b_mod[i]).reshape(N_MOD, d)

        h_l = modulate(x, g_pre_mix[i], mod_l[:, 0], mod_l[:, 1])
        h_c = modulate(xc, g_pre_mix[i], mod_c[0], mod_c[1])

        q_l, k_l, v_l, f_l = split_proj(h_l @ w_in[i])
        q_l = apply_rope(rms_norm(q_l, q_norm[i]), cos, sin)
        k_l = apply_rope(rms_norm(k_l, k_norm[i]), cos, sin)

        if last:
            p_kv = h_c @ w_in[i][:, Q_END:V_END]
            k_c = rms_norm(p_kv[..., :K_END - Q_END].reshape(b, -1, N_KV_HEADS, HEAD_DIM), k_norm[i])
            v_c = p_kv[..., K_END - Q_END:].reshape(b, -1, N_KV_HEADS, HEAD_DIM)
        else:
            q_c, k_c, v_c, f_c = split_proj(h_c @ w_in[i])
            q_c = rms_norm(q_c, q_norm[i])
            k_c = rms_norm(k_c, k_norm[i])

        attn_l = latent_attention(q_l, k_l, v_l, k_c, v_c)
        four_l = fourier_mix(f_l, w_four[i])
        mix_l = merge_groups(attn_l, four_l, g_attn_out[i], g_four_out[i], w_out[i])
        x_new = x + mod_l[:, 2] * rms_norm(mix_l, g_post_mix[i])

        if not last:
            attn_c = attend(q_c, k_c, v_c).reshape(b, -1, ATTN_WIDTH)
            four_c = fourier_mix(f_c, w_four[i])
            mix_c = merge_groups(attn_c, four_c, g_attn_out[i], g_four_out[i], w_out[i])
            xc = xc + mod_c[2] * rms_norm(mix_c, g_post_mix[i])
        x = x_new

        h_l = modulate(x, g_pre_ffn[i], mod_l[:, 3], mod_l[:, 4])
        y_l = conv_glu(h_l, w_up[i], conv_w[i], conv_b[i], w_down[i])
        x = x + mod_l[:, 5] * rms_norm(y_l, g_post_ffn[i])
        if not last:
            h_c = modulate(xc, g_pre_ffn[i], mod_c[3], mod_c[4])
            y_c = conv_glu(h_c, w_up[i], conv_w[i], conv_b[i], w_down[i])
            xc = xc + mod_c[5] * rms_norm(y_c, g_post_ffn[i])

    return x
```

```python
import functools
import math

import jax
import jax.numpy as jnp
from jax import lax
from jax.experimental import pallas as pl
from jax.experimental.pallas import tpu as pltpu

GRID_W = 64
HEAD_DIM = 128
N_Q_HEADS = 8
N_KV_HEADS = 2
Q_PER_KV = N_Q_HEADS // N_KV_HEADS
FOURIER_GROUP = 128
N_FOURIER_GROUPS = 8
ATTN_WIDTH = N_Q_HEADS * HEAD_DIM
KV_WIDTH = N_KV_HEADS * HEAD_DIM
FOURIER_WIDTH = N_FOURIER_GROUPS * FOURIER_GROUP
Q_END = ATTN_WIDTH
K_END = Q_END + KV_WIDTH
V_END = K_END + KV_WIDTH
IN_WIDTH = V_END + FOURIER_WIDTH
ROPE_THETA = 10000.0
EPS = 1e-6
N_MOD = 6
MOD_ROWS = 16

VMEM_LIMIT_BYTES = 56 * 1024 * 1024

F32 = jnp.float32
BF16 = jnp.bfloat16


def _params(*semantics):
    return pltpu.CompilerParams(dimension_semantics=semantics, vmem_limit_bytes=VMEM_LIMIT_BYTES)


def _rms(x, g):
    ms = jnp.mean(x * x, axis=-1, keepdims=True)
    return x * lax.rsqrt(ms + EPS) * g


def _const_spec(shape):
    return pl.BlockSpec(shape, lambda *_: (0,) * len(shape), pipeline_mode=pl.Buffered(1))


def _mod_kernel(c_ref, w_ref, b_ref, o_ref):
    c = c_ref[...]
    s = (c * jax.nn.sigmoid(c)).astype(BF16)
    o_ref[0] = jnp.dot(s, w_ref[0].astype(BF16), preferred_element_type=F32) + b_ref[0]


def _mod_proj(cc, w_mod, b_mod, tn=1024):
    depth, d, width = w_mod.shape
    return pl.pallas_call(
        _mod_kernel,
        grid=(depth, width // tn),
        in_specs=[
            pl.BlockSpec((MOD_ROWS, d), lambda l, j: (0, 0)),
            pl.BlockSpec((1, d, tn), lambda l, j: (l, 0, j)),
            pl.BlockSpec((1, 1, tn), lambda l, j: (l, 0, j)),
        ],
        out_specs=pl.BlockSpec((1, MOD_ROWS, tn), lambda l, j: (l, 0, j)),
        out_shape=jax.ShapeDtypeStruct((depth, MOD_ROWS, width), F32),
        compiler_params=_params("arbitrary", "arbitrary"),
        name="mod_proj",
    )(cc, w_mod, b_mod.reshape(depth, 1, width))


def _fold_kernel(cs_ref, w_ref, o_ref):
    w = w_ref[0, 0]
    o_ref[0, 0, :, :FOURIER_GROUP] = jnp.dot(
        cs_ref[0], w, preferred_element_type=F32, precision=lax.Precision.HIGHEST).astype(BF16)
    o_ref[0, 0, :, FOURIER_GROUP:] = jnp.dot(
        cs_ref[1], w, preferred_element_type=F32, precision=lax.Precision.HIGHEST).astype(BF16)


def _fold_channel_dft(cs, w_four):
    depth, groups, c, _ = w_four.shape
    return pl.pallas_call(
        _fold_kernel,
        grid=(depth, groups),
        in_specs=[
            pl.BlockSpec((2, c, c), lambda l, g: (0, 0, 0)),
            pl.BlockSpec((1, 1, c, c), lambda l, g: (l, g, 0, 0)),
        ],
        out_specs=pl.BlockSpec((1, 1, c, 2 * c), lambda l, g: (l, g, 0, 0)),
        out_shape=jax.ShapeDtypeStruct((depth, groups, c, 2 * c), BF16),
        compiler_params=_params("arbitrary", "arbitrary"),
        name="fold_channel_dft",
    )(cs, w_four)


def _in_proj_kernel(x_ref, mod_ref, gpre_ref, w_ref, qn_ref, kn_ref, cos_ref, sin_ref, m_ref,
                    q_ref, k_ref, v_ref, ab_ref, *, rope):
    mod = mod_ref[0]
    h = _rms(x_ref[...], gpre_ref[...]) * (1.0 + mod[1:2]) + mod[0:1]
    p = jnp.dot(h.astype(BF16), w_ref[...], preferred_element_type=F32)

    def head(col, gain, post_scale):
        y = _rms(p[:, col:col + HEAD_DIM], gain)
        if rope:
            y = y * cos_ref[...] + pltpu.roll(y, HEAD_DIM // 2, axis=1) * sin_ref[...]
        if post_scale != 1.0:
            y = y * post_scale
        return y.astype(BF16)

    for i in range(N_Q_HEADS):
        q_ref[:, i * HEAD_DIM:(i + 1) * HEAD_DIM] = head(i * HEAD_DIM, qn_ref[...], HEAD_DIM ** -0.5)
    for i in range(N_KV_HEADS):
        k_ref[:, i * HEAD_DIM:(i + 1) * HEAD_DIM] = head(Q_END + i * HEAD_DIM, kn_ref[...], 1.0)
    v_ref[...] = p[:, K_END:V_END].astype(BF16)
    for g in range(N_FOURIER_GROUPS):
        lo = V_END + g * FOURIER_GROUP
        ab = jnp.dot(p[:, lo:lo + FOURIER_GROUP].astype(BF16), m_ref[g], preferred_element_type=F32)
        ab_ref[:, g * FOURIER_GROUP:(g + 1) * FOURIER_GROUP] = ab[:, :FOURIER_GROUP].astype(BF16)
        ab_ref[:, FOURIER_WIDTH + g * FOURIER_GROUP:FOURIER_WIDTH + (g + 1) * FOURIER_GROUP] = (
            ab[:, FOURIER_GROUP:].astype(BF16))


def _in_proj(x2d, mod, g_pre, w_in, qn, kn, cos_t, sin_t, m_fold, *, seq, rope, tm):
    t, d = x2d.shape
    per_seq = seq // tm
    nb = mod.shape[0]
    mod_map = (lambda i: (i // per_seq, 0, 0)) if nb > 1 else (lambda i: (0, 0, 0))
    rope_map = (lambda i: (i % per_seq, 0)) if rope else (lambda i: (0, 0))
    return pl.pallas_call(
        functools.partial(_in_proj_kernel, rope=rope),
        grid=(t // tm,),
        in_specs=[
            pl.BlockSpec((tm, d), lambda i: (i, 0)),
            pl.BlockSpec((1, N_MOD, d), mod_map),
            _const_spec((1, d)),
            _const_spec((d, IN_WIDTH)),
            _const_spec((1, HEAD_DIM)),
            _const_spec((1, HEAD_DIM)),
            pl.BlockSpec((tm, HEAD_DIM), rope_map),
            pl.BlockSpec((tm, HEAD_DIM), rope_map),
            _const_spec((N_FOURIER_GROUPS, FOURIER_GROUP, 2 * FOURIER_GROUP)),
        ],
        out_specs=[
            pl.BlockSpec((tm, ATTN_WIDTH), lambda i: (i, 0)),
            pl.BlockSpec((tm, KV_WIDTH), lambda i: (i, 0)),
            pl.BlockSpec((tm, KV_WIDTH), lambda i: (i, 0)),
            pl.BlockSpec((tm, 2 * FOURIER_WIDTH), lambda i: (i, 0)),
        ],
        out_shape=[
            jax.ShapeDtypeStruct((t, ATTN_WIDTH), BF16),
            jax.ShapeDtypeStruct((t, KV_WIDTH), BF16),
            jax.ShapeDtypeStruct((t, KV_WIDTH), BF16),
            jax.ShapeDtypeStruct((t, 2 * FOURIER_WIDTH), BF16),
        ],
        compiler_params=_params("arbitrary"),
        name="in_proj",
    )(x2d, mod, g_pre, w_in, qn, kn, cos_t, sin_t, m_fold)


def _attn_kernel(*refs, n_seg):
    q_ref = refs[0]
    kv_refs = refs[1:1 + 2 * n_seg]
    g_ref, o_ref, acc_ref = refs[1 + 2 * n_seg:]
    for h in range(N_Q_HEADS):
        kv = (h // Q_PER_KV) * HEAD_DIM
        qh = q_ref[:, h * HEAD_DIM:(h + 1) * HEAD_DIM]
        scores = [
            lax.dot_general(qh, kv_refs[2 * i][:, kv:kv + HEAD_DIM], (((1,), (1,)), ((), ())),
                            preferred_element_type=F32)
            for i in range(n_seg)
        ]
        m = functools.reduce(jnp.maximum, [jnp.max(s, axis=-1, keepdims=True) for s in scores])
        probs = [jnp.exp(s - m) for s in scores]
        denom = functools.reduce(jnp.add, [jnp.sum(p, axis=-1, keepdims=True) for p in probs])
        out = functools.reduce(jnp.add, [
            jnp.dot(probs[i].astype(BF16), kv_refs[2 * i + 1][:, kv:kv + HEAD_DIM],
                    preferred_element_type=F32)
            for i in range(n_seg)
        ])
        acc_ref[:, h * HEAD_DIM:(h + 1) * HEAD_DIM] = out / denom
    o_ref[...] = _rms(acc_ref[...], g_ref[...]).astype(BF16)


def _attention(q, segments, g_attn, *, q_seq, tq):
    t = q.shape[0]
    per_seq = q_seq // tq
    in_specs = [pl.BlockSpec((tq, ATTN_WIDTH), lambda b, j: (b * per_seq + j, 0))]
    args = [q]
    for k, v, rows in segments:
        in_specs += [pl.BlockSpec((rows, KV_WIDTH), lambda b, j: (b, 0))] * 2
        args += [k, v]
    in_specs.append(_const_spec((1, ATTN_WIDTH)))
    args.append(g_attn)
    return pl.pallas_call(
        functools.partial(_attn_kernel, n_seg=len(segments)),
        grid=(t // q_seq, per_seq),
        in_specs=in_specs,
        out_specs=pl.BlockSpec((tq, ATTN_WIDTH), lambda b, j: (b * per_seq + j, 0)),
        out_shape=jax.ShapeDtypeStruct((t, ATTN_WIDTH), BF16),
        scratch_shapes=[pltpu.VMEM((tq, ATTN_WIDTH), F32)],
        compiler_params=_params("arbitrary", "arbitrary"),
        name="attention",
    )(*args)


def _pos_dft_kernel(c_ref, s_ref, a_ref, b_ref, g_ref, o_ref):
    y = (jnp.dot(c_ref[...], a_ref[...], preferred_element_type=F32)
         - jnp.dot(s_ref[...], b_ref[...], preferred_element_type=F32))
    o_ref[...] = _rms(y, g_ref[...]).astype(BF16)


def _pos_dft(cos_m, sin_m, ab, g_four, *, tm):
    seq = cos_m.shape[0]
    t = ab.shape[0]
    per_seq = seq // tm
    return pl.pallas_call(
        _pos_dft_kernel,
        grid=(t // seq, per_seq),
        in_specs=[
            pl.BlockSpec((tm, seq), lambda b, j: (j, 0)),
            pl.BlockSpec((tm, seq), lambda b, j: (j, 0)),
            pl.BlockSpec((seq, FOURIER_WIDTH), lambda b, j: (b, 0)),
            pl.BlockSpec((seq, FOURIER_WIDTH), lambda b, j: (b, 1)),
            _const_spec((1, FOURIER_WIDTH)),
        ],
        out_specs=pl.BlockSpec((tm, FOURIER_WIDTH), lambda b, j: (b * per_seq + j, 0)),
        out_shape=jax.ShapeDtypeStruct((t, FOURIER_WIDTH), BF16),
        compiler_params=_params("arbitrary", "arbitrary"),
        name="pos_dft",
    )(cos_m, sin_m, ab, ab, g_four)


def _out_proj_kernel(a_ref, f_ref, w_ref, x_ref, mod_ref, gpost_ref, gpre_ref, xo_ref, h_ref):
    mod = mod_ref[0]
    mix = (jnp.dot(a_ref[...], w_ref[:ATTN_WIDTH, :], preferred_element_type=F32)
           + jnp.dot(f_ref[...], w_ref[ATTN_WIDTH:, :], preferred_element_type=F32))
    xn = x_ref[...] + mod[2:3] * _rms(mix, gpost_ref[...])
    xo_ref[...] = xn
    h_ref[...] = (_rms(xn, gpre_ref[...]) * (1.0 + mod[4:5]) + mod[3:4]).astype(BF16)


def _out_proj(a, f, w_out, x2d, mod, g_post, g_pre_ffn, *, seq, tm):
    t, d = x2d.shape
    per_seq = seq // tm
    nb = mod.shape[0]
    mod_map = (lambda i: (i // per_seq, 0, 0)) if nb > 1 else (lambda i: (0, 0, 0))
    return pl.pallas_call(
        _out_proj_kernel,
        grid=(t // tm,),
        in_specs=[
            pl.BlockSpec((tm, ATTN_WIDTH), lambda i: (i, 0)),
            pl.BlockSpec((tm, FOURIER_WIDTH), lambda i: (i, 0)),
            _const_spec((ATTN_WIDTH + FOURIER_WIDTH, d)),
            pl.BlockSpec((tm, d), lambda i: (i, 0)),
            pl.BlockSpec((1, N_MOD, d), mod_map),
            _const_spec((1, d)),
            _const_spec((1, d)),
        ],
        out_specs=[
            pl.BlockSpec((tm, d), lambda i: (i, 0)),
            pl.BlockSpec((tm, d), lambda i: (i, 0)),
        ],
        out_shape=[
            jax.ShapeDtypeStruct((t, d), F32),
            jax.ShapeDtypeStruct((t, d), BF16),
        ],
        compiler_params=_params("arbitrary"),
        name="out_proj",
    )(a, f, w_out, x2d, mod, g_post, g_pre_ffn)


def _ffn_up_kernel(h_ref, wg_ref, wv_ref, cw_ref, cb_ref, o_ref, *, seg_len):
    h = h_ref[...]
    gate = jnp.dot(h, wg_ref[...], preferred_element_type=F32)
    val = jnp.dot(h, wv_ref[...], preferred_element_type=F32)
    rows = gate.shape[0]
    pos = lax.broadcasted_iota(jnp.int32, (rows, 1), 0) % seg_len
    prev = jnp.where(pos == 0, 0.0, pltpu.roll(gate, 1, axis=0))
    nxt = jnp.where(pos == seg_len - 1, 0.0, pltpu.roll(gate, rows - 1, axis=0))
    cw = cw_ref[...]
    c = prev * cw[0:1] + gate * cw[1:2] + nxt * cw[2:3] + cb_ref[...]
    act = 0.5 * c * (1.0 + jnp.tanh(math.sqrt(2.0 / math.pi) * (c + 0.044715 * (c * c * c))))
    o_ref[...] = (act * val).astype(BF16)


def _ffn_up(h, w_up, conv_w, conv_b, *, seg_len, tm, tn):
    t, d = h.shape
    ffn = conv_w.shape[1]
    n_col = ffn // tn
    return pl.pallas_call(
        functools.partial(_ffn_up_kernel, seg_len=seg_len),
        grid=(t // tm, n_col),
        in_specs=[
            pl.BlockSpec((tm, d), lambda i, j: (i, 0)),
            pl.BlockSpec((d, tn), lambda i, j: (0, j)),
            pl.BlockSpec((d, tn), lambda i, j: (0, j + n_col)),
            pl.BlockSpec((3, tn), lambda i, j: (0, j)),
            pl.BlockSpec((1, tn), lambda i, j: (0, j)),
        ],
        out_specs=pl.BlockSpec((tm, tn), lambda i, j: (i, j)),
        out_shape=jax.ShapeDtypeStruct((t, ffn), BF16),
        compiler_params=_params("arbitrary", "arbitrary"),
        name="ffn_up",
    )(h, w_up, w_up, conv_w, conv_b)


def _ffn_down_kernel(u_ref, w_ref, x_ref, mod_ref, gpost_ref, xo_ref, acc_ref):
    kk = pl.program_id(1)

    @pl.when(kk == 0)
    def _():
        acc_ref[...] = jnp.zeros_like(acc_ref)

    acc_ref[...] += jnp.dot(u_ref[...], w_ref[...], preferred_element_type=F32)

    @pl.when(kk == pl.num_programs(1) - 1)
    def _():
        mod = mod_ref[0]
        xo_ref[...] = x_ref[...] + mod[5:6] * _rms(acc_ref[...], gpost_ref[...])


def _ffn_down(u, w_down, x2d, mod, g_post, *, seq, tm, tk):
    t, d = x2d.shape
    ffn = u.shape[1]
    per_seq = seq // tm
    nb = mod.shape[0]
    mod_map = (lambda i, k: (i // per_seq, 0, 0)) if nb > 1 else (lambda i, k: (0, 0, 0))
    return pl.pallas_call(
        _ffn_down_kernel,
        grid=(t // tm, ffn // tk),
        in_specs=[
            pl.BlockSpec((tm, tk), lambda i, k: (i, k)),
            pl.BlockSpec((tk, d), lambda i, k: (k, 0)),
            pl.BlockSpec((tm, d), lambda i, k: (i, 0)),
            pl.BlockSpec((1, N_MOD, d), mod_map),
            _const_spec((1, d)),
        ],
        out_specs=pl.BlockSpec((tm, d), lambda i, k: (i, 0)),
        out_shape=jax.ShapeDtypeStruct((t, d), F32),
        scratch_shapes=[pltpu.VMEM((tm, d), F32)],
        compiler_params=_params("arbitrary", "arbitrary"),
        name="ffn_down",
    )(u, w_down, x2d, mod, g_post)


def _rope_tables(n):
    rows = n // GRID_W
    row = jnp.repeat(jnp.arange(rows), GRID_W).astype(F32)
    col = jnp.tile(jnp.arange(GRID_W), rows).astype(F32)
    n_pairs_axis = HEAD_DIM // 4
    freqs = ROPE_THETA ** (-jnp.arange(n_pairs_axis, dtype=F32) / n_pairs_axis)
    ang = jnp.concatenate([row[:, None] * freqs, col[:, None] * freqs], axis=-1)
    c, s = jnp.cos(ang), jnp.sin(ang)
    return jnp.concatenate([c, c], axis=-1), jnp.concatenate([-s, s], axis=-1)


def _dft_matrices(n):
    idx = jnp.arange(n, dtype=jnp.int32)
    kt = (idx[:, None] * idx[None, :]) % n
    ang = kt.astype(F32) * (2.0 * math.pi / n)
    scale = n ** -0.5
    return jnp.cos(ang) * scale, jnp.sin(ang) * scale


def kernel(x, c, ctx, c_ctx, w_mod, b_mod, g_pre_mix, g_post_mix, g_pre_ffn, g_post_ffn,
           w_in, q_norm, k_norm, w_four, g_attn_out, g_four_out, w_out,
           w_up, conv_w, conv_b, w_down):
    b, n, d = x.shape
    ctx_len = ctx.shape[1]
    depth = w_mod.shape[0]
    ffn = conv_w.shape[-1]

    perm = jnp.concatenate([jnp.arange(0, HEAD_DIM, 2), jnp.arange(1, HEAD_DIM, 2)])
    qk_cols = (jnp.arange(N_Q_HEADS + N_KV_HEADS)[:, None] * HEAD_DIM + perm[None, :]).reshape(-1)
    col_perm = jnp.concatenate([qk_cols, jnp.arange(K_END, IN_WIDTH)])
    w_in_b = w_in.astype(BF16)[:, :, col_perm]
    w_out_b = w_out.astype(BF16)
    w_up_b = w_up.astype(BF16)
    w_down_b = w_down.astype(BF16)
    qn = q_norm[:, perm].reshape(depth, 1, HEAD_DIM)
    kn = k_norm[:, perm].reshape(depth, 1, HEAD_DIM)
    cos_t, sin_t = _rope_tables(n)
    cos_n, sin_n = (m.astype(BF16) for m in _dft_matrices(n))
    cos_c, sin_c = (m.astype(BF16) for m in _dft_matrices(ctx_len))
    cs_ch = jnp.stack(_dft_matrices(FOURIER_GROUP))
    m_fold = _fold_channel_dft(cs_ch, w_four)

    cc = jnp.zeros((MOD_ROWS, d), F32).at[:b].set(c).at[b].set(c_ctx)
    mod_all = _mod_proj(cc, w_mod, b_mod).reshape(depth, MOD_ROWS, N_MOD, d)

    xl = x.reshape(b * n, d)
    xc = ctx.reshape(b * ctx_len, d)
    row = lambda v: v.reshape(1, -1)

    for i in range(depth):
        last = i == depth - 1
        mod_l = mod_all[i, :b]
        mod_c = mod_all[i, b:b + 1]

        q_l, k_l, v_l, ab_l = _in_proj(xl, mod_l, row(g_pre_mix[i]), w_in_b[i], qn[i], kn[i],
                                       cos_t, sin_t, m_fold[i], seq=n, rope=True, tm=512)
        q_c, k_c, v_c, ab_c = _in_proj(xc, mod_c, row(g_pre_mix[i]), w_in_b[i], qn[i], kn[i],
                                       cos_t, sin_t, m_fold[i], seq=ctx_len, rope=False, tm=256)

        attn_l = _attention(q_l, [(k_l, v_l, n), (k_c, v_c, ctx_len)], row(g_attn_out[i]),
                            q_seq=n, tq=512)
        four_l = _pos_dft(cos_n, sin_n, ab_l, row(g_four_out[i]), tm=512)
        xl, h_l = _out_proj(attn_l, four_l, w_out_b[i], xl, mod_l, row(g_post_mix[i]),
                            row(g_pre_ffn[i]), seq=n, tm=512)
        u_l = _ffn_up(h_l, w_up_b[i], conv_w[i], row(conv_b[i]), seg_len=n, tm=n, tn=512)
        xl = _ffn_down(u_l, w_down_b[i], xl, mod_l, row(g_post_ffn[i]), seq=n, tm=512, tk=ffn // 4)

        if not last:
            attn_c = _attention(q_c, [(k_c, v_c, ctx_len)], row(g_attn_out[i]),
                                q_seq=ctx_len, tq=ctx_len)
            four_c = _pos_dft(cos_c, sin_c, ab_c, row(g_four_out[i]), tm=ctx_len)
            xc, h_c = _out_proj(attn_c, four_c, w_out_b[i], xc, mod_c, row(g_post_mix[i]),
                                row(g_pre_ffn[i]), seq=ctx_len, tm=256)
            u_c = _ffn_up(h_c, w_up_b[i], conv_w[i], row(conv_b[i]), seg_len=ctx_len,
                          tm=b * ctx_len, tn=512)
            xc = _ffn_down(u_c, w_down_b[i], xc, mod_c, row(g_post_ffn[i]), seq=ctx_len,
                           tm=256, tk=ffn // 4)

    return xl.reshape(b, n, d)
```

```python
import functools
import math

import jax
import jax.numpy as jnp
from jax import lax
from jax.experimental import pallas as pl
from jax.experimental.pallas import tpu as pltpu

GRID_W = 64
HEAD_DIM = 128
N_Q_HEADS = 8
N_KV_HEADS = 2
Q_PER_KV = N_Q_HEADS // N_KV_HEADS
FOURIER_GROUP = 128
N_FOURIER_GROUPS = 8
ATTN_WIDTH = N_Q_HEADS * HEAD_DIM
KV_WIDTH = N_KV_HEADS * HEAD_DIM
FOURIER_WIDTH = N_FOURIER_GROUPS * FOURIER_GROUP
Q_END = ATTN_WIDTH
K_END = Q_END + KV_WIDTH
V_END = K_END + KV_WIDTH
IN_WIDTH = V_END + FOURIER_WIDTH
ROPE_THETA = 10000.0
EPS = 1e-6
N_MOD = 6
MOD_ROWS = 16

VMEM_LIMIT_BYTES = 56 * 1024 * 1024

F32 = jnp.float32
BF16 = jnp.bfloat16


def _params(*semantics):
    return pltpu.CompilerParams(dimension_semantics=semantics, vmem_limit_bytes=VMEM_LIMIT_BYTES)


def _rms(x, g):
    ms = jnp.mean(x * x, axis=-1, keepdims=True)
    return x * lax.rsqrt(ms + EPS) * g


def _const_spec(shape):
    return pl.BlockSpec(shape, lambda *_: (0,) * len(shape), pipeline_mode=pl.Buffered(1))


def _layer_spec(shape, layer):
    zeros = (0,) * len(shape)
    return pl.BlockSpec((None,) + shape, lambda *_: (layer,) + zeros, pipeline_mode=pl.Buffered(1))


def _mod_spec(d, layer, row_of):
    return pl.BlockSpec((None, 1, N_MOD, d), lambda i, *_: (layer, row_of(i), 0, 0))


def _mod_kernel(c_ref, w_ref, b_ref, o_ref):
    c = c_ref[...]
    s = (c * jax.nn.sigmoid(c)).astype(BF16)
    o_ref[0] = jnp.dot(s, w_ref[0].astype(BF16), preferred_element_type=F32) + b_ref[0]


def _mod_proj(cc, w_mod, b_mod, tn=1024):
    depth, d, width = w_mod.shape
    return pl.pallas_call(
        _mod_kernel,
        grid=(depth, width // tn),
        in_specs=[
            pl.BlockSpec((MOD_ROWS, d), lambda l, j: (0, 0)),
            pl.BlockSpec((1, d, tn), lambda l, j: (l, 0, j)),
            pl.BlockSpec((1, 1, tn), lambda l, j: (l, 0, j)),
        ],
        out_specs=pl.BlockSpec((1, MOD_ROWS, tn), lambda l, j: (l, 0, j)),
        out_shape=jax.ShapeDtypeStruct((depth, MOD_ROWS, width), F32),
        compiler_params=_params("arbitrary", "arbitrary"),
        name="mod_proj",
    )(cc, w_mod, b_mod.reshape(depth, 1, width))


def _fold_kernel(cs_ref, w_ref, o_ref):
    w = w_ref[0, 0]
    o_ref[0, 0, :, :FOURIER_GROUP] = jnp.dot(
        cs_ref[0], w, preferred_element_type=F32, precision=lax.Precision.HIGHEST).astype(BF16)
    o_ref[0, 0, :, FOURIER_GROUP:] = jnp.dot(
        cs_ref[1], w, preferred_element_type=F32, precision=lax.Precision.HIGHEST).astype(BF16)


def _fold_channel_dft(cs, w_four):
    depth, groups, c, _ = w_four.shape
    return pl.pallas_call(
        _fold_kernel,
        grid=(depth, groups),
        in_specs=[
            pl.BlockSpec((2, c, c), lambda l, g: (0, 0, 0)),
            pl.BlockSpec((1, 1, c, c), lambda l, g: (l, g, 0, 0)),
        ],
        out_specs=pl.BlockSpec((1, 1, c, 2 * c), lambda l, g: (l, g, 0, 0)),
        out_shape=jax.ShapeDtypeStruct((depth, groups, c, 2 * c), BF16),
        compiler_params=_params("arbitrary", "arbitrary"),
        name="fold_channel_dft",
    )(cs, w_four)


def _prep_w_in_kernel(w_ref, p_ref, o_ref):
    j = pl.program_id(1)
    w = w_ref[0].astype(BF16)

    @pl.when(j < N_Q_HEADS + N_KV_HEADS)
    def _():
        o_ref[0] = jnp.dot(w, p_ref[...], preferred_element_type=F32).astype(BF16)

    @pl.when(j >= N_Q_HEADS + N_KV_HEADS)
    def _():
        o_ref[0] = w


def _prep_w_in(w_in, perm_matrix):
    depth, d, width = w_in.shape
    return pl.pallas_call(
        _prep_w_in_kernel,
        grid=(depth, width // HEAD_DIM),
        in_specs=[
            pl.BlockSpec((1, d, HEAD_DIM), lambda l, j: (l, 0, j)),
            pl.BlockSpec((HEAD_DIM, HEAD_DIM), lambda l, j: (0, 0)),
        ],
        out_specs=pl.BlockSpec((1, d, HEAD_DIM), lambda l, j: (l, 0, j)),
        out_shape=jax.ShapeDtypeStruct((depth, d, width), BF16),
        compiler_params=_params("arbitrary", "arbitrary"),
        name="prep_w_in",
    )(w_in, perm_matrix)


def _in_proj_kernel(x_ref, mod_ref, gpre_ref, w_ref, qn_ref, kn_ref, cos_ref, sin_ref, m_ref,
                    q_ref, k_ref, v_ref, ab_ref, *, rope):
    mod = mod_ref[0]
    h = _rms(x_ref[...], gpre_ref[...]) * (1.0 + mod[1:2]) + mod[0:1]
    p = jnp.dot(h.astype(BF16), w_ref[...], preferred_element_type=F32)

    def head(col, gain, post_scale):
        y = _rms(p[:, col:col + HEAD_DIM], gain)
        if rope:
            y = y * cos_ref[...] + pltpu.roll(y, HEAD_DIM // 2, axis=1) * sin_ref[...]
        if post_scale != 1.0:
            y = y * post_scale
        return y.astype(BF16)

    for i in range(N_Q_HEADS):
        q_ref[:, i * HEAD_DIM:(i + 1) * HEAD_DIM] = head(i * HEAD_DIM, qn_ref[...], HEAD_DIM ** -0.5)
    for i in range(N_KV_HEADS):
        k_ref[:, i * HEAD_DIM:(i + 1) * HEAD_DIM] = head(Q_END + i * HEAD_DIM, kn_ref[...], 1.0)
    v_ref[...] = p[:, K_END:V_END].astype(BF16)
    for g in range(N_FOURIER_GROUPS):
        lo = V_END + g * FOURIER_GROUP
        ab = jnp.dot(p[:, lo:lo + FOURIER_GROUP].astype(BF16), m_ref[g], preferred_element_type=F32)
        ab_ref[:, g * FOURIER_GROUP:(g + 1) * FOURIER_GROUP] = ab[:, :FOURIER_GROUP].astype(BF16)
        ab_ref[:, FOURIER_WIDTH + g * FOURIER_GROUP:FOURIER_WIDTH + (g + 1) * FOURIER_GROUP] = (
            ab[:, FOURIER_GROUP:].astype(BF16))


def _in_proj(x2d, mod_all, g_pre, w_in, qn, kn, cos_t, sin_t, m_fold, *, layer, mod_row, seq, rope, tm):
    t, d = x2d.shape
    per_seq = seq // tm
    rope_map = (lambda i: (i % per_seq, 0)) if rope else (lambda i: (0, 0))
    return pl.pallas_call(
        functools.partial(_in_proj_kernel, rope=rope),
        grid=(t // tm,),
        in_specs=[
            pl.BlockSpec((tm, d), lambda i: (i, 0)),
            _mod_spec(d, layer, mod_row),
            _layer_spec((1, d), layer),
            _layer_spec((d, IN_WIDTH), layer),
            _layer_spec((1, HEAD_DIM), layer),
            _layer_spec((1, HEAD_DIM), layer),
            pl.BlockSpec((tm, HEAD_DIM), rope_map),
            pl.BlockSpec((tm, HEAD_DIM), rope_map),
            _layer_spec((N_FOURIER_GROUPS, FOURIER_GROUP, 2 * FOURIER_GROUP), layer),
        ],
        out_specs=[
            pl.BlockSpec((tm, ATTN_WIDTH), lambda i: (i, 0)),
            pl.BlockSpec((tm, KV_WIDTH), lambda i: (i, 0)),
            pl.BlockSpec((tm, KV_WIDTH), lambda i: (i, 0)),
            pl.BlockSpec((tm, 2 * FOURIER_WIDTH), lambda i: (i, 0)),
        ],
        out_shape=[
            jax.ShapeDtypeStruct((t, ATTN_WIDTH), BF16),
            jax.ShapeDtypeStruct((t, KV_WIDTH), BF16),
            jax.ShapeDtypeStruct((t, KV_WIDTH), BF16),
            jax.ShapeDtypeStruct((t, 2 * FOURIER_WIDTH), BF16),
        ],
        compiler_params=_params("arbitrary"),
        name="in_proj",
    )(x2d, mod_all, g_pre, w_in, qn, kn, cos_t, sin_t, m_fold)


def _attn_kernel(*refs, n_seg):
    q_ref = refs[0]
    kv_refs = refs[1:1 + 2 * n_seg]
    g_ref, o_ref, acc_ref = refs[1 + 2 * n_seg:]
    for h in range(N_Q_HEADS):
        kv = (h // Q_PER_KV) * HEAD_DIM
        qh = q_ref[:, h * HEAD_DIM:(h + 1) * HEAD_DIM]
        scores = [
            lax.dot_general(qh, kv_refs[2 * i][:, kv:kv + HEAD_DIM], (((1,), (1,)), ((), ())),
                            preferred_element_type=F32)
            for i in range(n_seg)
        ]
        m = functools.reduce(jnp.maximum, [jnp.max(s, axis=-1, keepdims=True) for s in scores])
        probs = [jnp.exp(s - m) for s in scores]
        denom = functools.reduce(jnp.add, [jnp.sum(p, axis=-1, keepdims=True) for p in probs])
        out = functools.reduce(jnp.add, [
            jnp.dot(probs[i].astype(BF16), kv_refs[2 * i + 1][:, kv:kv + HEAD_DIM],
                    preferred_element_type=F32)
            for i in range(n_seg)
        ])
        acc_ref[:, h * HEAD_DIM:(h + 1) * HEAD_DIM] = out / denom
    o_ref[...] = _rms(acc_ref[...], g_ref[...]).astype(BF16)


def _attention(q, segments, g_attn, *, layer, q_seq, tq):
    t = q.shape[0]
    per_seq = q_seq // tq
    in_specs = [pl.BlockSpec((tq, ATTN_WIDTH), lambda b, j: (b * per_seq + j, 0))]
    args = [q]
    for k, v, rows in segments:
        in_specs += [pl.BlockSpec((rows, KV_WIDTH), lambda b, j: (b, 0))] * 2
        args += [k, v]
    in_specs.append(_layer_spec((1, ATTN_WIDTH), layer))
    args.append(g_attn)
    return pl.pallas_call(
        functools.partial(_attn_kernel, n_seg=len(segments)),
        grid=(t // q_seq, per_seq),
        in_specs=in_specs,
        out_specs=pl.BlockSpec((tq, ATTN_WIDTH), lambda b, j: (b * per_seq + j, 0)),
        out_shape=jax.ShapeDtypeStruct((t, ATTN_WIDTH), BF16),
        scratch_shapes=[pltpu.VMEM((tq, ATTN_WIDTH), F32)],
        compiler_params=_params("arbitrary", "arbitrary"),
        name="attention",
    )(*args)


def _pos_dft_kernel(c_ref, s_ref, a_ref, b_ref, g_ref, o_ref):
    y = (jnp.dot(c_ref[...], a_ref[...], preferred_element_type=F32)
         - jnp.dot(s_ref[...], b_ref[...], preferred_element_type=F32))
    o_ref[...] = _rms(y, g_ref[...]).astype(BF16)


def _pos_dft(cos_m, sin_m, ab, g_four, *, layer, tm):
    seq = cos_m.shape[0]
    t = ab.shape[0]
    per_seq = seq // tm
    return pl.pallas_call(
        _pos_dft_kernel,
        grid=(t // seq, per_seq),
        in_specs=[
            pl.BlockSpec((tm, seq), lambda b, j: (j, 0)),
            pl.BlockSpec((tm, seq), lambda b, j: (j, 0)),
            pl.BlockSpec((seq, FOURIER_WIDTH), lambda b, j: (b, 0)),
            pl.BlockSpec((seq, FOURIER_WIDTH), lambda b, j: (b, 1)),
            _layer_spec((1, FOURIER_WIDTH), layer),
        ],
        out_specs=pl.BlockSpec((tm, FOURIER_WIDTH), lambda b, j: (b * per_seq + j, 0)),
        out_shape=jax.ShapeDtypeStruct((t, FOURIER_WIDTH), BF16),
        compiler_params=_params("arbitrary", "arbitrary"),
        name="pos_dft",
    )(cos_m, sin_m, ab, ab, g_four)


def _out_proj_kernel(a_ref, f_ref, w_ref, x_ref, mod_ref, gpost_ref, gpre_ref, xo_ref, h_ref):
    mod = mod_ref[0]
    mix = (jnp.dot(a_ref[...], w_ref[:ATTN_WIDTH, :], preferred_element_type=F32)
           + jnp.dot(f_ref[...], w_ref[ATTN_WIDTH:, :], preferred_element_type=F32))
    xn = x_ref[...] + mod[2:3] * _rms(mix, gpost_ref[...])
    xo_ref[...] = xn
    h_ref[...] = (_rms(xn, gpre_ref[...]) * (1.0 + mod[4:5]) + mod[3:4]).astype(BF16)


def _out_proj(a, f, w_out, x2d, mod_all, g_post, g_pre_ffn, *, layer, mod_row, tm):
    t, d = x2d.shape
    return pl.pallas_call(
        _out_proj_kernel,
        grid=(t // tm,),
        in_specs=[
            pl.BlockSpec((tm, ATTN_WIDTH), lambda i: (i, 0)),
            pl.BlockSpec((tm, FOURIER_WIDTH), lambda i: (i, 0)),
            _layer_spec((ATTN_WIDTH + FOURIER_WIDTH, d), layer),
            pl.BlockSpec((tm, d), lambda i: (i, 0)),
            _mod_spec(d, layer, mod_row),
            _layer_spec((1, d), layer),
            _layer_spec((1, d), layer),
        ],
        out_specs=[
            pl.BlockSpec((tm, d), lambda i: (i, 0)),
            pl.BlockSpec((tm, d), lambda i: (i, 0)),
        ],
        out_shape=[
            jax.ShapeDtypeStruct((t, d), F32),
            jax.ShapeDtypeStruct((t, d), BF16),
        ],
        compiler_params=_params("arbitrary"),
        name="out_proj",
    )(a, f, w_out, x2d, mod_all, g_post, g_pre_ffn)


def _ffn_up_kernel(h_ref, wg_ref, wv_ref, cw_ref, cb_ref, o_ref, *, seg_len):
    h = h_ref[...]
    gate = jnp.dot(h, wg_ref[...], preferred_element_type=F32)
    val = jnp.dot(h, wv_ref[...], preferred_element_type=F32)
    rows = gate.shape[0]
    pos = lax.broadcasted_iota(jnp.int32, (rows, 1), 0) % seg_len
    prev = jnp.where(pos == 0, 0.0, pltpu.roll(gate, 1, axis=0))
    nxt = jnp.where(pos == seg_len - 1, 0.0, pltpu.roll(gate, rows - 1, axis=0))
    cw = cw_ref[...]
    c = prev * cw[0:1] + gate * cw[1:2] + nxt * cw[2:3] + cb_ref[...]
    act = 0.5 * c * (1.0 + jnp.tanh(math.sqrt(2.0 / math.pi) * (c + 0.044715 * (c * c * c))))
    o_ref[...] = (act * val).astype(BF16)


def _ffn_up(h, w_up, conv_w, conv_b, *, layer, seg_len, tm, tn):
    t, d = h.shape
    ffn = conv_w.shape[-1]
    n_col = ffn // tn
    return pl.pallas_call(
        functools.partial(_ffn_up_kernel, seg_len=seg_len),
        grid=(t // tm, n_col),
        in_specs=[
            pl.BlockSpec((tm, d), lambda i, j: (i, 0)),
            pl.BlockSpec((None, d, tn), lambda i, j: (layer, 0, j)),
            pl.BlockSpec((None, d, tn), lambda i, j: (layer, 0, j + n_col)),
            pl.BlockSpec((None, 3, tn), lambda i, j: (layer, 0, j)),
            pl.BlockSpec((None, 1, tn), lambda i, j: (layer, 0, j)),
        ],
        out_specs=pl.BlockSpec((tm, tn), lambda i, j: (i, j)),
        out_shape=jax.ShapeDtypeStruct((t, ffn), BF16),
        compiler_params=_params("arbitrary", "arbitrary"),
        name="ffn_up",
    )(h, w_up, w_up, conv_w, conv_b)


def _ffn_down_kernel(u_ref, w_ref, x_ref, mod_ref, gpost_ref, xo_ref, y_ref):
    j = pl.program_id(1)
    n_col, _, tn = y_ref.shape
    y_ref[j] = jnp.dot(u_ref[...], w_ref[...], preferred_element_type=F32)

    @pl.when(j == n_col - 1)
    def _():
        d = n_col * tn
        ssq = functools.reduce(
            jnp.add, [jnp.sum(y_ref[c] * y_ref[c], axis=-1, keepdims=True) for c in range(n_col)])
        inv = lax.rsqrt(ssq / d + EPS)
        gate = mod_ref[0][5:6]
        for c in range(n_col):
            cols = slice(c * tn, (c + 1) * tn)
            xo_ref[:, cols] = x_ref[:, cols] + gate[:, cols] * (y_ref[c] * inv * gpost_ref[:, cols])


def _ffn_down(u, w_down, x2d, mod_all, g_post, *, layer, mod_row, tm, tn):
    t, d = x2d.shape
    ffn = u.shape[1]
    n_col = d // tn
    return pl.pallas_call(
        _ffn_down_kernel,
        grid=(t // tm, n_col),
        in_specs=[
            pl.BlockSpec((tm, ffn), lambda i, j: (i, 0)),
            pl.BlockSpec((None, ffn, tn), lambda i, j: (layer, 0, j)),
            pl.BlockSpec((tm, d), lambda i, j: (i, 0)),
            _mod_spec(d, layer, mod_row),
            _layer_spec((1, d), layer),
        ],
        out_specs=pl.BlockSpec((tm, d), lambda i, j: (i, 0)),
        out_shape=jax.ShapeDtypeStruct((t, d), F32),
        scratch_shapes=[pltpu.VMEM((n_col, tm, tn), F32)],
        compiler_params=_params("arbitrary", "arbitrary"),
        name="ffn_down",
    )(u, w_down, x2d, mod_all, g_post)


def _rope_tables(n):
    rows = n // GRID_W
    row = jnp.repeat(jnp.arange(rows), GRID_W).astype(F32)
    col = jnp.tile(jnp.arange(GRID_W), rows).astype(F32)
    n_pairs_axis = HEAD_DIM // 4
    freqs = ROPE_THETA ** (-jnp.arange(n_pairs_axis, dtype=F32) / n_pairs_axis)
    ang = jnp.concatenate([row[:, None] * freqs, col[:, None] * freqs], axis=-1)
    c, s = jnp.cos(ang), jnp.sin(ang)
    return jnp.concatenate([c, c], axis=-1), jnp.concatenate([-s, s], axis=-1)


def _dft_matrices(n):
    idx = jnp.arange(n, dtype=jnp.int32)
    kt = (idx[:, None] * idx[None, :]) % n
    ang = kt.astype(F32) * (2.0 * math.pi / n)
    scale = n ** -0.5
    return jnp.cos(ang) * scale, jnp.sin(ang) * scale


def _deinterleave(v):
    return jnp.concatenate([v[..., 0::2], v[..., 1::2]], axis=-1)


def kernel(x, c, ctx, c_ctx, w_mod, b_mod, g_pre_mix, g_post_mix, g_pre_ffn, g_post_ffn,
           w_in, q_norm, k_norm, w_four, g_attn_out, g_four_out, w_out,
           w_up, conv_w, conv_b, w_down):
    b, n, d = x.shape
    ctx_len = ctx.shape[1]
    depth = w_mod.shape[0]

    perm_matrix = _deinterleave(jnp.eye(HEAD_DIM, dtype=BF16))
    w_in_b = _prep_w_in(w_in, perm_matrix)
    w_out_b = w_out.astype(BF16)
    w_up_b = w_up.astype(BF16)
    w_down_b = w_down.astype(BF16)
    qn = _deinterleave(q_norm).reshape(depth, 1, HEAD_DIM)
    kn = _deinterleave(k_norm).reshape(depth, 1, HEAD_DIM)
    cos_t, sin_t = _rope_tables(n)
    cos_n, sin_n = (m.astype(BF16) for m in _dft_matrices(n))
    cos_c, sin_c = (m.astype(BF16) for m in _dft_matrices(ctx_len))
    cs_ch = jnp.stack(_dft_matrices(FOURIER_GROUP))
    m_fold = _fold_channel_dft(cs_ch, w_four)
    rows3 = lambda v: v.reshape(depth, 1, -1)
    g_pre_mix, g_post_mix, g_pre_ffn, g_post_ffn, g_attn_out, g_four_out, conv_b = map(
        rows3, (g_pre_mix, g_post_mix, g_pre_ffn, g_post_ffn, g_attn_out, g_four_out, conv_b))

    cc = jnp.zeros((MOD_ROWS, d), F32).at[:b].set(c).at[b].set(c_ctx)
    mod_all = _mod_proj(cc, w_mod, b_mod).reshape(depth, MOD_ROWS, N_MOD, d)

    xl = x.reshape(b * n, d)
    xc = ctx.reshape(b * ctx_len, d)
    tm = 512
    lat_row = lambda i: i // (n // tm)
    ctx_row = lambda i: b

    for i in range(depth):
        last = i == depth - 1
        q_l, k_l, v_l, ab_l = _in_proj(xl, mod_all, g_pre_mix, w_in_b, qn, kn, cos_t, sin_t, m_fold,
                                       layer=i, mod_row=lat_row, seq=n, rope=True, tm=tm)
        q_c, k_c, v_c, ab_c = _in_proj(xc, mod_all, g_pre_mix, w_in_b, qn, kn, cos_t, sin_t, m_fold,
                                       layer=i, mod_row=ctx_row, seq=ctx_len, rope=False, tm=ctx_len)

        attn_l = _attention(q_l, [(k_l, v_l, n), (k_c, v_c, ctx_len)], g_attn_out,
                            layer=i, q_seq=n, tq=512)
        four_l = _pos_dft(cos_n, sin_n, ab_l, g_four_out, layer=i, tm=512)
        xl, h_l = _out_proj(attn_l, four_l, w_out_b, xl, mod_all, g_post_mix, g_pre_ffn,
                            layer=i, mod_row=lat_row, tm=tm)
        u_l = _ffn_up(h_l, w_up_b, conv_w, conv_b, layer=i, seg_len=n, tm=n, tn=512)
        xl = _ffn_down(u_l, w_down_b, xl, mod_all, g_post_ffn, layer=i, mod_row=lat_row, tm=tm, tn=512)

        if not last:
            attn_c = _attention(q_c, [(k_c, v_c, ctx_len)], g_attn_out,
                                layer=i, q_seq=ctx_len, tq=ctx_len)
            four_c = _pos_dft(cos_c, sin_c, ab_c, g_four_out, layer=i, tm=ctx_len)
            xc, h_c = _out_proj(attn_c, four_c, w_out_b, xc, mod_all, g_post_mix, g_pre_ffn,
                                layer=i, mod_row=ctx_row, tm=tm)
            u_c = _ffn_up(h_c, w_up_b, conv_w, conv_b, layer=i, seg_len=ctx_len,
                          tm=b * ctx_len, tn=512)
            xc = _ffn_down(u_c, w_down_b, xc, mod_all, g_post_ffn, layer=i, mod_row=ctx_row,
                           tm=tm, tn=512)

    return xl.reshape(b, n, d)
```

```python
import functools
import math

import jax
import jax.numpy as jnp
from jax import lax
from jax.experimental import pallas as pl
from jax.experimental.pallas import tpu as pltpu

GRID_W = 64
HEAD_DIM = 128
N_Q_HEADS = 8
N_KV_HEADS = 2
Q_PER_KV = N_Q_HEADS // N_KV_HEADS
FOURIER_GROUP = 128
N_FOURIER_GROUPS = 8
ATTN_WIDTH = N_Q_HEADS * HEAD_DIM
KV_WIDTH = N_KV_HEADS * HEAD_DIM
FOURIER_WIDTH = N_FOURIER_GROUPS * FOURIER_GROUP
Q_END = ATTN_WIDTH
K_END = Q_END + KV_WIDTH
V_END = K_END + KV_WIDTH
IN_WIDTH = V_END + FOURIER_WIDTH
ROPE_THETA = 10000.0
EPS = 1e-6
N_MOD = 6
MOD_ROWS = 16

VMEM_LIMIT_BYTES = 56 * 1024 * 1024

F32 = jnp.float32
BF16 = jnp.bfloat16


def _params(*semantics):
    return pltpu.CompilerParams(dimension_semantics=semantics, vmem_limit_bytes=VMEM_LIMIT_BYTES)


def _rms(x, g):
    ms = jnp.mean(x * x, axis=-1, keepdims=True)
    return x * lax.rsqrt(ms + EPS) * g


def _const_spec(shape):
    return pl.BlockSpec(shape, lambda *_: (0,) * len(shape), pipeline_mode=pl.Buffered(1))


def _layer_spec(shape, layer):
    zeros = (0,) * len(shape)
    return pl.BlockSpec((None,) + shape, lambda *_: (layer,) + zeros, pipeline_mode=pl.Buffered(1))


def _mod_spec(d, layer, row_of):
    return pl.BlockSpec((None, 1, N_MOD, d), lambda i, *_: (layer, row_of(i), 0, 0))


def _mod_kernel(c_ref, w_ref, b_ref, o_ref):
    c = c_ref[...]
    s = (c * jax.nn.sigmoid(c)).astype(BF16)
    o_ref[0] = jnp.dot(s, w_ref[0].astype(BF16), preferred_element_type=F32) + b_ref[0]


def _mod_proj(cc, w_mod, b_mod, tn=1024):
    depth, d, width = w_mod.shape
    return pl.pallas_call(
        _mod_kernel,
        grid=(depth, width // tn),
        in_specs=[
            pl.BlockSpec((MOD_ROWS, d), lambda l, j: (0, 0)),
            pl.BlockSpec((1, d, tn), lambda l, j: (l, 0, j)),
            pl.BlockSpec((1, 1, tn), lambda l, j: (l, 0, j)),
        ],
        out_specs=pl.BlockSpec((1, MOD_ROWS, tn), lambda l, j: (l, 0, j)),
        out_shape=jax.ShapeDtypeStruct((depth, MOD_ROWS, width), F32),
        compiler_params=_params("arbitrary", "arbitrary"),
        name="mod_proj",
    )(cc, w_mod, b_mod.reshape(depth, 1, width))


def _fold_kernel(cs_ref, w_ref, o_ref):
    w = w_ref[0, 0]
    o_ref[0, 0, :, :FOURIER_GROUP] = jnp.dot(
        cs_ref[0], w, preferred_element_type=F32, precision=lax.Precision.HIGHEST).astype(BF16)
    o_ref[0, 0, :, FOURIER_GROUP:] = jnp.dot(
        cs_ref[1], w, preferred_element_type=F32, precision=lax.Precision.HIGHEST).astype(BF16)


def _fold_channel_dft(cs, w_four):
    depth, groups, c, _ = w_four.shape
    return pl.pallas_call(
        _fold_kernel,
        grid=(depth, groups),
        in_specs=[
            pl.BlockSpec((2, c, c), lambda l, g: (0, 0, 0)),
            pl.BlockSpec((1, 1, c, c), lambda l, g: (l, g, 0, 0)),
        ],
        out_specs=pl.BlockSpec((1, 1, c, 2 * c), lambda l, g: (l, g, 0, 0)),
        out_shape=jax.ShapeDtypeStruct((depth, groups, c, 2 * c), BF16),
        compiler_params=_params("arbitrary", "arbitrary"),
        name="fold_channel_dft",
    )(cs, w_four)


def _prep_w_in_kernel(w_ref, p_ref, o_ref):
    j = pl.program_id(1)
    w = w_ref[0].astype(BF16)

    @pl.when(j < N_Q_HEADS + N_KV_HEADS)
    def _():
        o_ref[0] = jnp.dot(w, p_ref[...], preferred_element_type=F32).astype(BF16)

    @pl.when(j >= N_Q_HEADS + N_KV_HEADS)
    def _():
        o_ref[0] = w


def _prep_w_in(w_in, perm_matrix):
    depth, d, width = w_in.shape
    return pl.pallas_call(
        _prep_w_in_kernel,
        grid=(depth, width // HEAD_DIM),
        in_specs=[
            pl.BlockSpec((1, d, HEAD_DIM), lambda l, j: (l, 0, j)),
            pl.BlockSpec((HEAD_DIM, HEAD_DIM), lambda l, j: (0, 0)),
        ],
        out_specs=pl.BlockSpec((1, d, HEAD_DIM), lambda l, j: (l, 0, j)),
        out_shape=jax.ShapeDtypeStruct((depth, d, width), BF16),
        compiler_params=_params("arbitrary", "arbitrary"),
        name="prep_w_in",
    )(w_in, perm_matrix)


def _in_proj_kernel(x_ref, mod_ref, gpre_ref, w_ref, qn_ref, kn_ref, cos_ref, sin_ref, m_ref,
                    q_ref, k_ref, v_ref, ab_ref, *, rope, n_sub):
    mod = mod_ref[0]
    scale_g = (1.0 + mod[1:2]) * gpre_ref[...]
    rows = x_ref.shape[0] // n_sub
    for s in range(n_sub):
        r = slice(s * rows, (s + 1) * rows)
        h = _rms(x_ref[r, :], scale_g) + mod[0:1]
        p = jnp.dot(h.astype(BF16), w_ref[...], preferred_element_type=F32)

        def head(col, gain, post_scale):
            y = _rms(p[:, col:col + HEAD_DIM], gain)
            if rope:
                y = y * cos_ref[r, :] + pltpu.roll(y, HEAD_DIM // 2, axis=1) * sin_ref[r, :]
            if post_scale != 1.0:
                y = y * post_scale
            return y.astype(BF16)

        for i in range(N_Q_HEADS):
            q_ref[r, i * HEAD_DIM:(i + 1) * HEAD_DIM] = head(i * HEAD_DIM, qn_ref[...], HEAD_DIM ** -0.5)
        for i in range(N_KV_HEADS):
            k_ref[r, i * HEAD_DIM:(i + 1) * HEAD_DIM] = head(Q_END + i * HEAD_DIM, kn_ref[...], 1.0)
        v_ref[r, :] = p[:, K_END:V_END].astype(BF16)
        for g in range(N_FOURIER_GROUPS):
            lo = V_END + g * FOURIER_GROUP
            ab = jnp.dot(p[:, lo:lo + FOURIER_GROUP].astype(BF16), m_ref[g], preferred_element_type=F32)
            ab_ref[r, g * FOURIER_GROUP:(g + 1) * FOURIER_GROUP] = ab[:, :FOURIER_GROUP].astype(BF16)
            ab_ref[r, FOURIER_WIDTH + g * FOURIER_GROUP:FOURIER_WIDTH + (g + 1) * FOURIER_GROUP] = (
                ab[:, FOURIER_GROUP:].astype(BF16))


def _in_proj(x2d, mod_all, g_pre, w_in, qn, kn, cos_t, sin_t, m_fold, *, layer, mod_row, seq, rope, tm,
             n_sub=1):
    t, d = x2d.shape
    per_seq = seq // tm
    rope_map = (lambda i: (i % per_seq, 0)) if rope else (lambda i: (0, 0))
    return pl.pallas_call(
        functools.partial(_in_proj_kernel, rope=rope, n_sub=n_sub),
        grid=(t // tm,),
        in_specs=[
            pl.BlockSpec((tm, d), lambda i: (i, 0)),
            _mod_spec(d, layer, mod_row),
            _layer_spec((1, d), layer),
            _layer_spec((d, IN_WIDTH), layer),
            _layer_spec((1, HEAD_DIM), layer),
            _layer_spec((1, HEAD_DIM), layer),
            pl.BlockSpec((tm, HEAD_DIM), rope_map),
            pl.BlockSpec((tm, HEAD_DIM), rope_map),
            _layer_spec((N_FOURIER_GROUPS, FOURIER_GROUP, 2 * FOURIER_GROUP), layer),
        ],
        out_specs=[
            pl.BlockSpec((tm, ATTN_WIDTH), lambda i: (i, 0)),
            pl.BlockSpec((tm, KV_WIDTH), lambda i: (i, 0)),
            pl.BlockSpec((tm, KV_WIDTH), lambda i: (i, 0)),
            pl.BlockSpec((tm, 2 * FOURIER_WIDTH), lambda i: (i, 0)),
        ],
        out_shape=[
            jax.ShapeDtypeStruct((t, ATTN_WIDTH), BF16),
            jax.ShapeDtypeStruct((t, KV_WIDTH), BF16),
            jax.ShapeDtypeStruct((t, KV_WIDTH), BF16),
            jax.ShapeDtypeStruct((t, 2 * FOURIER_WIDTH), BF16),
        ],
        compiler_params=_params("arbitrary"),
        name="in_proj",
    )(x2d, mod_all, g_pre, w_in, qn, kn, cos_t, sin_t, m_fold)


def _attn_kernel(*refs, n_seg):
    q_ref = refs[0]
    kv_refs = refs[1:1 + 2 * n_seg]
    g_ref, o_ref, acc_ref = refs[1 + 2 * n_seg:]
    for h in range(N_Q_HEADS):
        kv = (h // Q_PER_KV) * HEAD_DIM
        qh = q_ref[:, h * HEAD_DIM:(h + 1) * HEAD_DIM]
        scores = [
            lax.dot_general(qh, kv_refs[2 * i][:, kv:kv + HEAD_DIM], (((1,), (1,)), ((), ())),
                            preferred_element_type=F32)
            for i in range(n_seg)
        ]
        m = functools.reduce(jnp.maximum, [jnp.max(s, axis=-1, keepdims=True) for s in scores])
        probs = [jnp.exp(s - m) for s in scores]
        denom = functools.reduce(jnp.add, [jnp.sum(p, axis=-1, keepdims=True) for p in probs])
        out = functools.reduce(jnp.add, [
            jnp.dot(probs[i].astype(BF16), kv_refs[2 * i + 1][:, kv:kv + HEAD_DIM],
                    preferred_element_type=F32)
            for i in range(n_seg)
        ])
        acc_ref[:, h * HEAD_DIM:(h + 1) * HEAD_DIM] = out / denom
    o_ref[...] = _rms(acc_ref[...], g_ref[...]).astype(BF16)


def _attention(q, segments, g_attn, *, layer, q_seq, tq):
    t = q.shape[0]
    per_seq = q_seq // tq
    in_specs = [pl.BlockSpec((tq, ATTN_WIDTH), lambda b, j: (b * per_seq + j, 0))]
    args = [q]
    for k, v, rows in segments:
        in_specs += [pl.BlockSpec((rows, KV_WIDTH), lambda b, j: (b, 0))] * 2
        args += [k, v]
    in_specs.append(_layer_spec((1, ATTN_WIDTH), layer))
    args.append(g_attn)
    return pl.pallas_call(
        functools.partial(_attn_kernel, n_seg=len(segments)),
        grid=(t // q_seq, per_seq),
        in_specs=in_specs,
        out_specs=pl.BlockSpec((tq, ATTN_WIDTH), lambda b, j: (b * per_seq + j, 0)),
        out_shape=jax.ShapeDtypeStruct((t, ATTN_WIDTH), BF16),
        scratch_shapes=[pltpu.VMEM((tq, ATTN_WIDTH), F32)],
        compiler_params=_params("arbitrary", "arbitrary"),
        name="attention",
    )(*args)


def _pos_dft_kernel(c_ref, s_ref, a_ref, b_ref, g_ref, o_ref):
    y = (jnp.dot(c_ref[...], a_ref[...], preferred_element_type=F32)
         - jnp.dot(s_ref[...], b_ref[...], preferred_element_type=F32))
    o_ref[...] = _rms(y, g_ref[...]).astype(BF16)


def _pos_dft(cos_m, sin_m, ab, g_four, *, layer, tm):
    seq = cos_m.shape[0]
    t = ab.shape[0]
    per_seq = seq // tm
    return pl.pallas_call(
        _pos_dft_kernel,
        grid=(t // seq, per_seq),
        in_specs=[
            pl.BlockSpec((tm, seq), lambda b, j: (j, 0)),
            pl.BlockSpec((tm, seq), lambda b, j: (j, 0)),
            pl.BlockSpec((seq, FOURIER_WIDTH), lambda b, j: (b, 0)),
            pl.BlockSpec((seq, FOURIER_WIDTH), lambda b, j: (b, 1)),
            _layer_spec((1, FOURIER_WIDTH), layer),
        ],
        out_specs=pl.BlockSpec((tm, FOURIER_WIDTH), lambda b, j: (b * per_seq + j, 0)),
        out_shape=jax.ShapeDtypeStruct((t, FOURIER_WIDTH), BF16),
        compiler_params=_params("arbitrary", "arbitrary"),
        name="pos_dft",
    )(cos_m, sin_m, ab, ab, g_four)


FLIP_BLOCK = 256
DFT_PAD_ROWS = 16


def _pos_dft_sym_kernel(alo_ref, ahi_ref, blo_ref, bhi_ref, c_ref, s_ref, q_ref, sgn_ref, g_ref,
                        o_ref, e_ref, d_ref, h_ref):
    half = alo_ref.shape[0]
    blk = q_ref.shape[0]
    nblk = half // blk
    qm = q_ref[...]
    is_row0 = lax.broadcasted_iota(jnp.int32, (blk, 1), 0) == 0

    def flipped_block(src_ref, bt, first_row):
        lo = (nblk - 1 - bt) * blk
        y = jnp.dot(qm, src_ref[lo:lo + blk, :], preferred_element_type=F32)
        if bt > 0:
            first_row = src_ref[lo + blk:lo + blk + 1, :].astype(F32)
        return jnp.where(is_row0, first_row, y)

    for bt in range(nblk):
        r = slice(bt * blk, (bt + 1) * blk)
        e_ref[r, :] = (alo_ref[r, :].astype(F32) + flipped_block(ahi_ref, bt, 0.0)).astype(BF16)
        d_ref[r, :] = (blo_ref[r, :].astype(F32) - flipped_block(bhi_ref, bt, 0.0)).astype(BF16)

    a_mid = ahi_ref[0:1, :].astype(F32)
    g = g_ref[...]
    mid_row = None
    for kb in range(nblk):
        r = slice(kb * blk, (kb + 1) * blk)
        rows = blk + (DFT_PAD_ROWS if kb == nblk - 1 else 0)
        cr = slice(kb * blk, kb * blk + rows)
        yc = jnp.dot(c_ref[cr, :], e_ref[...], preferred_element_type=F32) + sgn_ref[cr, :] * a_mid
        ys = jnp.dot(s_ref[r, :], d_ref[...], preferred_element_type=F32)
        o_ref[r, :] = _rms(yc[:blk] - ys, g).astype(BF16)
        h_ref[r, :] = _rms(yc[:blk] + ys, g).astype(BF16)
        if kb == nblk - 1:
            mid_row = _rms(yc[blk:blk + 1], g)

    for bs in range(nblk):
        r = slice(half + bs * blk, half + (bs + 1) * blk)
        o_ref[r, :] = flipped_block(h_ref, bs, mid_row).astype(BF16)


def _pos_dft_sym(tables, ab, g_four, *, layer):
    cos_m, sin_m, flip_m, sgn = tables
    half = sin_m.shape[0]
    t = ab.shape[0]
    blocks = lambda row, col: pl.BlockSpec((half, FOURIER_WIDTH), lambda b: (2 * b + row, col))
    return pl.pallas_call(
        _pos_dft_sym_kernel,
        grid=(t // (2 * half),),
        in_specs=[
            blocks(0, 0), blocks(1, 0), blocks(0, 1), blocks(1, 1),
            _const_spec(cos_m.shape),
            _const_spec(sin_m.shape),
            _const_spec(flip_m.shape),
            _const_spec(sgn.shape),
            _layer_spec((1, FOURIER_WIDTH), layer),
        ],
        out_specs=pl.BlockSpec((2 * half, FOURIER_WIDTH), lambda b: (b, 0)),
        out_shape=jax.ShapeDtypeStruct((t, FOURIER_WIDTH), BF16),
        scratch_shapes=[pltpu.VMEM((half, FOURIER_WIDTH), BF16)] * 3,
        compiler_params=_params("arbitrary"),
        name="pos_dft_sym",
    )(ab, ab, ab, ab, cos_m, sin_m, flip_m, sgn, g_four)


def _out_proj_kernel(a_ref, f_ref, w_ref, x_ref, mod_ref, gpost_ref, gpre_ref, xo_ref, h_ref, *, n_sub):
    mod = mod_ref[0]
    gate_g = mod[2:3] * gpost_ref[...]
    scale_g = (1.0 + mod[4:5]) * gpre_ref[...]
    rows = x_ref.shape[0] // n_sub
    for s in range(n_sub):
        r = slice(s * rows, (s + 1) * rows)
        mix = (jnp.dot(a_ref[r, :], w_ref[:ATTN_WIDTH, :], preferred_element_type=F32)
               + jnp.dot(f_ref[r, :], w_ref[ATTN_WIDTH:, :], preferred_element_type=F32))
        xn = x_ref[r, :] + _rms(mix, gate_g)
        xo_ref[r, :] = xn
        h_ref[r, :] = (_rms(xn, scale_g) + mod[3:4]).astype(BF16)


def _out_proj(a, f, w_out, x2d, mod_all, g_post, g_pre_ffn, *, layer, mod_row, tm, n_sub=4):
    t, d = x2d.shape
    return pl.pallas_call(
        functools.partial(_out_proj_kernel, n_sub=n_sub),
        grid=(t // tm,),
        in_specs=[
            pl.BlockSpec((tm, ATTN_WIDTH), lambda i: (i, 0)),
            pl.BlockSpec((tm, FOURIER_WIDTH), lambda i: (i, 0)),
            _layer_spec((ATTN_WIDTH + FOURIER_WIDTH, d), layer),
            pl.BlockSpec((tm, d), lambda i: (i, 0)),
            _mod_spec(d, layer, mod_row),
            _layer_spec((1, d), layer),
            _layer_spec((1, d), layer),
        ],
        out_specs=[
            pl.BlockSpec((tm, d), lambda i: (i, 0)),
            pl.BlockSpec((tm, d), lambda i: (i, 0)),
        ],
        out_shape=[
            jax.ShapeDtypeStruct((t, d), F32),
            jax.ShapeDtypeStruct((t, d), BF16),
        ],
        compiler_params=_params("arbitrary"),
        name="out_proj",
    )(a, f, w_out, x2d, mod_all, g_post, g_pre_ffn)


def _ffn_up_kernel(h_ref, wg_ref, wv_ref, cw_ref, cb_ref, o_ref, *, seg_len, n_sub):
    h = h_ref[...]
    rows, tn = o_ref.shape
    width = tn // n_sub
    pos = lax.broadcasted_iota(jnp.int32, (rows, 1), 0) % seg_len
    first, final = pos == 0, pos == seg_len - 1
    for s in range(n_sub):
        cols = slice(s * width, (s + 1) * width)
        gate = jnp.dot(h, wg_ref[:, cols], preferred_element_type=F32)
        val = jnp.dot(h, wv_ref[:, cols], preferred_element_type=F32)
        prev = jnp.where(first, 0.0, pltpu.roll(gate, 1, axis=0))
        nxt = jnp.where(final, 0.0, pltpu.roll(gate, rows - 1, axis=0))
        cw = cw_ref[:, cols]
        c = prev * cw[0:1] + gate * cw[1:2] + nxt * cw[2:3] + cb_ref[:, cols]
        act = 0.5 * c * (1.0 + jnp.tanh(math.sqrt(2.0 / math.pi) * (c + 0.044715 * (c * c * c))))
        o_ref[:, cols] = (act * val).astype(BF16)


def _ffn_up(h, w_up, conv_w, conv_b, *, layer, seg_len, tm, tn, n_sub=1):
    t, d = h.shape
    ffn = conv_w.shape[-1]
    n_col = ffn // tn
    return pl.pallas_call(
        functools.partial(_ffn_up_kernel, seg_len=seg_len, n_sub=n_sub),
        grid=(t // tm, n_col),
        in_specs=[
            pl.BlockSpec((tm, d), lambda i, j: (i, 0)),
            pl.BlockSpec((None, d, tn), lambda i, j: (layer, 0, j)),
            pl.BlockSpec((None, d, tn), lambda i, j: (layer, 0, j + n_col)),
            pl.BlockSpec((None, 3, tn), lambda i, j: (layer, 0, j)),
            pl.BlockSpec((None, 1, tn), lambda i, j: (layer, 0, j)),
        ],
        out_specs=pl.BlockSpec((tm, tn), lambda i, j: (i, j)),
        out_shape=jax.ShapeDtypeStruct((t, ffn), BF16),
        compiler_params=_params("arbitrary", "arbitrary"),
        name="ffn_up",
    )(h, w_up, w_up, conv_w, conv_b)


def _ffn_down_kernel(u_ref, w_ref, x_ref, mod_ref, gpost_ref, xo_ref, y_ref):
    j = pl.program_id(1)
    n_col, _, tn = y_ref.shape
    y_ref[j] = jnp.dot(u_ref[...], w_ref[...], preferred_element_type=F32)

    @pl.when(j == n_col - 1)
    def _():
        d = n_col * tn
        ssq = functools.reduce(
            jnp.add, [jnp.sum(y_ref[c] * y_ref[c], axis=-1, keepdims=True) for c in range(n_col)])
        inv = lax.rsqrt(ssq / d + EPS)
        gate = mod_ref[0][5:6]
        for c in range(n_col):
            cols = slice(c * tn, (c + 1) * tn)
            xo_ref[:, cols] = x_ref[:, cols] + gate[:, cols] * (y_ref[c] * inv * gpost_ref[:, cols])


def _ffn_down(u, w_down, x2d, mod_all, g_post, *, layer, mod_row, tm, tn):
    t, d = x2d.shape
    ffn = u.shape[1]
    n_col = d // tn
    return pl.pallas_call(
        _ffn_down_kernel,
        grid=(t // tm, n_col),
        in_specs=[
            pl.BlockSpec((tm, ffn), lambda i, j: (i, 0)),
            pl.BlockSpec((None, ffn, tn), lambda i, j: (layer, 0, j)),
            pl.BlockSpec((tm, d), lambda i, j: (i, 0)),
            _mod_spec(d, layer, mod_row),
            _layer_spec((1, d), layer),
        ],
        out_specs=pl.BlockSpec((tm, d), lambda i, j: (i, 0)),
        out_shape=jax.ShapeDtypeStruct((t, d), F32),
        scratch_shapes=[pltpu.VMEM((n_col, tm, tn), F32)],
        compiler_params=_params("arbitrary", "arbitrary"),
        name="ffn_down",
    )(u, w_down, x2d, mod_all, g_post)


def _rope_tables(n):
    rows = n // GRID_W
    row = jnp.repeat(jnp.arange(rows), GRID_W).astype(F32)
    col = jnp.tile(jnp.arange(GRID_W), rows).astype(F32)
    n_pairs_axis = HEAD_DIM // 4
    freqs = ROPE_THETA ** (-jnp.arange(n_pairs_axis, dtype=F32) / n_pairs_axis)
    ang = jnp.concatenate([row[:, None] * freqs, col[:, None] * freqs], axis=-1)
    c, s = jnp.cos(ang), jnp.sin(ang)
    return jnp.concatenate([c, c], axis=-1), jnp.concatenate([-s, s], axis=-1)


def _dft_matrices(n):
    idx = jnp.arange(n, dtype=jnp.int32)
    kt = (idx[:, None] * idx[None, :]) % n
    ang = kt.astype(F32) * (2.0 * math.pi / n)
    scale = n ** -0.5
    return jnp.cos(ang) * scale, jnp.sin(ang) * scale


def _sym_dft_tables(n):
    half = n // 2
    k = jnp.arange(half + DFT_PAD_ROWS, dtype=jnp.int32)[:, None]
    t = jnp.arange(half, dtype=jnp.int32)[None, :]
    ang = ((k * t) % n).astype(F32) * (2.0 * math.pi / n)
    scale = n ** -0.5
    valid = k <= half
    cos_m = jnp.where(valid, jnp.cos(ang) * scale, 0.0).astype(BF16)
    sin_m = (jnp.sin(ang[:half]) * scale).astype(BF16)
    sgn = jnp.where(valid, (1 - 2 * (k % 2)).astype(F32) * scale, 0.0)
    i = jnp.arange(FLIP_BLOCK, dtype=jnp.int32)
    flip_m = (i[:, None] + i[None, :] == FLIP_BLOCK).astype(BF16)
    return cos_m, sin_m, flip_m, sgn


def _deinterleave(v):
    return jnp.concatenate([v[..., 0::2], v[..., 1::2]], axis=-1)


def kernel(x, c, ctx, c_ctx, w_mod, b_mod, g_pre_mix, g_post_mix, g_pre_ffn, g_post_ffn,
           w_in, q_norm, k_norm, w_four, g_attn_out, g_four_out, w_out,
           w_up, conv_w, conv_b, w_down):
    b, n, d = x.shape
    ctx_len = ctx.shape[1]
    depth = w_mod.shape[0]

    perm_matrix = _deinterleave(jnp.eye(HEAD_DIM, dtype=BF16))
    w_in_b = _prep_w_in(w_in, perm_matrix)
    w_out_b = w_out.astype(BF16)
    w_up_b = w_up.astype(BF16)
    w_down_b = w_down.astype(BF16)
    qn = _deinterleave(q_norm).reshape(depth, 1, HEAD_DIM)
    kn = _deinterleave(k_norm).reshape(depth, 1, HEAD_DIM)
    cos_t, sin_t = _rope_tables(n)
    dft_tables = _sym_dft_tables(n)
    cos_c, sin_c = (m.astype(BF16) for m in _dft_matrices(ctx_len))
    cs_ch = jnp.stack(_dft_matrices(FOURIER_GROUP))
    m_fold = _fold_channel_dft(cs_ch, w_four)
    rows3 = lambda v: v.reshape(depth, 1, -1)
    g_pre_mix, g_post_mix, g_pre_ffn, g_post_ffn, g_attn_out, g_four_out, conv_b = map(
        rows3, (g_pre_mix, g_post_mix, g_pre_ffn, g_post_ffn, g_attn_out, g_four_out, conv_b))

    cc = jnp.zeros((MOD_ROWS, d), F32).at[:b].set(c).at[b].set(c_ctx)
    mod_all = _mod_proj(cc, w_mod, b_mod).reshape(depth, MOD_ROWS, N_MOD, d)

    xl = x.reshape(b * n, d)
    xc = ctx.reshape(b * ctx_len, d)
    tm = 512
    lat_row = lambda i: i // (n // tm)
    ctx_row = lambda i: b

    for i in range(depth):
        last = i == depth - 1
        q_l, k_l, v_l, ab_l = _in_proj(xl, mod_all, g_pre_mix, w_in_b, qn, kn, cos_t, sin_t, m_fold,
                                       layer=i, mod_row=lat_row, seq=n, rope=True, tm=tm)
        q_c, k_c, v_c, ab_c = _in_proj(xc, mod_all, g_pre_mix, w_in_b, qn, kn, cos_t, sin_t, m_fold,
                                       layer=i, mod_row=ctx_row, seq=ctx_len, rope=False, tm=ctx_len)

        attn_l = _attention(q_l, [(k_l, v_l, n), (k_c, v_c, ctx_len)], g_attn_out,
                            layer=i, q_seq=n, tq=512)
        four_l = _pos_dft_sym(dft_tables, ab_l, g_four_out, layer=i)
        xl, h_l = _out_proj(attn_l, four_l, w_out_b, xl, mod_all, g_post_mix, g_pre_ffn,
                            layer=i, mod_row=lat_row, tm=tm)
        u_l = _ffn_up(h_l, w_up_b, conv_w, conv_b, layer=i, seg_len=n, tm=n, tn=512)
        xl = _ffn_down(u_l, w_down_b, xl, mod_all, g_post_ffn, layer=i, mod_row=lat_row, tm=tm, tn=512)

        if not last:
            attn_c = _attention(q_c, [(k_c, v_c, ctx_len)], g_attn_out,
                                layer=i, q_seq=ctx_len, tq=ctx_len)
            four_c = _pos_dft(cos_c, sin_c, ab_c, g_four_out, layer=i, tm=ctx_len)
            xc, h_c = _out_proj(attn_c, four_c, w_out_b, xc, mod_all, g_post_mix, g_pre_ffn,
                                layer=i, mod_row=ctx_row, tm=tm)
            u_c = _ffn_up(h_c, w_up_b, conv_w, conv_b, layer=i, seg_len=ctx_len,
                          tm=b * ctx_len, tn=512)
            xc = _ffn_down(u_c, w_down_b, xc, mod_all, g_post_ffn, layer=i, mod_row=ctx_row,
                           tm=tm, tn=512)

    return xl.reshape(b, n, d)
```

```python
import functools
import math

import jax
import jax.numpy as jnp
from jax import lax
from jax.experimental import pallas as pl
from jax.experimental.pallas import tpu as pltpu

GRID_W = 64
HEAD_DIM = 128
N_Q_HEADS = 8
N_KV_HEADS = 2
Q_PER_KV = N_Q_HEADS // N_KV_HEADS
FOURIER_GROUP = 128
N_FOURIER_GROUPS = 8
ATTN_WIDTH = N_Q_HEADS * HEAD_DIM
KV_WIDTH = N_KV_HEADS * HEAD_DIM
FOURIER_WIDTH = N_FOURIER_GROUPS * FOURIER_GROUP
Q_END = ATTN_WIDTH
K_END = Q_END + KV_WIDTH
V_END = K_END + KV_WIDTH
IN_WIDTH = V_END + FOURIER_WIDTH
ROPE_THETA = 10000.0
EPS = 1e-6
N_MOD = 6
MOD_ROWS = 16

VMEM_LIMIT_BYTES = 56 * 1024 * 1024

F32 = jnp.float32
BF16 = jnp.bfloat16


def _params(*semantics):
    return pltpu.CompilerParams(dimension_semantics=semantics, vmem_limit_bytes=VMEM_LIMIT_BYTES)


def _rms(x, g):
    ms = jnp.mean(x * x, axis=-1, keepdims=True)
    return x * lax.rsqrt(ms + EPS) * g


def _const_spec(shape):
    return pl.BlockSpec(shape, lambda *_: (0,) * len(shape), pipeline_mode=pl.Buffered(1))


def _layer_spec(shape, layer):
    zeros = (0,) * len(shape)
    return pl.BlockSpec((None,) + shape, lambda *_: (layer,) + zeros, pipeline_mode=pl.Buffered(1))


def _mod_spec(d, layer, row_of):
    return pl.BlockSpec((None, 1, N_MOD, d), lambda i, *_: (layer, row_of(i), 0, 0))


def _mod_kernel(c_ref, w_ref, b_ref, o_ref):
    c = c_ref[...]
    s = (c * jax.nn.sigmoid(c)).astype(BF16)
    o_ref[0] = jnp.dot(s, w_ref[0].astype(BF16), preferred_element_type=F32) + b_ref[0]


def _mod_proj(cc, w_mod, b_mod, tn=1024):
    depth, d, width = w_mod.shape
    return pl.pallas_call(
        _mod_kernel,
        grid=(depth, width // tn),
        in_specs=[
            pl.BlockSpec((MOD_ROWS, d), lambda l, j: (0, 0)),
            pl.BlockSpec((1, d, tn), lambda l, j: (l, 0, j)),
            pl.BlockSpec((1, 1, tn), lambda l, j: (l, 0, j)),
        ],
        out_specs=pl.BlockSpec((1, MOD_ROWS, tn), lambda l, j: (l, 0, j)),
        out_shape=jax.ShapeDtypeStruct((depth, MOD_ROWS, width), F32),
        compiler_params=_params("arbitrary", "arbitrary"),
        name="mod_proj",
    )(cc, w_mod, b_mod.reshape(depth, 1, width))


def _fold_kernel(cs_ref, w_ref, o_ref):
    w = w_ref[0, 0]
    o_ref[0, 0, :, :FOURIER_GROUP] = jnp.dot(
        cs_ref[0], w, preferred_element_type=F32, precision=lax.Precision.HIGHEST).astype(BF16)
    o_ref[0, 0, :, FOURIER_GROUP:] = jnp.dot(
        cs_ref[1], w, preferred_element_type=F32, precision=lax.Precision.HIGHEST).astype(BF16)


def _fold_channel_dft(cs, w_four):
    depth, groups, c, _ = w_four.shape
    return pl.pallas_call(
        _fold_kernel,
        grid=(depth, groups),
        in_specs=[
            pl.BlockSpec((2, c, c), lambda l, g: (0, 0, 0)),
            pl.BlockSpec((1, 1, c, c), lambda l, g: (l, g, 0, 0)),
        ],
        out_specs=pl.BlockSpec((1, 1, c, 2 * c), lambda l, g: (l, g, 0, 0)),
        out_shape=jax.ShapeDtypeStruct((depth, groups, c, 2 * c), BF16),
        compiler_params=_params("arbitrary", "arbitrary"),
        name="fold_channel_dft",
    )(cs, w_four)


def _prep_w_in_kernel(w_ref, p_ref, o_ref):
    j = pl.program_id(1)
    w = w_ref[0].astype(BF16)

    @pl.when(j < N_Q_HEADS + N_KV_HEADS)
    def _():
        o_ref[0] = jnp.dot(w, p_ref[...], preferred_element_type=F32).astype(BF16)

    @pl.when(j >= N_Q_HEADS + N_KV_HEADS)
    def _():
        o_ref[0] = w


def _prep_w_in(w_in, perm_matrix):
    depth, d, width = w_in.shape
    return pl.pallas_call(
        _prep_w_in_kernel,
        grid=(depth, width // HEAD_DIM),
        in_specs=[
            pl.BlockSpec((1, d, HEAD_DIM), lambda l, j: (l, 0, j)),
            pl.BlockSpec((HEAD_DIM, HEAD_DIM), lambda l, j: (0, 0)),
        ],
        out_specs=pl.BlockSpec((1, d, HEAD_DIM), lambda l, j: (l, 0, j)),
        out_shape=jax.ShapeDtypeStruct((depth, d, width), BF16),
        compiler_params=_params("arbitrary", "arbitrary"),
        name="prep_w_in",
    )(w_in, perm_matrix)


def _in_proj_kernel(x_ref, mod_ref, gpre_ref, w_ref, qn_ref, kn_ref, cos_ref, sin_ref, m_ref,
                    q_ref, k_ref, v_ref, ab_ref, *, rope, n_sub):
    mod = mod_ref[0]
    scale_g = (1.0 + mod[1:2]) * gpre_ref[...]
    rows = x_ref.shape[0] // n_sub
    for s in range(n_sub):
        r = slice(s * rows, (s + 1) * rows)
        h = _rms(x_ref[r, :], scale_g) + mod[0:1]
        p = jnp.dot(h.astype(BF16), w_ref[...], preferred_element_type=F32)

        def head(col, gain, post_scale):
            y = _rms(p[:, col:col + HEAD_DIM], gain)
            if rope:
                y = y * cos_ref[r, :] + pltpu.roll(y, HEAD_DIM // 2, axis=1) * sin_ref[r, :]
            if post_scale != 1.0:
                y = y * post_scale
            return y.astype(BF16)

        for i in range(N_Q_HEADS):
            q_ref[r, i * HEAD_DIM:(i + 1) * HEAD_DIM] = head(i * HEAD_DIM, qn_ref[...], HEAD_DIM ** -0.5)
        for i in range(N_KV_HEADS):
            k_ref[r, i * HEAD_DIM:(i + 1) * HEAD_DIM] = head(Q_END + i * HEAD_DIM, kn_ref[...], 1.0)
        v_ref[r, :] = p[:, K_END:V_END].astype(BF16)
        for g in range(N_FOURIER_GROUPS):
            lo = V_END + g * FOURIER_GROUP
            ab = jnp.dot(p[:, lo:lo + FOURIER_GROUP].astype(BF16), m_ref[g], preferred_element_type=F32)
            ab_ref[r, g * FOURIER_GROUP:(g + 1) * FOURIER_GROUP] = ab[:, :FOURIER_GROUP].astype(BF16)
            ab_ref[r, FOURIER_WIDTH + g * FOURIER_GROUP:FOURIER_WIDTH + (g + 1) * FOURIER_GROUP] = (
                ab[:, FOURIER_GROUP:].astype(BF16))


def _in_proj(x2d, mod_all, g_pre, w_in, qn, kn, cos_t, sin_t, m_fold, *, layer, mod_row, seq, rope, tm,
             n_sub=1):
    t, d = x2d.shape
    per_seq = seq // tm
    rope_map = (lambda i: (i % per_seq, 0)) if rope else (lambda i: (0, 0))
    return pl.pallas_call(
        functools.partial(_in_proj_kernel, rope=rope, n_sub=n_sub),
        grid=(t // tm,),
        in_specs=[
            pl.BlockSpec((tm, d), lambda i: (i, 0)),
            _mod_spec(d, layer, mod_row),
            _layer_spec((1, d), layer),
            _layer_spec((d, IN_WIDTH), layer),
            _layer_spec((1, HEAD_DIM), layer),
            _layer_spec((1, HEAD_DIM), layer),
            pl.BlockSpec((tm, HEAD_DIM), rope_map),
            pl.BlockSpec((tm, HEAD_DIM), rope_map),
            _layer_spec((N_FOURIER_GROUPS, FOURIER_GROUP, 2 * FOURIER_GROUP), layer),
        ],
        out_specs=[
            pl.BlockSpec((tm, ATTN_WIDTH), lambda i: (i, 0)),
            pl.BlockSpec((tm, KV_WIDTH), lambda i: (i, 0)),
            pl.BlockSpec((tm, KV_WIDTH), lambda i: (i, 0)),
            pl.BlockSpec((tm, 2 * FOURIER_WIDTH), lambda i: (i, 0)),
        ],
        out_shape=[
            jax.ShapeDtypeStruct((t, ATTN_WIDTH), BF16),
            jax.ShapeDtypeStruct((t, KV_WIDTH), BF16),
            jax.ShapeDtypeStruct((t, KV_WIDTH), BF16),
            jax.ShapeDtypeStruct((t, 2 * FOURIER_WIDTH), BF16),
        ],
        compiler_params=_params("arbitrary"),
        name="in_proj",
    )(x2d, mod_all, g_pre, w_in, qn, kn, cos_t, sin_t, m_fold)


def _attn_kernel(*refs, n_seg):
    q_ref = refs[0]
    kv_refs = refs[1:1 + 2 * n_seg]
    g_ref, o_ref, acc_ref = refs[1 + 2 * n_seg:]
    for h in range(N_Q_HEADS):
        kv = (h // Q_PER_KV) * HEAD_DIM
        qh = q_ref[:, h * HEAD_DIM:(h + 1) * HEAD_DIM]
        scores = [
            lax.dot_general(qh, kv_refs[2 * i][:, kv:kv + HEAD_DIM], (((1,), (1,)), ((), ())),
                            preferred_element_type=F32)
            for i in range(n_seg)
        ]
        m = functools.reduce(jnp.maximum, [jnp.max(s, axis=-1, keepdims=True) for s in scores])
        probs = [jnp.exp(s - m) for s in scores]
        denom = functools.reduce(jnp.add, [jnp.sum(p, axis=-1, keepdims=True) for p in probs])
        out = functools.reduce(jnp.add, [
            jnp.dot(probs[i].astype(BF16), kv_refs[2 * i + 1][:, kv:kv + HEAD_DIM],
                    preferred_element_type=F32)
            for i in range(n_seg)
        ])
        acc_ref[:, h * HEAD_DIM:(h + 1) * HEAD_DIM] = out / denom
    o_ref[...] = _rms(acc_ref[...], g_ref[...]).astype(BF16)


def _attention(q, segments, g_attn, *, layer, q_seq, tq):
    t = q.shape[0]
    per_seq = q_seq // tq
    in_specs = [pl.BlockSpec((tq, ATTN_WIDTH), lambda b, j: (b * per_seq + j, 0))]
    args = [q]
    for k, v, rows in segments:
        in_specs += [pl.BlockSpec((rows, KV_WIDTH), lambda b, j: (b, 0))] * 2
        args += [k, v]
    in_specs.append(_layer_spec((1, ATTN_WIDTH), layer))
    args.append(g_attn)
    return pl.pallas_call(
        functools.partial(_attn_kernel, n_seg=len(segments)),
        grid=(t // q_seq, per_seq),
        in_specs=in_specs,
        out_specs=pl.BlockSpec((tq, ATTN_WIDTH), lambda b, j: (b * per_seq + j, 0)),
        out_shape=jax.ShapeDtypeStruct((t, ATTN_WIDTH), BF16),
        scratch_shapes=[pltpu.VMEM((tq, ATTN_WIDTH), F32)],
        compiler_params=_params("arbitrary", "arbitrary"),
        name="attention",
    )(*args)


def _pos_dft_kernel(c_ref, s_ref, a_ref, b_ref, g_ref, o_ref):
    y = (jnp.dot(c_ref[...], a_ref[...], preferred_element_type=F32)
         - jnp.dot(s_ref[...], b_ref[...], preferred_element_type=F32))
    o_ref[...] = _rms(y, g_ref[...]).astype(BF16)


def _pos_dft(cos_m, sin_m, ab, g_four, *, layer, tm):
    seq = cos_m.shape[0]
    t = ab.shape[0]
    per_seq = seq // tm
    return pl.pallas_call(
        _pos_dft_kernel,
        grid=(t // seq, per_seq),
        in_specs=[
            pl.BlockSpec((tm, seq), lambda b, j: (j, 0)),
            pl.BlockSpec((tm, seq), lambda b, j: (j, 0)),
            pl.BlockSpec((seq, FOURIER_WIDTH), lambda b, j: (b, 0)),
            pl.BlockSpec((seq, FOURIER_WIDTH), lambda b, j: (b, 1)),
            _layer_spec((1, FOURIER_WIDTH), layer),
        ],
        out_specs=pl.BlockSpec((tm, FOURIER_WIDTH), lambda b, j: (b * per_seq + j, 0)),
        out_shape=jax.ShapeDtypeStruct((t, FOURIER_WIDTH), BF16),
        compiler_params=_params("arbitrary", "arbitrary"),
        name="pos_dft",
    )(cos_m, sin_m, ab, ab, g_four)


FLIP_BLOCK = 256
DFT_PAD_ROWS = 16


def _pos_dft_sym_kernel(alo_ref, ahi_ref, blo_ref, bhi_ref, c_ref, s_ref, q_ref, sgn_ref, g_ref,
                        o_ref, e_ref, d_ref, h_ref):
    half = alo_ref.shape[0]
    blk = q_ref.shape[0]
    nblk = half // blk
    qm = q_ref[...]
    is_row0 = lax.broadcasted_iota(jnp.int32, (blk, 1), 0) == 0

    def flipped_block(src_ref, bt, first_row):
        lo = (nblk - 1 - bt) * blk
        y = jnp.dot(qm, src_ref[lo:lo + blk, :], preferred_element_type=F32)
        if bt > 0:
            first_row = src_ref[lo + blk:lo + blk + 1, :].astype(F32)
        return jnp.where(is_row0, first_row, y)

    for bt in range(nblk):
        r = slice(bt * blk, (bt + 1) * blk)
        e_ref[r, :] = (alo_ref[r, :].astype(F32) + flipped_block(ahi_ref, bt, 0.0)).astype(BF16)
        d_ref[r, :] = (blo_ref[r, :].astype(F32) - flipped_block(bhi_ref, bt, 0.0)).astype(BF16)

    a_mid = ahi_ref[0:1, :].astype(F32)
    g = g_ref[...]
    mid_row = None
    for kb in range(nblk):
        r = slice(kb * blk, (kb + 1) * blk)
        rows = blk + (DFT_PAD_ROWS if kb == nblk - 1 else 0)
        cr = slice(kb * blk, kb * blk + rows)
        yc = jnp.dot(c_ref[cr, :], e_ref[...], preferred_element_type=F32) + sgn_ref[cr, :] * a_mid
        ys = jnp.dot(s_ref[r, :], d_ref[...], preferred_element_type=F32)
        o_ref[r, :] = _rms(yc[:blk] - ys, g).astype(BF16)
        h_ref[r, :] = _rms(yc[:blk] + ys, g).astype(BF16)
        if kb == nblk - 1:
            mid_row = _rms(yc[blk:blk + 1], g)

    for bs in range(nblk):
        r = slice(half + bs * blk, half + (bs + 1) * blk)
        o_ref[r, :] = flipped_block(h_ref, bs, mid_row).astype(BF16)


def _pos_dft_sym(tables, ab, g_four, *, layer):
    cos_m, sin_m, flip_m, sgn = tables
    half = sin_m.shape[0]
    t = ab.shape[0]
    blocks = lambda row, col: pl.BlockSpec((half, FOURIER_WIDTH), lambda b: (2 * b + row, col))
    return pl.pallas_call(
        _pos_dft_sym_kernel,
        grid=(t // (2 * half),),
        in_specs=[
            blocks(0, 0), blocks(1, 0), blocks(0, 1), blocks(1, 1),
            _const_spec(cos_m.shape),
            _const_spec(sin_m.shape),
            _const_spec(flip_m.shape),
            _const_spec(sgn.shape),
            _layer_spec((1, FOURIER_WIDTH), layer),
        ],
        out_specs=pl.BlockSpec((2 * half, FOURIER_WIDTH), lambda b: (b, 0)),
        out_shape=jax.ShapeDtypeStruct((t, FOURIER_WIDTH), BF16),
        scratch_shapes=[pltpu.VMEM((half, FOURIER_WIDTH), BF16)] * 3,
        compiler_params=_params("arbitrary"),
        name="pos_dft_sym",
    )(ab, ab, ab, ab, cos_m, sin_m, flip_m, sgn, g_four)


def _out_proj_kernel(a_ref, f_ref, w_ref, x_ref, mod_ref, gpost_ref, gpre_ref, xo_ref, h_ref, *, n_sub):
    mod = mod_ref[0]
    gate_g = mod[2:3] * gpost_ref[...]
    scale_g = (1.0 + mod[4:5]) * gpre_ref[...]
    rows = x_ref.shape[0] // n_sub
    for s in range(n_sub):
        r = slice(s * rows, (s + 1) * rows)
        mix = (jnp.dot(a_ref[r, :], w_ref[:ATTN_WIDTH, :], preferred_element_type=F32)
               + jnp.dot(f_ref[r, :], w_ref[ATTN_WIDTH:, :], preferred_element_type=F32))
        xn = x_ref[r, :] + _rms(mix, gate_g)
        xo_ref[r, :] = xn
        h_ref[r, :] = (_rms(xn, scale_g) + mod[3:4]).astype(BF16)


def _out_proj(a, f, w_out, x2d, mod_all, g_post, g_pre_ffn, *, layer, mod_row, tm, n_sub=4):
    t, d = x2d.shape
    return pl.pallas_call(
        functools.partial(_out_proj_kernel, n_sub=n_sub),
        grid=(t // tm,),
        in_specs=[
            pl.BlockSpec((tm, ATTN_WIDTH), lambda i: (i, 0)),
            pl.BlockSpec((tm, FOURIER_WIDTH), lambda i: (i, 0)),
            _layer_spec((ATTN_WIDTH + FOURIER_WIDTH, d), layer),
            pl.BlockSpec((tm, d), lambda i: (i, 0)),
            _mod_spec(d, layer, mod_row),
            _layer_spec((1, d), layer),
            _layer_spec((1, d), layer),
        ],
        out_specs=[
            pl.BlockSpec((tm, d), lambda i: (i, 0)),
            pl.BlockSpec((tm, d), lambda i: (i, 0)),
        ],
        out_shape=[
            jax.ShapeDtypeStruct((t, d), F32),
            jax.ShapeDtypeStruct((t, d), BF16),
        ],
        compiler_params=_params("arbitrary"),
        name="out_proj",
    )(a, f, w_out, x2d, mod_all, g_post, g_pre_ffn)


def _ffn_up_kernel(h_ref, wg_ref, wv_ref, cw_ref, cb_ref, o_ref, *, seg_len, n_sub):
    h = h_ref[...]
    rows, tn = o_ref.shape
    width = tn // n_sub
    pos = lax.broadcasted_iota(jnp.int32, (rows, 1), 0) % seg_len
    first, final = pos == 0, pos == seg_len - 1
    for s in range(n_sub):
        cols = slice(s * width, (s + 1) * width)
        gate = jnp.dot(h, wg_ref[:, cols], preferred_element_type=F32)
        val = jnp.dot(h, wv_ref[:, cols], preferred_element_type=F32)
        prev = jnp.where(first, 0.0, pltpu.roll(gate, 1, axis=0))
        nxt = jnp.where(final, 0.0, pltpu.roll(gate, rows - 1, axis=0))
        cw = cw_ref[:, cols]
        c = prev * cw[0:1] + gate * cw[1:2] + nxt * cw[2:3] + cb_ref[:, cols]
        act = 0.5 * c * (1.0 + jnp.tanh(math.sqrt(2.0 / math.pi) * (c + 0.044715 * (c * c * c))))
        o_ref[:, cols] = (act * val).astype(BF16)


def _ffn_up(h, w_up, conv_w, conv_b, *, layer, seg_len, tm, tn, n_sub=1):
    t, d = h.shape
    ffn = conv_w.shape[-1]
    n_col = ffn // tn
    return pl.pallas_call(
        functools.partial(_ffn_up_kernel, seg_len=seg_len, n_sub=n_sub),
        grid=(t // tm, n_col),
        in_specs=[
            pl.BlockSpec((tm, d), lambda i, j: (i, 0)),
            pl.BlockSpec((None, d, tn), lambda i, j: (layer, 0, j)),
            pl.BlockSpec((None, d, tn), lambda i, j: (layer, 0, j + n_col)),
            pl.BlockSpec((None, 3, tn), lambda i, j: (layer, 0, j)),
            pl.BlockSpec((None, 1, tn), lambda i, j: (layer, 0, j)),
        ],
        out_specs=pl.BlockSpec((tm, tn), lambda i, j: (i, j)),
        out_shape=jax.ShapeDtypeStruct((t, ffn), BF16),
        compiler_params=_params("arbitrary", "arbitrary"),
        name="ffn_up",
    )(h, w_up, w_up, conv_w, conv_b)


def _ffn_down_kernel(u_ref, w_ref, x_ref, mod_ref, gpost_ref, xo_ref, y_ref):
    j = pl.program_id(1)
    n_col, _, tn = y_ref.shape
    y_ref[j] = jnp.dot(u_ref[...], w_ref[...], preferred_element_type=F32)

    @pl.when(j == n_col - 1)
    def _():
        d = n_col * tn
        ssq = functools.reduce(
            jnp.add, [jnp.sum(y_ref[c] * y_ref[c], axis=-1, keepdims=True) for c in range(n_col)])
        inv = lax.rsqrt(ssq / d + EPS)
        gate = mod_ref[0][5:6]
        for c in range(n_col):
            cols = slice(c * tn, (c + 1) * tn)
            xo_ref[:, cols] = x_ref[:, cols] + gate[:, cols] * (y_ref[c] * inv * gpost_ref[:, cols])


def _ffn_down(u, w_down, x2d, mod_all, g_post, *, layer, mod_row, tm, tn):
    t, d = x2d.shape
    ffn = u.shape[1]
    n_col = d // tn
    return pl.pallas_call(
        _ffn_down_kernel,
        grid=(t // tm, n_col),
        in_specs=[
            pl.BlockSpec((tm, ffn), lambda i, j: (i, 0)),
            pl.BlockSpec((None, ffn, tn), lambda i, j: (layer, 0, j)),
            pl.BlockSpec((tm, d), lambda i, j: (i, 0)),
            _mod_spec(d, layer, mod_row),
            _layer_spec((1, d), layer),
        ],
        out_specs=pl.BlockSpec((tm, d), lambda i, j: (i, 0)),
        out_shape=jax.ShapeDtypeStruct((t, d), F32),
        scratch_shapes=[pltpu.VMEM((n_col, tm, tn), F32)],
        compiler_params=_params("arbitrary", "arbitrary"),
        name="ffn_down",
    )(u, w_down, x2d, mod_all, g_post)


def _ffn_down_skew_kernel(u_ref, w_ref, x_ref, mod_ref, gpost_ref, xo_ref, y_ref, inv_ref):
    i, j = pl.program_id(0), pl.program_id(1)
    n_col, _, tn = y_ref.shape

    @pl.when((i == 0) & (j == 0))
    def _():
        y_ref[...] = jnp.zeros_like(y_ref)
        inv_ref[...] = jnp.zeros_like(inv_ref)

    gate_g = mod_ref[0][N_MOD - 1:N_MOD] * gpost_ref[...]
    xo_ref[...] = x_ref[...] + (y_ref[j] * inv_ref[...]) * gate_g
    y_ref[j] = jnp.dot(u_ref[...], w_ref[...], preferred_element_type=F32)

    @pl.when(j == n_col - 1)
    def _():
        ssq = functools.reduce(
            jnp.add, [jnp.sum(y_ref[c] * y_ref[c], axis=-1, keepdims=True) for c in range(n_col)])
        inv_ref[...] = lax.rsqrt(ssq / (n_col * tn) + EPS)


def _ffn_down_skew(u, w_down, x2d, mod_all, g_post, *, layer, mod_row, tm, tn):
    t, d = x2d.shape
    ffn = u.shape[1]
    n_tiles = t // tm
    prev = lambda i: jnp.maximum(i - 1, 0)
    col = lambda i, j: jnp.where(i > 0, j, 0)
    return pl.pallas_call(
        _ffn_down_skew_kernel,
        grid=(n_tiles + 1, d // tn),
        in_specs=[
            pl.BlockSpec((tm, ffn), lambda i, j: (jnp.minimum(i, n_tiles - 1), 0)),
            pl.BlockSpec((None, ffn, tn), lambda i, j: (layer, 0, j)),
            pl.BlockSpec((tm, tn), lambda i, j: (prev(i), col(i, j))),
            pl.BlockSpec((None, 1, N_MOD, tn), lambda i, j: (layer, mod_row(prev(i)), 0, col(i, j))),
            pl.BlockSpec((None, 1, tn), lambda i, j: (layer, 0, col(i, j))),
        ],
        out_specs=pl.BlockSpec((tm, tn), lambda i, j: (prev(i), col(i, j))),
        out_shape=jax.ShapeDtypeStruct((t, d), F32),
        scratch_shapes=[pltpu.VMEM((d // tn, tm, tn), F32), pltpu.VMEM((tm, 1), F32)],
        compiler_params=_params("arbitrary", "arbitrary"),
        name="ffn_down_skew",
    )(u, w_down, x2d, mod_all, g_post)


def _rope_tables(n):
    rows = n // GRID_W
    row = jnp.repeat(jnp.arange(rows), GRID_W).astype(F32)
    col = jnp.tile(jnp.arange(GRID_W), rows).astype(F32)
    n_pairs_axis = HEAD_DIM // 4
    freqs = ROPE_THETA ** (-jnp.arange(n_pairs_axis, dtype=F32) / n_pairs_axis)
    ang = jnp.concatenate([row[:, None] * freqs, col[:, None] * freqs], axis=-1)
    c, s = jnp.cos(ang), jnp.sin(ang)
    return jnp.concatenate([c, c], axis=-1), jnp.concatenate([-s, s], axis=-1)


def _dft_matrices(n):
    idx = jnp.arange(n, dtype=jnp.int32)
    kt = (idx[:, None] * idx[None, :]) % n
    ang = kt.astype(F32) * (2.0 * math.pi / n)
    scale = n ** -0.5
    return jnp.cos(ang) * scale, jnp.sin(ang) * scale


def _sym_dft_tables(n):
    half = n // 2
    k = jnp.arange(half + DFT_PAD_ROWS, dtype=jnp.int32)[:, None]
    t = jnp.arange(half, dtype=jnp.int32)[None, :]
    ang = ((k * t) % n).astype(F32) * (2.0 * math.pi / n)
    scale = n ** -0.5
    valid = k <= half
    cos_m = jnp.where(valid, jnp.cos(ang) * scale, 0.0).astype(BF16)
    sin_m = (jnp.sin(ang[:half]) * scale).astype(BF16)
    sgn = jnp.where(valid, (1 - 2 * (k % 2)).astype(F32) * scale, 0.0)
    i = jnp.arange(FLIP_BLOCK, dtype=jnp.int32)
    flip_m = (i[:, None] + i[None, :] == FLIP_BLOCK).astype(BF16)
    return cos_m, sin_m, flip_m, sgn


def _deinterleave(v):
    return jnp.concatenate([v[..., 0::2], v[..., 1::2]], axis=-1)


def kernel(x, c, ctx, c_ctx, w_mod, b_mod, g_pre_mix, g_post_mix, g_pre_ffn, g_post_ffn,
           w_in, q_norm, k_norm, w_four, g_attn_out, g_four_out, w_out,
           w_up, conv_w, conv_b, w_down):
    b, n, d = x.shape
    ctx_len = ctx.shape[1]
    depth = w_mod.shape[0]

    perm_matrix = _deinterleave(jnp.eye(HEAD_DIM, dtype=BF16))
    w_in_b = _prep_w_in(w_in, perm_matrix)
    w_out_b = w_out.astype(BF16)
    w_up_b = w_up.astype(BF16)
    w_down_b = w_down.astype(BF16)
    qn = _deinterleave(q_norm).reshape(depth, 1, HEAD_DIM)
    kn = _deinterleave(k_norm).reshape(depth, 1, HEAD_DIM)
    cos_t, sin_t = _rope_tables(n)
    dft_tables = _sym_dft_tables(n)
    cos_c, sin_c = (m.astype(BF16) for m in _dft_matrices(ctx_len))
    cs_ch = jnp.stack(_dft_matrices(FOURIER_GROUP))
    m_fold = _fold_channel_dft(cs_ch, w_four)
    rows3 = lambda v: v.reshape(depth, 1, -1)
    g_pre_mix, g_post_mix, g_pre_ffn, g_post_ffn, g_attn_out, g_four_out, conv_b = map(
        rows3, (g_pre_mix, g_post_mix, g_pre_ffn, g_post_ffn, g_attn_out, g_four_out, conv_b))

    cc = jnp.zeros((MOD_ROWS, d), F32).at[:b].set(c).at[b].set(c_ctx)
    mod_all = _mod_proj(cc, w_mod, b_mod).reshape(depth, MOD_ROWS, N_MOD, d)

    xl = x.reshape(b * n, d)
    xc = ctx.reshape(b * ctx_len, d)
    tm = 512
    lat_row = lambda i: i // (n // tm)
    ctx_row = lambda i: b

    for i in range(depth):
        last = i == depth - 1
        q_l, k_l, v_l, ab_l = _in_proj(xl, mod_all, g_pre_mix, w_in_b, qn, kn, cos_t, sin_t, m_fold,
                                       layer=i, mod_row=lat_row, seq=n, rope=True, tm=tm)
        q_c, k_c, v_c, ab_c = _in_proj(xc, mod_all, g_pre_mix, w_in_b, qn, kn, cos_t, sin_t, m_fold,
                                       layer=i, mod_row=ctx_row, seq=ctx_len, rope=False, tm=ctx_len)

        attn_l = _attention(q_l, [(k_l, v_l, n), (k_c, v_c, ctx_len)], g_attn_out,
                            layer=i, q_seq=n, tq=512)
        four_l = _pos_dft_sym(dft_tables, ab_l, g_four_out, layer=i)
        xl, h_l = _out_proj(attn_l, four_l, w_out_b, xl, mod_all, g_post_mix, g_pre_ffn,
                            layer=i, mod_row=lat_row, tm=tm)
        u_l = _ffn_up(h_l, w_up_b, conv_w, conv_b, layer=i, seg_len=n, tm=n, tn=512)
        xl = _ffn_down_skew(u_l, w_down_b, xl, mod_all, g_post_ffn, layer=i,
                            mod_row=lambda r: r // (n // 1024), tm=1024, tn=512)

        if not last:
            attn_c = _attention(q_c, [(k_c, v_c, ctx_len)], g_attn_out,
                                layer=i, q_seq=ctx_len, tq=ctx_len)
            four_c = _pos_dft(cos_c, sin_c, ab_c, g_four_out, layer=i, tm=ctx_len)
            xc, h_c = _out_proj(attn_c, four_c, w_out_b, xc, mod_all, g_post_mix, g_pre_ffn,
                                layer=i, mod_row=ctx_row, tm=tm)
            u_c = _ffn_up(h_c, w_up_b, conv_w, conv_b, layer=i, seg_len=ctx_len,
                          tm=b * ctx_len, tn=512)
            xc = _ffn_down(u_c, w_down_b, xc, mod_all, g_post_ffn, layer=i, mod_row=ctx_row,
                           tm=tm, tn=512)

    return xl.reshape(b, n, d)
```

```python
import functools
import math

import jax
import jax.numpy as jnp
import numpy as np
from jax import lax
from jax.experimental import pallas as pl
from jax.experimental.pallas import tpu as pltpu

GRID_W = 64
HEAD_DIM = 128
N_Q_HEADS = 8
N_KV_HEADS = 2
Q_PER_KV = N_Q_HEADS // N_KV_HEADS
FOURIER_GROUP = 128
N_FOURIER_GROUPS = 8
ATTN_WIDTH = N_Q_HEADS * HEAD_DIM
KV_WIDTH = N_KV_HEADS * HEAD_DIM
FOURIER_WIDTH = N_FOURIER_GROUPS * FOURIER_GROUP
Q_END = ATTN_WIDTH
K_END = Q_END + KV_WIDTH
V_END = K_END + KV_WIDTH
IN_WIDTH = V_END + FOURIER_WIDTH
ROPE_THETA = 10000.0
EPS = 1e-6
N_MOD = 6
MOD_ROWS = 16

VMEM_LIMIT_BYTES = 56 * 1024 * 1024

F32 = jnp.float32
BF16 = jnp.bfloat16


def _params(*semantics):
    return pltpu.CompilerParams(dimension_semantics=semantics, vmem_limit_bytes=VMEM_LIMIT_BYTES)


def _rms(x, g):
    ms = jnp.mean(x * x, axis=-1, keepdims=True)
    return x * lax.rsqrt(ms + EPS) * g


def _const_spec(shape):
    return pl.BlockSpec(shape, lambda *_: (0,) * len(shape), pipeline_mode=pl.Buffered(1))


def _layer_spec(shape, layer):
    zeros = (0,) * len(shape)
    return pl.BlockSpec((None,) + shape, lambda *_: (layer,) + zeros, pipeline_mode=pl.Buffered(1))


def _side_casts(sources, layer, n_steps, step_of):
    in_specs, out_specs, out_shapes = [], [], []
    for src in sources:
        _, r, c = src.shape
        rows = r // n_steps
        in_specs.append(pl.BlockSpec((None, rows, c), lambda *g: (layer, step_of(*g), 0)))
        out_specs.append(pl.BlockSpec((rows, c), lambda *g: (step_of(*g), 0)))
        out_shapes.append(jax.ShapeDtypeStruct((r, c), BF16))
    return in_specs, out_specs, out_shapes


def _run_side_casts(src_refs, dst_refs):
    for src, dst in zip(src_refs, dst_refs, strict=True):
        dst[...] = src[...].astype(BF16)


def _mod_spec(d, layer, row_of):
    return pl.BlockSpec((None, 1, N_MOD, d), lambda i, *_: (layer, row_of(i), 0, 0))


def _mod_kernel(c_ref, w_ref, b_ref, o_ref):
    c = c_ref[...]
    s = (c * jax.nn.sigmoid(c)).astype(BF16)
    o_ref[0] = jnp.dot(s, w_ref[0].astype(BF16), preferred_element_type=F32) + b_ref[0]


def _mod_proj(cc, w_mod, b_mod, tn=1024):
    depth, d, width = w_mod.shape
    return pl.pallas_call(
        _mod_kernel,
        grid=(depth, width // tn),
        in_specs=[
            pl.BlockSpec((MOD_ROWS, d), lambda l, j: (0, 0)),
            pl.BlockSpec((1, d, tn), lambda l, j: (l, 0, j)),
            pl.BlockSpec((1, 1, tn), lambda l, j: (l, 0, j)),
        ],
        out_specs=pl.BlockSpec((1, MOD_ROWS, tn), lambda l, j: (l, 0, j)),
        out_shape=jax.ShapeDtypeStruct((depth, MOD_ROWS, width), F32),
        compiler_params=_params("arbitrary", "arbitrary"),
        name="mod_proj",
    )(cc, w_mod, b_mod.reshape(depth, 1, width))


def _fold_kernel(cs_ref, w_ref, o_ref):
    w = w_ref[0, 0]
    o_ref[0, 0, :, :FOURIER_GROUP] = jnp.dot(
        cs_ref[0], w, preferred_element_type=F32, precision=lax.Precision.HIGHEST).astype(BF16)
    o_ref[0, 0, :, FOURIER_GROUP:] = jnp.dot(
        cs_ref[1], w, preferred_element_type=F32, precision=lax.Precision.HIGHEST).astype(BF16)


def _fold_channel_dft(cs, w_four):
    depth, groups, c, _ = w_four.shape
    return pl.pallas_call(
        _fold_kernel,
        grid=(depth, groups),
        in_specs=[
            pl.BlockSpec((2, c, c), lambda l, g: (0, 0, 0)),
            pl.BlockSpec((1, 1, c, c), lambda l, g: (l, g, 0, 0)),
        ],
        out_specs=pl.BlockSpec((1, 1, c, 2 * c), lambda l, g: (l, g, 0, 0)),
        out_shape=jax.ShapeDtypeStruct((depth, groups, c, 2 * c), BF16),
        compiler_params=_params("arbitrary", "arbitrary"),
        name="fold_channel_dft",
    )(cs, w_four)


def _prep_w_in_kernel(w_ref, p_ref, o_ref):
    j = pl.program_id(1)
    w = w_ref[0].astype(BF16)

    @pl.when(j < N_Q_HEADS + N_KV_HEADS)
    def _():
        o_ref[0] = jnp.dot(w, p_ref[...], preferred_element_type=F32).astype(BF16)

    @pl.when(j >= N_Q_HEADS + N_KV_HEADS)
    def _():
        o_ref[0] = w


def _prep_w_in(w_in, perm_matrix):
    depth, d, width = w_in.shape
    return pl.pallas_call(
        _prep_w_in_kernel,
        grid=(depth, width // HEAD_DIM),
        in_specs=[
            pl.BlockSpec((1, d, HEAD_DIM), lambda l, j: (l, 0, j)),
            pl.BlockSpec((HEAD_DIM, HEAD_DIM), lambda l, j: (0, 0)),
        ],
        out_specs=pl.BlockSpec((1, d, HEAD_DIM), lambda l, j: (l, 0, j)),
        out_shape=jax.ShapeDtypeStruct((depth, d, width), BF16),
        compiler_params=_params("arbitrary", "arbitrary"),
        name="prep_w_in",
    )(w_in, perm_matrix)


def _in_proj_kernel(x_ref, mod_ref, gpre_ref, w_ref, qn_ref, kn_ref, cos_ref, sin_ref, m_ref, *rest,
                    rope, n_sub, n_cast):
    cast_src, (q_ref, k_ref, v_ref, ab_ref), cast_dst = rest[:n_cast], rest[n_cast:n_cast + 4], rest[n_cast + 4:]
    _run_side_casts(cast_src, cast_dst)
    mod = mod_ref[0]
    scale_g = (1.0 + mod[1:2]) * gpre_ref[...]
    rows = x_ref.shape[0] // n_sub
    for s in range(n_sub):
        r = slice(s * rows, (s + 1) * rows)
        h = _rms(x_ref[r, :], scale_g) + mod[0:1]
        p = jnp.dot(h.astype(BF16), w_ref[...], preferred_element_type=F32)

        def head(col, gain, post_scale):
            y = _rms(p[:, col:col + HEAD_DIM], gain)
            if rope:
                y = y * cos_ref[r, :] + pltpu.roll(y, HEAD_DIM // 2, axis=1) * sin_ref[r, :]
            if post_scale != 1.0:
                y = y * post_scale
            return y.astype(BF16)

        for i in range(N_Q_HEADS):
            q_ref[r, i * HEAD_DIM:(i + 1) * HEAD_DIM] = head(i * HEAD_DIM, qn_ref[...], HEAD_DIM ** -0.5)
        for i in range(N_KV_HEADS):
            k_ref[r, i * HEAD_DIM:(i + 1) * HEAD_DIM] = head(Q_END + i * HEAD_DIM, kn_ref[...], 1.0)
        v_ref[r, :] = p[:, K_END:V_END].astype(BF16)
        for g in range(N_FOURIER_GROUPS):
            lo = V_END + g * FOURIER_GROUP
            ab = jnp.dot(p[:, lo:lo + FOURIER_GROUP].astype(BF16), m_ref[g], preferred_element_type=F32)
            ab_ref[r, g * FOURIER_GROUP:(g + 1) * FOURIER_GROUP] = ab[:, :FOURIER_GROUP].astype(BF16)
            ab_ref[r, FOURIER_WIDTH + g * FOURIER_GROUP:FOURIER_WIDTH + (g + 1) * FOURIER_GROUP] = (
                ab[:, FOURIER_GROUP:].astype(BF16))


def _in_proj(x2d, mod_all, g_pre, w_in, qn, kn, cos_t, sin_t, m_fold, *, layer, mod_row, seq, rope, tm,
             n_sub=1, cast=()):
    t, d = x2d.shape
    per_seq = seq // tm
    rope_map = (lambda i: (i % per_seq, 0)) if rope else (lambda i: (0, 0))
    cast_in, cast_out, cast_shapes = _side_casts(cast, layer, t // tm, lambda i: i)
    return pl.pallas_call(
        functools.partial(_in_proj_kernel, rope=rope, n_sub=n_sub, n_cast=len(cast)),
        grid=(t // tm,),
        in_specs=[
            pl.BlockSpec((tm, d), lambda i: (i, 0)),
            _mod_spec(d, layer, mod_row),
            _layer_spec((1, d), layer),
            _layer_spec((d, IN_WIDTH), layer),
            _layer_spec((1, HEAD_DIM), layer),
            _layer_spec((1, HEAD_DIM), layer),
            pl.BlockSpec((tm, HEAD_DIM), rope_map),
            pl.BlockSpec((tm, HEAD_DIM), rope_map),
            _layer_spec((N_FOURIER_GROUPS, FOURIER_GROUP, 2 * FOURIER_GROUP), layer),
        ] + cast_in,
        out_specs=[
            pl.BlockSpec((tm, ATTN_WIDTH), lambda i: (i, 0)),
            pl.BlockSpec((tm, KV_WIDTH), lambda i: (i, 0)),
            pl.BlockSpec((tm, KV_WIDTH), lambda i: (i, 0)),
            pl.BlockSpec((tm, 2 * FOURIER_WIDTH), lambda i: (i, 0)),
        ] + cast_out,
        out_shape=[
            jax.ShapeDtypeStruct((t, ATTN_WIDTH), BF16),
            jax.ShapeDtypeStruct((t, KV_WIDTH), BF16),
            jax.ShapeDtypeStruct((t, KV_WIDTH), BF16),
            jax.ShapeDtypeStruct((t, 2 * FOURIER_WIDTH), BF16),
        ] + cast_shapes,
        compiler_params=_params("arbitrary"),
        name="in_proj",
    )(x2d, mod_all, g_pre, w_in, qn, kn, cos_t, sin_t, m_fold, *cast)


def _ctx_kv_kernel(x_ref, mod_ref, gpre_ref, w_ref, kn_ref, k_ref, v_ref):
    mod = mod_ref[0]
    h = _rms(x_ref[...], (1.0 + mod[1:2]) * gpre_ref[...]) + mod[0:1]
    p = jnp.dot(h.astype(BF16), w_ref[...], preferred_element_type=F32)
    for i in range(N_KV_HEADS):
        cols = slice(i * HEAD_DIM, (i + 1) * HEAD_DIM)
        k_ref[:, cols] = _rms(p[:, cols], kn_ref[...]).astype(BF16)
    v_ref[...] = p[:, KV_WIDTH:].astype(BF16)


def _ctx_kv(x2d, mod_all, g_pre, w_in, kn, *, layer, mod_row, tm):
    t, d = x2d.shape
    assert Q_END % (2 * KV_WIDTH) == 0
    return pl.pallas_call(
        _ctx_kv_kernel,
        grid=(t // tm,),
        in_specs=[
            pl.BlockSpec((tm, d), lambda i: (i, 0)),
            _mod_spec(d, layer, mod_row),
            _layer_spec((1, d), layer),
            pl.BlockSpec((None, d, 2 * KV_WIDTH), lambda i: (layer, 0, Q_END // (2 * KV_WIDTH)),
                         pipeline_mode=pl.Buffered(1)),
            _layer_spec((1, HEAD_DIM), layer),
        ],
        out_specs=[pl.BlockSpec((tm, KV_WIDTH), lambda i: (i, 0))] * 2,
        out_shape=[jax.ShapeDtypeStruct((t, KV_WIDTH), BF16)] * 2,
        compiler_params=_params("arbitrary"),
        name="ctx_kv",
    )(x2d, mod_all, g_pre, w_in, kn)


def _attn_kernel(*refs, n_seg, n_cast):
    q_ref = refs[0]
    kv_refs = refs[1:1 + 2 * n_seg]
    g_ref = refs[1 + 2 * n_seg]
    rest = refs[2 + 2 * n_seg:]
    cast_src, o_ref, cast_dst, acc_ref = rest[:n_cast], rest[n_cast], rest[n_cast + 1:-1], rest[-1]
    _run_side_casts(cast_src, cast_dst)
    for h in range(N_Q_HEADS):
        kv = (h // Q_PER_KV) * HEAD_DIM
        qh = q_ref[:, h * HEAD_DIM:(h + 1) * HEAD_DIM]
        scores = [
            lax.dot_general(qh, kv_refs[2 * i][:, kv:kv + HEAD_DIM], (((1,), (1,)), ((), ())),
                            preferred_element_type=F32)
            for i in range(n_seg)
        ]
        m = functools.reduce(jnp.maximum, [jnp.max(s, axis=-1, keepdims=True) for s in scores])
        probs = [jnp.exp(s - m) for s in scores]
        denom = functools.reduce(jnp.add, [jnp.sum(p, axis=-1, keepdims=True) for p in probs])
        out = functools.reduce(jnp.add, [
            jnp.dot(probs[i].astype(BF16), kv_refs[2 * i + 1][:, kv:kv + HEAD_DIM],
                    preferred_element_type=F32)
            for i in range(n_seg)
        ])
        acc_ref[:, h * HEAD_DIM:(h + 1) * HEAD_DIM] = out / denom
    o_ref[...] = _rms(acc_ref[...], g_ref[...]).astype(BF16)


def _attention(q, segments, g_attn, *, layer, q_seq, tq, cast=()):
    t = q.shape[0]
    per_seq = q_seq // tq
    in_specs = [pl.BlockSpec((tq, ATTN_WIDTH), lambda b, j: (b * per_seq + j, 0))]
    args = [q]
    for k, v, rows in segments:
        in_specs += [pl.BlockSpec((rows, KV_WIDTH), lambda b, j: (b, 0))] * 2
        args += [k, v]
    in_specs.append(_layer_spec((1, ATTN_WIDTH), layer))
    args.append(g_attn)
    cast_in, cast_out, cast_shapes = _side_casts(cast, layer, t // tq, lambda b, j: b * per_seq + j)
    return pl.pallas_call(
        functools.partial(_attn_kernel, n_seg=len(segments), n_cast=len(cast)),
        grid=(t // q_seq, per_seq),
        in_specs=in_specs + cast_in,
        out_specs=[pl.BlockSpec((tq, ATTN_WIDTH), lambda b, j: (b * per_seq + j, 0))] + cast_out,
        out_shape=[jax.ShapeDtypeStruct((t, ATTN_WIDTH), BF16)] + cast_shapes,
        scratch_shapes=[pltpu.VMEM((tq, ATTN_WIDTH), F32)],
        compiler_params=_params("arbitrary", "arbitrary"),
        name="attention",
    )(*args, *cast)


def _pos_dft_kernel(c_ref, s_ref, a_ref, b_ref, g_ref, o_ref):
    y = (jnp.dot(c_ref[...], a_ref[...], preferred_element_type=F32)
         - jnp.dot(s_ref[...], b_ref[...], preferred_element_type=F32))
    o_ref[...] = _rms(y, g_ref[...]).astype(BF16)


def _pos_dft(cos_m, sin_m, ab, g_four, *, layer, tm):
    seq = cos_m.shape[0]
    t = ab.shape[0]
    per_seq = seq // tm
    return pl.pallas_call(
        _pos_dft_kernel,
        grid=(t // seq, per_seq),
        in_specs=[
            pl.BlockSpec((tm, seq), lambda b, j: (j, 0)),
            pl.BlockSpec((tm, seq), lambda b, j: (j, 0)),
            pl.BlockSpec((seq, FOURIER_WIDTH), lambda b, j: (b, 0)),
            pl.BlockSpec((seq, FOURIER_WIDTH), lambda b, j: (b, 1)),
            _layer_spec((1, FOURIER_WIDTH), layer),
        ],
        out_specs=pl.BlockSpec((tm, FOURIER_WIDTH), lambda b, j: (b * per_seq + j, 0)),
        out_shape=jax.ShapeDtypeStruct((t, FOURIER_WIDTH), BF16),
        compiler_params=_params("arbitrary", "arbitrary"),
        name="pos_dft",
    )(cos_m, sin_m, ab, ab, g_four)


FLIP_BLOCK = 256
DFT_PAD_ROWS = 16


def _pos_dft_sym_kernel(alo_ref, ahi_ref, blo_ref, bhi_ref, c_ref, s_ref, q_ref, sgn_ref, g_ref,
                        o_ref, e_ref, d_ref, h_ref):
    half = alo_ref.shape[0]
    blk = q_ref.shape[0]
    nblk = half // blk
    qm = q_ref[...]
    is_row0 = lax.broadcasted_iota(jnp.int32, (blk, 1), 0) == 0

    def flipped_block(src_ref, bt, first_row):
        lo = (nblk - 1 - bt) * blk
        y = jnp.dot(qm, src_ref[lo:lo + blk, :], preferred_element_type=F32)
        if bt > 0:
            first_row = src_ref[lo + blk:lo + blk + 1, :].astype(F32)
        return jnp.where(is_row0, first_row, y)

    for bt in range(nblk):
        r = slice(bt * blk, (bt + 1) * blk)
        e_ref[r, :] = (alo_ref[r, :].astype(F32) + flipped_block(ahi_ref, bt, 0.0)).astype(BF16)
        d_ref[r, :] = (blo_ref[r, :].astype(F32) - flipped_block(bhi_ref, bt, 0.0)).astype(BF16)

    a_mid = ahi_ref[0:1, :].astype(F32)
    g = g_ref[...]
    mid_row = None
    for kb in range(nblk):
        r = slice(kb * blk, (kb + 1) * blk)
        rows = blk + (DFT_PAD_ROWS if kb == nblk - 1 else 0)
        cr = slice(kb * blk, kb * blk + rows)
        yc = jnp.dot(c_ref[cr, :], e_ref[...], preferred_element_type=F32) + sgn_ref[cr, :] * a_mid
        ys = jnp.dot(s_ref[r, :], d_ref[...], preferred_element_type=F32)
        o_ref[r, :] = _rms(yc[:blk] - ys, g).astype(BF16)
        h_ref[r, :] = _rms(yc[:blk] + ys, g).astype(BF16)
        if kb == nblk - 1:
            mid_row = _rms(yc[blk:blk + 1], g)

    for bs in range(nblk):
        r = slice(half + bs * blk, half + (bs + 1) * blk)
        o_ref[r, :] = flipped_block(h_ref, bs, mid_row).astype(BF16)


def _pos_dft_sym(tables, ab, g_four, *, layer):
    cos_m, sin_m, flip_m, sgn = tables
    half = sin_m.shape[0]
    t = ab.shape[0]
    blocks = lambda row, col: pl.BlockSpec((half, FOURIER_WIDTH), lambda b: (2 * b + row, col))
    return pl.pallas_call(
        _pos_dft_sym_kernel,
        grid=(t // (2 * half),),
        in_specs=[
            blocks(0, 0), blocks(1, 0), blocks(0, 1), blocks(1, 1),
            _const_spec(cos_m.shape),
            _const_spec(sin_m.shape),
            _const_spec(flip_m.shape),
            _const_spec(sgn.shape),
            _layer_spec((1, FOURIER_WIDTH), layer),
        ],
        out_specs=pl.BlockSpec((2 * half, FOURIER_WIDTH), lambda b: (b, 0)),
        out_shape=jax.ShapeDtypeStruct((t, FOURIER_WIDTH), BF16),
        scratch_shapes=[pltpu.VMEM((half, FOURIER_WIDTH), BF16)] * 3,
        compiler_params=_params("arbitrary"),
        name="pos_dft_sym",
    )(ab, ab, ab, ab, cos_m, sin_m, flip_m, sgn, g_four)


def _out_proj_kernel(a_ref, f_ref, w_ref, x_ref, mod_ref, gpost_ref, gpre_ref, xo_ref, h_ref, *, n_sub):
    mod = mod_ref[0]
    gate_g = mod[2:3] * gpost_ref[...]
    scale_g = (1.0 + mod[4:5]) * gpre_ref[...]
    rows = x_ref.shape[0] // n_sub
    for s in range(n_sub):
        r = slice(s * rows, (s + 1) * rows)
        mix = (jnp.dot(a_ref[r, :], w_ref[:ATTN_WIDTH, :], preferred_element_type=F32)
               + jnp.dot(f_ref[r, :], w_ref[ATTN_WIDTH:, :], preferred_element_type=F32))
        xn = x_ref[r, :] + _rms(mix, gate_g)
        xo_ref[r, :] = xn
        h_ref[r, :] = (_rms(xn, scale_g) + mod[3:4]).astype(BF16)


def _out_proj(a, f, w_out, x2d, mod_all, g_post, g_pre_ffn, *, layer, mod_row, tm, n_sub=4):
    t, d = x2d.shape
    return pl.pallas_call(
        functools.partial(_out_proj_kernel, n_sub=n_sub),
        grid=(t // tm,),
        in_specs=[
            pl.BlockSpec((tm, ATTN_WIDTH), lambda i: (i, 0)),
            pl.BlockSpec((tm, FOURIER_WIDTH), lambda i: (i, 0)),
            _const_spec((ATTN_WIDTH + FOURIER_WIDTH, d)),
            pl.BlockSpec((tm, d), lambda i: (i, 0)),
            _mod_spec(d, layer, mod_row),
            _layer_spec((1, d), layer),
            _layer_spec((1, d), layer),
        ],
        out_specs=[
            pl.BlockSpec((tm, d), lambda i: (i, 0)),
            pl.BlockSpec((tm, d), lambda i: (i, 0)),
        ],
        out_shape=[
            jax.ShapeDtypeStruct((t, d), F32),
            jax.ShapeDtypeStruct((t, d), BF16),
        ],
        compiler_params=_params("arbitrary"),
        name="out_proj",
    )(a, f, w_out, x2d, mod_all, g_post, g_pre_ffn)


def _ffn_up_kernel(h_ref, wg_ref, wv_ref, cw_ref, cb_ref, o_ref, *, seg_len, n_sub):
    h = h_ref[...]
    rows, tn = o_ref.shape
    width = tn // n_sub
    pos = lax.broadcasted_iota(jnp.int32, (rows, 1), 0) % seg_len
    first, final = pos == 0, pos == seg_len - 1
    for s in range(n_sub):
        cols = slice(s * width, (s + 1) * width)
        gate = jnp.dot(h, wg_ref[:, cols], preferred_element_type=F32)
        val = jnp.dot(h, wv_ref[:, cols], preferred_element_type=F32)
        prev = jnp.where(first, 0.0, pltpu.roll(gate, 1, axis=0))
        nxt = jnp.where(final, 0.0, pltpu.roll(gate, rows - 1, axis=0))
        cw = cw_ref[:, cols]
        c = prev * cw[0:1] + gate * cw[1:2] + nxt * cw[2:3] + cb_ref[:, cols]
        act = 0.5 * c * (1.0 + jnp.tanh(math.sqrt(2.0 / math.pi) * (c + 0.044715 * (c * c * c))))
        o_ref[:, cols] = (act * val).astype(BF16)


def _ffn_up(h, w_up, conv_w, conv_b, *, layer, seg_len, tm, tn, n_sub=1):
    t, d = h.shape
    ffn = conv_w.shape[-1]
    n_col = ffn // tn
    return pl.pallas_call(
        functools.partial(_ffn_up_kernel, seg_len=seg_len, n_sub=n_sub),
        grid=(t // tm, n_col),
        in_specs=[
            pl.BlockSpec((tm, d), lambda i, j: (i, 0)),
            pl.BlockSpec((d, tn), lambda i, j: (0, j)),
            pl.BlockSpec((d, tn), lambda i, j: (0, j + n_col)),
            pl.BlockSpec((None, 3, tn), lambda i, j: (layer, 0, j)),
            pl.BlockSpec((None, 1, tn), lambda i, j: (layer, 0, j)),
        ],
        out_specs=pl.BlockSpec((tm, tn), lambda i, j: (i, j)),
        out_shape=jax.ShapeDtypeStruct((t, ffn), BF16),
        compiler_params=_params("arbitrary", "arbitrary"),
        name="ffn_up",
    )(h, w_up, w_up, conv_w, conv_b)


def _ffn_down_kernel(u_ref, w_ref, x_ref, mod_ref, gpost_ref, xo_ref, y_ref):
    j = pl.program_id(1)
    n_col, _, tn = y_ref.shape
    y_ref[j] = jnp.dot(u_ref[...], w_ref[...], preferred_element_type=F32)

    @pl.when(j == n_col - 1)
    def _():
        d = n_col * tn
        ssq = functools.reduce(
            jnp.add, [jnp.sum(y_ref[c] * y_ref[c], axis=-1, keepdims=True) for c in range(n_col)])
        inv = lax.rsqrt(ssq / d + EPS)
        gate = mod_ref[0][5:6]
        for c in range(n_col):
            cols = slice(c * tn, (c + 1) * tn)
            xo_ref[:, cols] = x_ref[:, cols] + gate[:, cols] * (y_ref[c] * inv * gpost_ref[:, cols])


def _ffn_down(u, w_down, x2d, mod_all, g_post, *, layer, mod_row, tm, tn):
    t, d = x2d.shape
    ffn = u.shape[1]
    n_col = d // tn
    return pl.pallas_call(
        _ffn_down_kernel,
        grid=(t // tm, n_col),
        in_specs=[
            pl.BlockSpec((tm, ffn), lambda i, j: (i, 0)),
            pl.BlockSpec((ffn, tn), lambda i, j: (0, j)),
            pl.BlockSpec((tm, d), lambda i, j: (i, 0)),
            _mod_spec(d, layer, mod_row),
            _layer_spec((1, d), layer),
        ],
        out_specs=pl.BlockSpec((tm, d), lambda i, j: (i, 0)),
        out_shape=jax.ShapeDtypeStruct((t, d), F32),
        scratch_shapes=[pltpu.VMEM((n_col, tm, tn), F32)],
        compiler_params=_params("arbitrary", "arbitrary"),
        name="ffn_down",
    )(u, w_down, x2d, mod_all, g_post)


def _ffn_down_skew_kernel(u_ref, w_ref, x_ref, mod_ref, gpost_ref, xo_ref, y_ref, inv_ref):
    i, j = pl.program_id(0), pl.program_id(1)
    n_col, _, tn = y_ref.shape

    @pl.when((i == 0) & (j == 0))
    def _():
        y_ref[...] = jnp.zeros_like(y_ref)
        inv_ref[...] = jnp.zeros_like(inv_ref)

    gate_g = mod_ref[0][N_MOD - 1:N_MOD] * gpost_ref[...]
    xo_ref[...] = x_ref[...] + (y_ref[j] * inv_ref[...]) * gate_g
    y_ref[j] = jnp.dot(u_ref[...], w_ref[...], preferred_element_type=F32)

    @pl.when(j == n_col - 1)
    def _():
        ssq = functools.reduce(
            jnp.add, [jnp.sum(y_ref[c] * y_ref[c], axis=-1, keepdims=True) for c in range(n_col)])
        inv_ref[...] = lax.rsqrt(ssq / (n_col * tn) + EPS)


def _ffn_down_skew(u, w_down, x2d, mod_all, g_post, *, layer, mod_row, tm, tn):
    t, d = x2d.shape
    ffn = u.shape[1]
    n_tiles = t // tm
    prev = lambda i: jnp.maximum(i - 1, 0)
    col = lambda i, j: jnp.where(i > 0, j, 0)
    return pl.pallas_call(
        _ffn_down_skew_kernel,
        grid=(n_tiles + 1, d // tn),
        in_specs=[
            pl.BlockSpec((tm, ffn), lambda i, j: (jnp.minimum(i, n_tiles - 1), 0)),
            pl.BlockSpec((ffn, tn), lambda i, j: (0, j)),
            pl.BlockSpec((tm, tn), lambda i, j: (prev(i), col(i, j))),
            pl.BlockSpec((None, 1, N_MOD, tn), lambda i, j: (layer, mod_row(prev(i)), 0, col(i, j))),
            pl.BlockSpec((None, 1, tn), lambda i, j: (layer, 0, col(i, j))),
        ],
        out_specs=pl.BlockSpec((tm, tn), lambda i, j: (prev(i), col(i, j))),
        out_shape=jax.ShapeDtypeStruct((t, d), F32),
        scratch_shapes=[pltpu.VMEM((d // tn, tm, tn), F32), pltpu.VMEM((tm, 1), F32)],
        compiler_params=_params("arbitrary", "arbitrary"),
        name="ffn_down_skew",
    )(u, w_down, x2d, mod_all, g_post)


def _rope_tables(n):
    pos = np.arange(n)
    n_pairs_axis = HEAD_DIM // 4
    freqs = ROPE_THETA ** (-np.arange(n_pairs_axis, dtype=np.float64) / n_pairs_axis)
    ang = np.concatenate([(pos // GRID_W)[:, None] * freqs, (pos % GRID_W)[:, None] * freqs], axis=-1)
    c, s = np.cos(ang), np.sin(ang)
    return (np.concatenate([c, c], axis=-1).astype(np.float32),
            np.concatenate([-s, s], axis=-1).astype(np.float32))


def _dft_matrices(n):
    idx = np.arange(n)
    ang = ((idx[:, None] * idx[None, :]) % n) * (2.0 * math.pi / n)
    return np.cos(ang) * n ** -0.5, np.sin(ang) * n ** -0.5


def _sym_dft_tables(n):
    half = n // 2
    k = np.arange(half + DFT_PAD_ROWS)[:, None]
    ang = ((k * np.arange(half)[None, :]) % n) * (2.0 * math.pi / n)
    scale = n ** -0.5
    valid = k <= half
    cos_m = np.where(valid, np.cos(ang) * scale, 0.0).astype(BF16)
    sin_m = (np.sin(ang[:half]) * scale).astype(BF16)
    sgn = np.where(valid, (1 - 2 * (k % 2)) * scale, 0.0).astype(np.float32)
    i = np.arange(FLIP_BLOCK)
    flip_m = (i[:, None] + i[None, :] == FLIP_BLOCK).astype(BF16)
    return cos_m, sin_m, flip_m, sgn


def _deinterleave(v):
    return jnp.concatenate([v[..., 0::2], v[..., 1::2]], axis=-1)


def kernel(x, c, ctx, c_ctx, w_mod, b_mod, g_pre_mix, g_post_mix, g_pre_ffn, g_post_ffn,
           w_in, q_norm, k_norm, w_four, g_attn_out, g_four_out, w_out,
           w_up, conv_w, conv_b, w_down):
    b, n, d = x.shape
    ctx_len = ctx.shape[1]
    depth = w_mod.shape[0]

    perm_matrix = _deinterleave(np.eye(HEAD_DIM, dtype=BF16))
    w_in_b = _prep_w_in(w_in, perm_matrix)
    qn = _deinterleave(q_norm).reshape(depth, 1, HEAD_DIM)
    kn = _deinterleave(k_norm).reshape(depth, 1, HEAD_DIM)
    cos_t, sin_t = _rope_tables(n)
    dft_tables = _sym_dft_tables(n)
    cos_c, sin_c = (m.astype(BF16) for m in _dft_matrices(ctx_len))
    cs_ch = np.stack(_dft_matrices(FOURIER_GROUP)).astype(np.float32)
    m_fold = _fold_channel_dft(cs_ch, w_four)
    rows3 = lambda v: v.reshape(depth, 1, -1)
    g_pre_mix, g_post_mix, g_pre_ffn, g_post_ffn, g_attn_out, g_four_out, conv_b = map(
        rows3, (g_pre_mix, g_post_mix, g_pre_ffn, g_post_ffn, g_attn_out, g_four_out, conv_b))

    cc = jnp.zeros((MOD_ROWS, d), F32).at[:b].set(c).at[b].set(c_ctx)
    mod_all = _mod_proj(cc, w_mod, b_mod).reshape(depth, MOD_ROWS, N_MOD, d)

    xl = x.reshape(b * n, d)
    xc = ctx.reshape(b * ctx_len, d)
    tm = 512
    lat_row = lambda i: i // (n // tm)
    ctx_row = lambda i: b

    for i in range(depth):
        last = i == depth - 1
        q_l, k_l, v_l, ab_l, w_up_b = _in_proj(
            xl, mod_all, g_pre_mix, w_in_b, qn, kn, cos_t, sin_t, m_fold,
            layer=i, mod_row=lat_row, seq=n, rope=True, tm=tm, cast=(w_up,))
        if last:
            k_c, v_c = _ctx_kv(xc, mod_all, g_pre_mix, w_in_b, kn, layer=i, mod_row=ctx_row, tm=tm)
        else:
            q_c, k_c, v_c, ab_c = _in_proj(
                xc, mod_all, g_pre_mix, w_in_b, qn, kn, cos_t, sin_t, m_fold,
                layer=i, mod_row=ctx_row, seq=ctx_len, rope=False, tm=ctx_len)

        attn_l, w_out_b, w_down_b = _attention(
            q_l, [(k_l, v_l, n), (k_c, v_c, ctx_len)], g_attn_out,
            layer=i, q_seq=n, tq=512, cast=(w_out, w_down))
        four_l = _pos_dft_sym(dft_tables, ab_l, g_four_out, layer=i)
        xl, h_l = _out_proj(attn_l, four_l, w_out_b, xl, mod_all, g_post_mix, g_pre_ffn,
                            layer=i, mod_row=lat_row, tm=tm)
        u_l = _ffn_up(h_l, w_up_b, conv_w, conv_b, layer=i, seg_len=n, tm=n, tn=512)
        xl = _ffn_down_skew(u_l, w_down_b, xl, mod_all, g_post_ffn, layer=i,
                            mod_row=lambda r: r // (n // 1024), tm=1024, tn=512)

        if not last:
            attn_c, = _attention(q_c, [(k_c, v_c, ctx_len)], g_attn_out,
                                 layer=i, q_seq=ctx_len, tq=ctx_len)
            four_c = _pos_dft(cos_c, sin_c, ab_c, g_four_out, layer=i, tm=ctx_len)
            xc, h_c = _out_proj(attn_c, four_c, w_out_b, xc, mod_all, g_post_mix, g_pre_ffn,
                                layer=i, mod_row=ctx_row, tm=tm)
            u_c = _ffn_up(h_c, w_up_b, conv_w, conv_b, layer=i, seg_len=ctx_len,
                          tm=b * ctx_len, tn=512)
            xc = _ffn_down(u_c, w_down_b, xc, mod_all, g_post_ffn, layer=i, mod_row=ctx_row,
                           tm=tm, tn=512)

    return xl.reshape(b, n, d)
```

```python
import functools
import math

import jax
import jax.numpy as jnp
import numpy as np
from jax import lax
from jax.experimental import pallas as pl
from jax.experimental.pallas import tpu as pltpu

GRID_W = 64
HEAD_DIM = 128
N_Q_HEADS = 8
N_KV_HEADS = 2
Q_PER_KV = N_Q_HEADS // N_KV_HEADS
FOURIER_GROUP = 128
N_FOURIER_GROUPS = 8
ATTN_WIDTH = N_Q_HEADS * HEAD_DIM
KV_WIDTH = N_KV_HEADS * HEAD_DIM
FOURIER_WIDTH = N_FOURIER_GROUPS * FOURIER_GROUP
Q_END = ATTN_WIDTH
K_END = Q_END + KV_WIDTH
V_END = K_END + KV_WIDTH
IN_WIDTH = V_END + FOURIER_WIDTH
ROPE_THETA = 10000.0
EPS = 1e-6
N_MOD = 6
Q_SCALE = HEAD_DIM ** -0.5 * math.log2(math.e)
MOD_ROWS = 16

VMEM_LIMIT_BYTES = 56 * 1024 * 1024

F32 = jnp.float32
BF16 = jnp.bfloat16


def _params(*semantics):
    return pltpu.CompilerParams(dimension_semantics=semantics, vmem_limit_bytes=VMEM_LIMIT_BYTES)


def _rms(x, g):
    ms = jnp.mean(x * x, axis=-1, keepdims=True)
    return x * lax.rsqrt(ms + EPS) * g


def _const_spec(shape):
    return pl.BlockSpec(shape, lambda *_: (0,) * len(shape), pipeline_mode=pl.Buffered(1))


def _layer_spec(shape, layer):
    zeros = (0,) * len(shape)
    return pl.BlockSpec((None,) + shape, lambda *_: (layer,) + zeros, pipeline_mode=pl.Buffered(1))


def _side_casts(sources, layer, n_steps, step_of):
    in_specs, out_specs, out_shapes = [], [], []
    for src in sources:
        _, r, c = src.shape
        rows = r // n_steps
        in_specs.append(pl.BlockSpec((None, rows, c), lambda *g: (layer, step_of(*g), 0)))
        out_specs.append(pl.BlockSpec((rows, c), lambda *g: (step_of(*g), 0)))
        out_shapes.append(jax.ShapeDtypeStruct((r, c), BF16))
    return in_specs, out_specs, out_shapes


def _run_side_casts(src_refs, dst_refs):
    for src, dst in zip(src_refs, dst_refs, strict=True):
        dst[...] = src[...].astype(BF16)


def _mod_spec(d, layer, row_of):
    return pl.BlockSpec((None, 1, N_MOD, d), lambda i, *_: (layer, row_of(i), 0, 0))


def _mod_kernel(c_ref, w_ref, b_ref, o_ref):
    c = c_ref[...]
    s = (c * jax.nn.sigmoid(c)).astype(BF16)
    o_ref[0] = jnp.dot(s, w_ref[0].astype(BF16), preferred_element_type=F32) + b_ref[0]


def _mod_proj(cc, w_mod, b_mod, tn=1024):
    depth, d, width = w_mod.shape
    return pl.pallas_call(
        _mod_kernel,
        grid=(depth, width // tn),
        in_specs=[
            pl.BlockSpec((MOD_ROWS, d), lambda l, j: (0, 0)),
            pl.BlockSpec((1, d, tn), lambda l, j: (l, 0, j)),
            pl.BlockSpec((1, 1, tn), lambda l, j: (l, 0, j)),
        ],
        out_specs=pl.BlockSpec((1, MOD_ROWS, tn), lambda l, j: (l, 0, j)),
        out_shape=jax.ShapeDtypeStruct((depth, MOD_ROWS, width), F32),
        compiler_params=_params("arbitrary", "arbitrary"),
        name="mod_proj",
    )(cc, w_mod, b_mod.reshape(depth, 1, width))


def _fold_kernel(cs_ref, w_ref, o_ref):
    w = w_ref[0, 0]
    o_ref[0, 0, :, :FOURIER_GROUP] = jnp.dot(
        cs_ref[0], w, preferred_element_type=F32, precision=lax.Precision.HIGHEST).astype(BF16)
    o_ref[0, 0, :, FOURIER_GROUP:] = jnp.dot(
        cs_ref[1], w, preferred_element_type=F32, precision=lax.Precision.HIGHEST).astype(BF16)


def _fold_channel_dft(cs, w_four):
    depth, groups, c, _ = w_four.shape
    return pl.pallas_call(
        _fold_kernel,
        grid=(depth, groups),
        in_specs=[
            pl.BlockSpec((2, c, c), lambda l, g: (0, 0, 0)),
            pl.BlockSpec((1, 1, c, c), lambda l, g: (l, g, 0, 0)),
        ],
        out_specs=pl.BlockSpec((1, 1, c, 2 * c), lambda l, g: (l, g, 0, 0)),
        out_shape=jax.ShapeDtypeStruct((depth, groups, c, 2 * c), BF16),
        compiler_params=_params("arbitrary", "arbitrary"),
        name="fold_channel_dft",
    )(cs, w_four)


def _prep_w_in_kernel(w_ref, p_ref, o_ref):
    j = pl.program_id(1)
    w = w_ref[0].astype(BF16)

    @pl.when(j < N_Q_HEADS + N_KV_HEADS)
    def _():
        o_ref[0] = jnp.dot(w, p_ref[...], preferred_element_type=F32).astype(BF16)

    @pl.when(j >= N_Q_HEADS + N_KV_HEADS)
    def _():
        o_ref[0] = w


def _prep_w_in(w_in, perm_matrix):
    depth, d, width = w_in.shape
    return pl.pallas_call(
        _prep_w_in_kernel,
        grid=(depth, width // HEAD_DIM),
        in_specs=[
            pl.BlockSpec((1, d, HEAD_DIM), lambda l, j: (l, 0, j)),
            pl.BlockSpec((HEAD_DIM, HEAD_DIM), lambda l, j: (0, 0)),
        ],
        out_specs=pl.BlockSpec((1, d, HEAD_DIM), lambda l, j: (l, 0, j)),
        out_shape=jax.ShapeDtypeStruct((depth, d, width), BF16),
        compiler_params=_params("arbitrary", "arbitrary"),
        name="prep_w_in",
    )(w_in, perm_matrix)


def _in_proj_kernel(x_ref, mod_ref, gpre_ref, w_ref, qn_ref, kn_ref, cos_ref, sin_ref, m_ref, *rest,
                    rope, n_sub, n_cast):
    cast_src, (q_ref, k_ref, v_ref, ab_ref), cast_dst = rest[:n_cast], rest[n_cast:n_cast + 4], rest[n_cast + 4:]
    _run_side_casts(cast_src, cast_dst)
    mod = mod_ref[0]
    scale_g = (1.0 + mod[1:2]) * gpre_ref[...]
    rows = x_ref.shape[0] // n_sub
    for s in range(n_sub):
        r = slice(s * rows, (s + 1) * rows)
        h = _rms(x_ref[r, :], scale_g) + mod[0:1]
        p = jnp.dot(h.astype(BF16), w_ref[...], preferred_element_type=F32)

        def head(col, gain, post_scale):
            y = _rms(p[:, col:col + HEAD_DIM], gain)
            if rope:
                y = y * cos_ref[r, :] + pltpu.roll(y, HEAD_DIM // 2, axis=1) * sin_ref[r, :]
            if post_scale != 1.0:
                y = y * post_scale
            return y.astype(BF16)

        for i in range(N_Q_HEADS):
            q_ref[r, i * HEAD_DIM:(i + 1) * HEAD_DIM] = head(i * HEAD_DIM, qn_ref[...], Q_SCALE)
        for i in range(N_KV_HEADS):
            k_ref[r, i * HEAD_DIM:(i + 1) * HEAD_DIM] = head(Q_END + i * HEAD_DIM, kn_ref[...], 1.0)
        v_ref[r, :] = p[:, K_END:V_END].astype(BF16)
        for g in range(N_FOURIER_GROUPS):
            lo = V_END + g * FOURIER_GROUP
            ab = jnp.dot(p[:, lo:lo + FOURIER_GROUP].astype(BF16), m_ref[g], preferred_element_type=F32)
            ab_ref[r, g * FOURIER_GROUP:(g + 1) * FOURIER_GROUP] = ab[:, :FOURIER_GROUP].astype(BF16)
            ab_ref[r, FOURIER_WIDTH + g * FOURIER_GROUP:FOURIER_WIDTH + (g + 1) * FOURIER_GROUP] = (
                ab[:, FOURIER_GROUP:].astype(BF16))


def _in_proj(x2d, mod_all, g_pre, w_in, qn, kn, cos_t, sin_t, m_fold, *, layer, mod_row, seq, rope, tm,
             n_sub=1, cast=()):
    t, d = x2d.shape
    per_seq = seq // tm
    rope_map = (lambda i: (i % per_seq, 0)) if rope else (lambda i: (0, 0))
    cast_in, cast_out, cast_shapes = _side_casts(cast, layer, t // tm, lambda i: i)
    return pl.pallas_call(
        functools.partial(_in_proj_kernel, rope=rope, n_sub=n_sub, n_cast=len(cast)),
        grid=(t // tm,),
        in_specs=[
            pl.BlockSpec((tm, d), lambda i: (i, 0)),
            _mod_spec(d, layer, mod_row),
            _layer_spec((1, d), layer),
            _layer_spec((d, IN_WIDTH), layer),
            _layer_spec((1, HEAD_DIM), layer),
            _layer_spec((1, HEAD_DIM), layer),
            pl.BlockSpec((tm, HEAD_DIM), rope_map),
            pl.BlockSpec((tm, HEAD_DIM), rope_map),
            _layer_spec((N_FOURIER_GROUPS, FOURIER_GROUP, 2 * FOURIER_GROUP), layer),
        ] + cast_in,
        out_specs=[
            pl.BlockSpec((tm, ATTN_WIDTH), lambda i: (i, 0)),
            pl.BlockSpec((tm, KV_WIDTH), lambda i: (i, 0)),
            pl.BlockSpec((tm, KV_WIDTH), lambda i: (i, 0)),
            pl.BlockSpec((tm, 2 * FOURIER_WIDTH), lambda i: (i, 0)),
        ] + cast_out,
        out_shape=[
            jax.ShapeDtypeStruct((t, ATTN_WIDTH), BF16),
            jax.ShapeDtypeStruct((t, KV_WIDTH), BF16),
            jax.ShapeDtypeStruct((t, KV_WIDTH), BF16),
            jax.ShapeDtypeStruct((t, 2 * FOURIER_WIDTH), BF16),
        ] + cast_shapes,
        compiler_params=_params("arbitrary"),
        name="in_proj",
    )(x2d, mod_all, g_pre, w_in, qn, kn, cos_t, sin_t, m_fold, *cast)


def _ctx_kv_kernel(x_ref, mod_ref, gpre_ref, w_ref, kn_ref, k_ref, v_ref):
    mod = mod_ref[0]
    h = _rms(x_ref[...], (1.0 + mod[1:2]) * gpre_ref[...]) + mod[0:1]
    p = jnp.dot(h.astype(BF16), w_ref[...], preferred_element_type=F32)
    for i in range(N_KV_HEADS):
        cols = slice(i * HEAD_DIM, (i + 1) * HEAD_DIM)
        k_ref[:, cols] = _rms(p[:, cols], kn_ref[...]).astype(BF16)
    v_ref[...] = p[:, KV_WIDTH:].astype(BF16)


def _ctx_kv(x2d, mod_all, g_pre, w_in, kn, *, layer, mod_row, tm):
    t, d = x2d.shape
    assert Q_END % (2 * KV_WIDTH) == 0
    return pl.pallas_call(
        _ctx_kv_kernel,
        grid=(t // tm,),
        in_specs=[
            pl.BlockSpec((tm, d), lambda i: (i, 0)),
            _mod_spec(d, layer, mod_row),
            _layer_spec((1, d), layer),
            pl.BlockSpec((None, d, 2 * KV_WIDTH), lambda i: (layer, 0, Q_END // (2 * KV_WIDTH)),
                         pipeline_mode=pl.Buffered(1)),
            _layer_spec((1, HEAD_DIM), layer),
        ],
        out_specs=[pl.BlockSpec((tm, KV_WIDTH), lambda i: (i, 0))] * 2,
        out_shape=[jax.ShapeDtypeStruct((t, KV_WIDTH), BF16)] * 2,
        compiler_params=_params("arbitrary"),
        name="ctx_kv",
    )(x2d, mod_all, g_pre, w_in, kn)


def _attn_kernel(*refs, n_seg, n_cast):
    q_ref = refs[0]
    kv_refs = refs[1:1 + 2 * n_seg]
    g_ref = refs[1 + 2 * n_seg]
    rest = refs[2 + 2 * n_seg:]
    cast_src, o_ref, cast_dst, acc_ref = rest[:n_cast], rest[n_cast], rest[n_cast + 1:-1], rest[-1]
    _run_side_casts(cast_src, cast_dst)
    for h in range(N_Q_HEADS):
        kv = (h // Q_PER_KV) * HEAD_DIM
        qh = q_ref[:, h * HEAD_DIM:(h + 1) * HEAD_DIM]
        scores = [
            lax.dot_general(qh, kv_refs[2 * i][:, kv:kv + HEAD_DIM], (((1,), (1,)), ((), ())),
                            preferred_element_type=F32)
            for i in range(n_seg)
        ]
        m = functools.reduce(jnp.maximum, [jnp.max(s, axis=-1, keepdims=True) for s in scores])
        probs = [jnp.exp2(s - m) for s in scores]
        denom = functools.reduce(jnp.add, [jnp.sum(p, axis=-1, keepdims=True) for p in probs])
        out = functools.reduce(jnp.add, [
            jnp.dot(probs[i].astype(BF16), kv_refs[2 * i + 1][:, kv:kv + HEAD_DIM],
                    preferred_element_type=F32)
            for i in range(n_seg)
        ])
        acc_ref[:, h * HEAD_DIM:(h + 1) * HEAD_DIM] = out / denom
    o_ref[...] = _rms(acc_ref[...], g_ref[...]).astype(BF16)


def _attn_t_kernel(*refs, n_seg, n_cast):
    q_ref = refs[0]
    kv_refs = refs[1:1 + 2 * n_seg]
    g_ref = refs[1 + 2 * n_seg]
    rest = refs[2 + 2 * n_seg:]
    cast_src, o_ref, cast_dst, acc_ref = rest[:n_cast], rest[n_cast], rest[n_cast + 1:-1], rest[-1]
    _run_side_casts(cast_src, cast_dst)

    def head_scores(h):
        kv = (h // Q_PER_KV) * HEAD_DIM
        qh = q_ref[:, h * HEAD_DIM:(h + 1) * HEAD_DIM]
        return [
            lax.dot_general(kv_refs[2 * i][:, kv:kv + HEAD_DIM], qh, (((1,), (1,)), ((), ())),
                            preferred_element_type=F32)
            for i in range(n_seg)
        ]

    def head_softmax(scores):
        m = functools.reduce(jnp.maximum, [jnp.max(s, axis=0, keepdims=True) for s in scores])
        probs = [jnp.exp2(s - m) for s in scores]
        denom = functools.reduce(jnp.add, [jnp.sum(p, axis=0, keepdims=True) for p in probs])
        return [p.astype(BF16) for p in probs], denom

    def head_output(h, probs, denom):
        kv = (h // Q_PER_KV) * HEAD_DIM
        out_t = functools.reduce(jnp.add, [
            lax.dot_general(kv_refs[2 * i + 1][:, kv:kv + HEAD_DIM], probs[i],
                            (((0,), (0,)), ((), ())), preferred_element_type=F32)
            for i in range(n_seg)
        ])
        acc_ref[:, h * HEAD_DIM:(h + 1) * HEAD_DIM] = (out_t / denom).T

    scores = {0: head_scores(0)}
    pending = None
    for h in range(N_Q_HEADS):
        if pending is not None:
            head_output(h - 1, *pending)
        if h + 1 < N_Q_HEADS:
            scores[h + 1] = head_scores(h + 1)
        pending = head_softmax(scores.pop(h))
    head_output(N_Q_HEADS - 1, *pending)
    o_ref[...] = _rms(acc_ref[...], g_ref[...]).astype(BF16)


def _attention(q, segments, g_attn, *, layer, q_seq, tq, cast=(), transposed=False):
    t = q.shape[0]
    per_seq = q_seq // tq
    in_specs = [pl.BlockSpec((tq, ATTN_WIDTH), lambda b, j: (b * per_seq + j, 0))]
    args = [q]
    for k, v, rows in segments:
        in_specs += [pl.BlockSpec((rows, KV_WIDTH), lambda b, j: (b, 0))] * 2
        args += [k, v]
    in_specs.append(_layer_spec((1, ATTN_WIDTH), layer))
    args.append(g_attn)
    cast_in, cast_out, cast_shapes = _side_casts(cast, layer, t // tq, lambda b, j: b * per_seq + j)
    return pl.pallas_call(
        functools.partial(_attn_t_kernel if transposed else _attn_kernel,
                          n_seg=len(segments), n_cast=len(cast)),
        grid=(t // q_seq, per_seq),
        in_specs=in_specs + cast_in,
        out_specs=[pl.BlockSpec((tq, ATTN_WIDTH), lambda b, j: (b * per_seq + j, 0))] + cast_out,
        out_shape=[jax.ShapeDtypeStruct((t, ATTN_WIDTH), BF16)] + cast_shapes,
        scratch_shapes=[pltpu.VMEM((tq, ATTN_WIDTH), F32)],
        compiler_params=_params("arbitrary", "arbitrary"),
        name="attention",
    )(*args, *cast)


def _pos_dft_kernel(c_ref, s_ref, a_ref, b_ref, g_ref, o_ref):
    y = (jnp.dot(c_ref[...], a_ref[...], preferred_element_type=F32)
         - jnp.dot(s_ref[...], b_ref[...], preferred_element_type=F32))
    o_ref[...] = _rms(y, g_ref[...]).astype(BF16)


def _pos_dft(cos_m, sin_m, ab, g_four, *, layer, tm):
    seq = cos_m.shape[0]
    t = ab.shape[0]
    per_seq = seq // tm
    return pl.pallas_call(
        _pos_dft_kernel,
        grid=(t // seq, per_seq),
        in_specs=[
            pl.BlockSpec((tm, seq), lambda b, j: (j, 0)),
            pl.BlockSpec((tm, seq), lambda b, j: (j, 0)),
            pl.BlockSpec((seq, FOURIER_WIDTH), lambda b, j: (b, 0)),
            pl.BlockSpec((seq, FOURIER_WIDTH), lambda b, j: (b, 1)),
            _layer_spec((1, FOURIER_WIDTH), layer),
        ],
        out_specs=pl.BlockSpec((tm, FOURIER_WIDTH), lambda b, j: (b * per_seq + j, 0)),
        out_shape=jax.ShapeDtypeStruct((t, FOURIER_WIDTH), BF16),
        compiler_params=_params("arbitrary", "arbitrary"),
        name="pos_dft",
    )(cos_m, sin_m, ab, ab, g_four)


FLIP_BLOCK = 256
DFT_PAD_ROWS = 16


def _pos_dft_sym_kernel(alo_ref, ahi_ref, blo_ref, bhi_ref, c_ref, s_ref, q_ref, sgn_ref, g_ref,
                        o_ref, e_ref, d_ref, h_ref):
    half = alo_ref.shape[0]
    blk = q_ref.shape[0]
    nblk = half // blk
    qm = q_ref[...]
    is_row0 = lax.broadcasted_iota(jnp.int32, (blk, 1), 0) == 0

    def flipped_block(src_ref, bt, first_row):
        lo = (nblk - 1 - bt) * blk
        y = jnp.dot(qm, src_ref[lo:lo + blk, :], preferred_element_type=F32)
        if bt > 0:
            first_row = src_ref[lo + blk:lo + blk + 1, :].astype(F32)
        return jnp.where(is_row0, first_row, y)

    for bt in range(nblk):
        r = slice(bt * blk, (bt + 1) * blk)
        e_ref[r, :] = (alo_ref[r, :].astype(F32) + flipped_block(ahi_ref, bt, 0.0)).astype(BF16)
        d_ref[r, :] = (blo_ref[r, :].astype(F32) - flipped_block(bhi_ref, bt, 0.0)).astype(BF16)

    a_mid = ahi_ref[0:1, :].astype(F32)
    g = g_ref[...]
    mid_row = None
    for kb in range(nblk):
        r = slice(kb * blk, (kb + 1) * blk)
        rows = blk + (DFT_PAD_ROWS if kb == nblk - 1 else 0)
        cr = slice(kb * blk, kb * blk + rows)
        yc = jnp.dot(c_ref[cr, :], e_ref[...], preferred_element_type=F32) + sgn_ref[cr, :] * a_mid
        ys = jnp.dot(s_ref[r, :], d_ref[...], preferred_element_type=F32)
        o_ref[r, :] = _rms(yc[:blk] - ys, g).astype(BF16)
        h_ref[r, :] = _rms(yc[:blk] + ys, g).astype(BF16)
        if kb == nblk - 1:
            mid_row = _rms(yc[blk:blk + 1], g)

    for bs in range(nblk):
        r = slice(half + bs * blk, half + (bs + 1) * blk)
        o_ref[r, :] = flipped_block(h_ref, bs, mid_row).astype(BF16)


def _pos_dft_sym(tables, ab, g_four, *, layer):
    cos_m, sin_m, flip_m, sgn = tables
    half = sin_m.shape[0]
    t = ab.shape[0]
    blocks = lambda row, col: pl.BlockSpec((half, FOURIER_WIDTH), lambda b: (2 * b + row, col))
    return pl.pallas_call(
        _pos_dft_sym_kernel,
        grid=(t // (2 * half),),
        in_specs=[
            blocks(0, 0), blocks(1, 0), blocks(0, 1), blocks(1, 1),
            _const_spec(cos_m.shape),
            _const_spec(sin_m.shape),
            _const_spec(flip_m.shape),
            _const_spec(sgn.shape),
            _layer_spec((1, FOURIER_WIDTH), layer),
        ],
        out_specs=pl.BlockSpec((2 * half, FOURIER_WIDTH), lambda b: (b, 0)),
        out_shape=jax.ShapeDtypeStruct((t, FOURIER_WIDTH), BF16),
        scratch_shapes=[pltpu.VMEM((half, FOURIER_WIDTH), BF16)] * 3,
        compiler_params=_params("arbitrary"),
        name="pos_dft_sym",
    )(ab, ab, ab, ab, cos_m, sin_m, flip_m, sgn, g_four)


def _out_proj_kernel(a_ref, f_ref, w_ref, x_ref, mod_ref, gpost_ref, gpre_ref, xo_ref, h_ref, *, n_sub):
    mod = mod_ref[0]
    gate_g = mod[2:3] * gpost_ref[...]
    scale_g = (1.0 + mod[4:5]) * gpre_ref[...]
    rows = x_ref.shape[0] // n_sub
    for s in range(n_sub):
        r = slice(s * rows, (s + 1) * rows)
        mix = (jnp.dot(a_ref[r, :], w_ref[:ATTN_WIDTH, :], preferred_element_type=F32)
               + jnp.dot(f_ref[r, :], w_ref[ATTN_WIDTH:, :], preferred_element_type=F32))
        xn = x_ref[r, :] + _rms(mix, gate_g)
        xo_ref[r, :] = xn
        h_ref[r, :] = (_rms(xn, scale_g) + mod[3:4]).astype(BF16)


def _out_proj(a, f, w_out, x2d, mod_all, g_post, g_pre_ffn, *, layer, mod_row, tm, n_sub=4):
    t, d = x2d.shape
    return pl.pallas_call(
        functools.partial(_out_proj_kernel, n_sub=n_sub),
        grid=(t // tm,),
        in_specs=[
            pl.BlockSpec((tm, ATTN_WIDTH), lambda i: (i, 0)),
            pl.BlockSpec((tm, FOURIER_WIDTH), lambda i: (i, 0)),
            _const_spec((ATTN_WIDTH + FOURIER_WIDTH, d)),
            pl.BlockSpec((tm, d), lambda i: (i, 0)),
            _mod_spec(d, layer, mod_row),
            _layer_spec((1, d), layer),
            _layer_spec((1, d), layer),
        ],
        out_specs=[
            pl.BlockSpec((tm, d), lambda i: (i, 0)),
            pl.BlockSpec((tm, d), lambda i: (i, 0)),
        ],
        out_shape=[
            jax.ShapeDtypeStruct((t, d), F32),
            jax.ShapeDtypeStruct((t, d), BF16),
        ],
        compiler_params=_params("arbitrary"),
        name="out_proj",
    )(a, f, w_out, x2d, mod_all, g_post, g_pre_ffn)


def _ffn_up_kernel(h_ref, wg_ref, wv_ref, cw_ref, cb_ref, o_ref, *, seg_len, n_sub):
    h = h_ref[...]
    rows, tn = o_ref.shape
    width = tn // n_sub
    pos = lax.broadcasted_iota(jnp.int32, (rows, 1), 0) % seg_len
    first, final = pos == 0, pos == seg_len - 1
    for s in range(n_sub):
        cols = slice(s * width, (s + 1) * width)
        gate = jnp.dot(h, wg_ref[:, cols], preferred_element_type=F32)
        val = jnp.dot(h, wv_ref[:, cols], preferred_element_type=F32)
        prev = jnp.where(first, 0.0, pltpu.roll(gate, 1, axis=0))
        nxt = jnp.where(final, 0.0, pltpu.roll(gate, rows - 1, axis=0))
        cw = cw_ref[:, cols]
        c = prev * cw[0:1] + gate * cw[1:2] + nxt * cw[2:3] + cb_ref[:, cols]
        act = 0.5 * c * (1.0 + jnp.tanh(math.sqrt(2.0 / math.pi) * (c + 0.044715 * (c * c * c))))
        o_ref[:, cols] = (act * val).astype(BF16)


def _ffn_up(h, w_up, conv_w, conv_b, *, layer, seg_len, tm, tn, n_sub=1):
    t, d = h.shape
    ffn = conv_w.shape[-1]
    n_col = ffn // tn
    return pl.pallas_call(
        functools.partial(_ffn_up_kernel, seg_len=seg_len, n_sub=n_sub),
        grid=(t // tm, n_col),
        in_specs=[
            pl.BlockSpec((tm, d), lambda i, j: (i, 0)),
            pl.BlockSpec((d, tn), lambda i, j: (0, j)),
            pl.BlockSpec((d, tn), lambda i, j: (0, j + n_col)),
            pl.BlockSpec((None, 3, tn), lambda i, j: (layer, 0, j)),
            pl.BlockSpec((None, 1, tn), lambda i, j: (layer, 0, j)),
        ],
        out_specs=pl.BlockSpec((tm, tn), lambda i, j: (i, j)),
        out_shape=jax.ShapeDtypeStruct((t, ffn), BF16),
        compiler_params=_params("arbitrary", "arbitrary"),
        name="ffn_up",
    )(h, w_up, w_up, conv_w, conv_b)


def _ffn_down_kernel(u_ref, w_ref, x_ref, mod_ref, gpost_ref, xo_ref, y_ref):
    j = pl.program_id(1)
    n_col, _, tn = y_ref.shape
    y_ref[j] = jnp.dot(u_ref[...], w_ref[...], preferred_element_type=F32)

    @pl.when(j == n_col - 1)
    def _():
        d = n_col * tn
        ssq = functools.reduce(
            jnp.add, [jnp.sum(y_ref[c] * y_ref[c], axis=-1, keepdims=True) for c in range(n_col)])
        inv = lax.rsqrt(ssq / d + EPS)
        gate = mod_ref[0][5:6]
        for c in range(n_col):
            cols = slice(c * tn, (c + 1) * tn)
            xo_ref[:, cols] = x_ref[:, cols] + gate[:, cols] * (y_ref[c] * inv * gpost_ref[:, cols])


def _ffn_down(u, w_down, x2d, mod_all, g_post, *, layer, mod_row, tm, tn):
    t, d = x2d.shape
    ffn = u.shape[1]
    n_col = d // tn
    return pl.pallas_call(
        _ffn_down_kernel,
        grid=(t // tm, n_col),
        in_specs=[
            pl.BlockSpec((tm, ffn), lambda i, j: (i, 0)),
            pl.BlockSpec((ffn, tn), lambda i, j: (0, j)),
            pl.BlockSpec((tm, d), lambda i, j: (i, 0)),
            _mod_spec(d, layer, mod_row),
            _layer_spec((1, d), layer),
        ],
        out_specs=pl.BlockSpec((tm, d), lambda i, j: (i, 0)),
        out_shape=jax.ShapeDtypeStruct((t, d), F32),
        scratch_shapes=[pltpu.VMEM((n_col, tm, tn), F32)],
        compiler_params=_params("arbitrary", "arbitrary"),
        name="ffn_down",
    )(u, w_down, x2d, mod_all, g_post)


def _ffn_down_skew_kernel(u_ref, w_ref, x_ref, mod_ref, gpost_ref, xo_ref, y_ref, inv_ref):
    i, j = pl.program_id(0), pl.program_id(1)
    n_col, _, tn = y_ref.shape

    @pl.when((i == 0) & (j == 0))
    def _():
        y_ref[...] = jnp.zeros_like(y_ref)
        inv_ref[...] = jnp.zeros_like(inv_ref)

    gate_g = mod_ref[0][N_MOD - 1:N_MOD] * gpost_ref[...]
    xo_ref[...] = x_ref[...] + (y_ref[j] * inv_ref[...]) * gate_g
    y_ref[j] = jnp.dot(u_ref[...], w_ref[...], preferred_element_type=F32)

    @pl.when(j == n_col - 1)
    def _():
        ssq = functools.reduce(
            jnp.add, [jnp.sum(y_ref[c] * y_ref[c], axis=-1, keepdims=True) for c in range(n_col)])
        inv_ref[...] = lax.rsqrt(ssq / (n_col * tn) + EPS)


def _ffn_down_skew(u, w_down, x2d, mod_all, g_post, *, layer, mod_row, tm, tn):
    t, d = x2d.shape
    ffn = u.shape[1]
    n_tiles = t // tm
    prev = lambda i: jnp.maximum(i - 1, 0)
    col = lambda i, j: jnp.where(i > 0, j, 0)
    return pl.pallas_call(
        _ffn_down_skew_kernel,
        grid=(n_tiles + 1, d // tn),
        in_specs=[
            pl.BlockSpec((tm, ffn), lambda i, j: (jnp.minimum(i, n_tiles - 1), 0)),
            pl.BlockSpec((ffn, tn), lambda i, j: (0, j)),
            pl.BlockSpec((tm, tn), lambda i, j: (prev(i), col(i, j))),
            pl.BlockSpec((None, 1, N_MOD, tn), lambda i, j: (layer, mod_row(prev(i)), 0, col(i, j))),
            pl.BlockSpec((None, 1, tn), lambda i, j: (layer, 0, col(i, j))),
        ],
        out_specs=pl.BlockSpec((tm, tn), lambda i, j: (prev(i), col(i, j))),
        out_shape=jax.ShapeDtypeStruct((t, d), F32),
        scratch_shapes=[pltpu.VMEM((d // tn, tm, tn), F32), pltpu.VMEM((tm, 1), F32)],
        compiler_params=_params("arbitrary", "arbitrary"),
        name="ffn_down_skew",
    )(u, w_down, x2d, mod_all, g_post)


def _rope_tables(n):
    pos = np.arange(n)
    n_pairs_axis = HEAD_DIM // 4
    freqs = ROPE_THETA ** (-np.arange(n_pairs_axis, dtype=np.float64) / n_pairs_axis)
    ang = np.concatenate([(pos // GRID_W)[:, None] * freqs, (pos % GRID_W)[:, None] * freqs], axis=-1)
    c, s = np.cos(ang), np.sin(ang)
    return (np.concatenate([c, c], axis=-1).astype(np.float32),
            np.concatenate([-s, s], axis=-1).astype(np.float32))


def _dft_matrices(n):
    idx = np.arange(n)
    ang = ((idx[:, None] * idx[None, :]) % n) * (2.0 * math.pi / n)
    return np.cos(ang) * n ** -0.5, np.sin(ang) * n ** -0.5


def _sym_dft_tables(n):
    half = n // 2
    k = np.arange(half + DFT_PAD_ROWS)[:, None]
    ang = ((k * np.arange(half)[None, :]) % n) * (2.0 * math.pi / n)
    scale = n ** -0.5
    valid = k <= half
    cos_m = np.where(valid, np.cos(ang) * scale, 0.0).astype(BF16)
    sin_m = (np.sin(ang[:half]) * scale).astype(BF16)
    sgn = np.where(valid, (1 - 2 * (k % 2)) * scale, 0.0).astype(np.float32)
    i = np.arange(FLIP_BLOCK)
    flip_m = (i[:, None] + i[None, :] == FLIP_BLOCK).astype(BF16)
    return cos_m, sin_m, flip_m, sgn


def _deinterleave(v):
    return jnp.concatenate([v[..., 0::2], v[..., 1::2]], axis=-1)


def kernel(x, c, ctx, c_ctx, w_mod, b_mod, g_pre_mix, g_post_mix, g_pre_ffn, g_post_ffn,
           w_in, q_norm, k_norm, w_four, g_attn_out, g_four_out, w_out,
           w_up, conv_w, conv_b, w_down):
    b, n, d = x.shape
    ctx_len = ctx.shape[1]
    depth = w_mod.shape[0]

    perm_matrix = _deinterleave(np.eye(HEAD_DIM, dtype=BF16))
    w_in_b = _prep_w_in(w_in, perm_matrix)
    qn = _deinterleave(q_norm).reshape(depth, 1, HEAD_DIM)
    kn = _deinterleave(k_norm).reshape(depth, 1, HEAD_DIM)
    cos_t, sin_t = _rope_tables(n)
    dft_tables = _sym_dft_tables(n)
    cos_c, sin_c = (m.astype(BF16) for m in _dft_matrices(ctx_len))
    cs_ch = np.stack(_dft_matrices(FOURIER_GROUP)).astype(np.float32)
    m_fold = _fold_channel_dft(cs_ch, w_four)
    rows3 = lambda v: v.reshape(depth, 1, -1)
    g_pre_mix, g_post_mix, g_pre_ffn, g_post_ffn, g_attn_out, g_four_out, conv_b = map(
        rows3, (g_pre_mix, g_post_mix, g_pre_ffn, g_post_ffn, g_attn_out, g_four_out, conv_b))

    cc = jnp.zeros((MOD_ROWS, d), F32).at[:b].set(c).at[b].set(c_ctx)
    mod_all = _mod_proj(cc, w_mod, b_mod).reshape(depth, MOD_ROWS, N_MOD, d)

    xl = x.reshape(b * n, d)
    xc = ctx.reshape(b * ctx_len, d)
    tm = 512
    lat_row = lambda i: i // (n // tm)
    ctx_row = lambda i: b

    for i in range(depth):
        last = i == depth - 1
        q_l, k_l, v_l, ab_l, w_up_b = _in_proj(
            xl, mod_all, g_pre_mix, w_in_b, qn, kn, cos_t, sin_t, m_fold,
            layer=i, mod_row=lat_row, seq=n, rope=True, tm=tm, cast=(w_up,))
        if last:
            k_c, v_c = _ctx_kv(xc, mod_all, g_pre_mix, w_in_b, kn, layer=i, mod_row=ctx_row, tm=tm)
        else:
            q_c, k_c, v_c, ab_c = _in_proj(
                xc, mod_all, g_pre_mix, w_in_b, qn, kn, cos_t, sin_t, m_fold,
                layer=i, mod_row=ctx_row, seq=ctx_len, rope=False, tm=ctx_len)

        attn_l, w_out_b, w_down_b = _attention(
            q_l, [(k_l, v_l, n), (k_c, v_c, ctx_len)], g_attn_out,
            layer=i, q_seq=n, tq=512, cast=(w_out, w_down), transposed=True)
        four_l = _pos_dft_sym(dft_tables, ab_l, g_four_out, layer=i)
        xl, h_l = _out_proj(attn_l, four_l, w_out_b, xl, mod_all, g_post_mix, g_pre_ffn,
                            layer=i, mod_row=lat_row, tm=tm)
        u_l = _ffn_up(h_l, w_up_b, conv_w, conv_b, layer=i, seg_len=n, tm=n, tn=512)
        xl = _ffn_down_skew(u_l, w_down_b, xl, mod_all, g_post_ffn, layer=i,
                            mod_row=lambda r: r // (n // 1024), tm=1024, tn=512)

        if not last:
            attn_c, = _attention(q_c, [(k_c, v_c, ctx_len)], g_attn_out,
                                 layer=i, q_seq=ctx_len, tq=ctx_len)
            four_c = _pos_dft(cos_c, sin_c, ab_c, g_four_out, layer=i, tm=ctx_len)
            xc, h_c = _out_proj(attn_c, four_c, w_out_b, xc, mod_all, g_post_mix, g_pre_ffn,
                                layer=i, mod_row=ctx_row, tm=tm)
            u_c = _ffn_up(h_c, w_up_b, conv_w, conv_b, layer=i, seg_len=ctx_len,
                          tm=b * ctx_len, tn=512)
            xc = _ffn_down(u_c, w_down_b, xc, mod_all, g_post_ffn, layer=i, mod_row=ctx_row,
                           tm=tm, tn=512)

    return xl.reshape(b, n, d)
```

```python
import functools
import math

import jax
import jax.numpy as jnp
import numpy as np
from jax import lax
from jax.experimental import pallas as pl
from jax.experimental.pallas import tpu as pltpu

GRID_W = 64
HEAD_DIM = 128
N_Q_HEADS = 8
N_KV_HEADS = 2
Q_PER_KV = N_Q_HEADS // N_KV_HEADS
FOURIER_GROUP = 128
N_FOURIER_GROUPS = 8
ATTN_WIDTH = N_Q_HEADS * HEAD_DIM
KV_WIDTH = N_KV_HEADS * HEAD_DIM
FOURIER_WIDTH = N_FOURIER_GROUPS * FOURIER_GROUP
Q_END = ATTN_WIDTH
K_END = Q_END + KV_WIDTH
V_END = K_END + KV_WIDTH
IN_WIDTH = V_END + FOURIER_WIDTH
ROPE_THETA = 10000.0
EPS = 1e-6
N_MOD = 6
Q_SCALE = HEAD_DIM ** -0.5 * math.log2(math.e)
MOD_ROWS = 16

VMEM_LIMIT_BYTES = 56 * 1024 * 1024

F32 = jnp.float32
BF16 = jnp.bfloat16


def _params(*semantics):
    return pltpu.CompilerParams(dimension_semantics=semantics, vmem_limit_bytes=VMEM_LIMIT_BYTES)


def _rms(x, g):
    ms = jnp.mean(x * x, axis=-1, keepdims=True)
    return x * lax.rsqrt(ms + EPS) * g


def _const_spec(shape):
    return pl.BlockSpec(shape, lambda *_: (0,) * len(shape), pipeline_mode=pl.Buffered(1))


def _layer_spec(shape, layer):
    zeros = (0,) * len(shape)
    return pl.BlockSpec((None,) + shape, lambda *_: (layer,) + zeros, pipeline_mode=pl.Buffered(1))


def _side_casts(sources, layer, n_steps, step_of):
    in_specs, out_specs, out_shapes = [], [], []
    for src in sources:
        _, r, c = src.shape
        rows = r // n_steps
        in_specs.append(pl.BlockSpec((None, rows, c), lambda *g: (layer, step_of(*g), 0)))
        out_specs.append(pl.BlockSpec((rows, c), lambda *g: (step_of(*g), 0)))
        out_shapes.append(jax.ShapeDtypeStruct((r, c), BF16))
    return in_specs, out_specs, out_shapes


def _run_side_casts(src_refs, dst_refs):
    for src, dst in zip(src_refs, dst_refs, strict=True):
        dst[...] = src[...].astype(BF16)


def _mod_spec(d, layer, row_of):
    return pl.BlockSpec((None, 1, N_MOD, d), lambda i, *_: (layer, row_of(i), 0, 0))


def _mod_kernel(c_ref, w_ref, b_ref, o_ref):
    c = c_ref[...]
    s = (c * jax.nn.sigmoid(c)).astype(BF16)
    o_ref[0] = jnp.dot(s, w_ref[0].astype(BF16), preferred_element_type=F32) + b_ref[0]


def _mod_proj(cc, w_mod, b_mod, tn=1024):
    depth, d, width = w_mod.shape
    return pl.pallas_call(
        _mod_kernel,
        grid=(depth, width // tn),
        in_specs=[
            pl.BlockSpec((MOD_ROWS, d), lambda l, j: (0, 0)),
            pl.BlockSpec((1, d, tn), lambda l, j: (l, 0, j)),
            pl.BlockSpec((1, 1, tn), lambda l, j: (l, 0, j)),
        ],
        out_specs=pl.BlockSpec((1, MOD_ROWS, tn), lambda l, j: (l, 0, j)),
        out_shape=jax.ShapeDtypeStruct((depth, MOD_ROWS, width), F32),
        compiler_params=_params("arbitrary", "arbitrary"),
        name="mod_proj",
    )(cc, w_mod, b_mod.reshape(depth, 1, width))


def _fold_kernel(cs_ref, w_ref, o_ref):
    w = w_ref[0, 0]
    o_ref[0, 0, :, :FOURIER_GROUP] = jnp.dot(
        cs_ref[0], w, preferred_element_type=F32, precision=lax.Precision.HIGHEST).astype(BF16)
    o_ref[0, 0, :, FOURIER_GROUP:] = jnp.dot(
        cs_ref[1], w, preferred_element_type=F32, precision=lax.Precision.HIGHEST).astype(BF16)


def _fold_channel_dft(cs, w_four):
    depth, groups, c, _ = w_four.shape
    return pl.pallas_call(
        _fold_kernel,
        grid=(depth, groups),
        in_specs=[
            pl.BlockSpec((2, c, c), lambda l, g: (0, 0, 0)),
            pl.BlockSpec((1, 1, c, c), lambda l, g: (l, g, 0, 0)),
        ],
        out_specs=pl.BlockSpec((1, 1, c, 2 * c), lambda l, g: (l, g, 0, 0)),
        out_shape=jax.ShapeDtypeStruct((depth, groups, c, 2 * c), BF16),
        compiler_params=_params("arbitrary", "arbitrary"),
        name="fold_channel_dft",
    )(cs, w_four)


def _prep_w_in_kernel(w_ref, p_ref, o_ref):
    j = pl.program_id(1)
    w = w_ref[0].astype(BF16)

    @pl.when(j < N_Q_HEADS + N_KV_HEADS)
    def _():
        o_ref[0] = jnp.dot(w, p_ref[...], preferred_element_type=F32).astype(BF16)

    @pl.when(j >= N_Q_HEADS + N_KV_HEADS)
    def _():
        o_ref[0] = w


def _prep_w_in(w_in, perm_matrix):
    depth, d, width = w_in.shape
    return pl.pallas_call(
        _prep_w_in_kernel,
        grid=(depth, width // HEAD_DIM),
        in_specs=[
            pl.BlockSpec((1, d, HEAD_DIM), lambda l, j: (l, 0, j)),
            pl.BlockSpec((HEAD_DIM, HEAD_DIM), lambda l, j: (0, 0)),
        ],
        out_specs=pl.BlockSpec((1, d, HEAD_DIM), lambda l, j: (l, 0, j)),
        out_shape=jax.ShapeDtypeStruct((depth, d, width), BF16),
        compiler_params=_params("arbitrary", "arbitrary"),
        name="prep_w_in",
    )(w_in, perm_matrix)


def _in_proj_kernel(x_ref, mod_ref, gpre_ref, w_ref, qn_ref, kn_ref, cos_ref, sin_ref, m_ref, *rest,
                    rope, n_sub, n_cast):
    cast_src, (q_ref, k_ref, v_ref, ab_ref), cast_dst = rest[:n_cast], rest[n_cast:n_cast + 4], rest[n_cast + 4:]
    _run_side_casts(cast_src, cast_dst)
    mod = mod_ref[0]
    scale_g = (1.0 + mod[1:2]) * gpre_ref[...]
    rows = x_ref.shape[0] // n_sub
    for s in range(n_sub):
        r = slice(s * rows, (s + 1) * rows)
        h = _rms(x_ref[r, :], scale_g) + mod[0:1]
        p = jnp.dot(h.astype(BF16), w_ref[...], preferred_element_type=F32)

        def head(col, gain, post_scale):
            y = _rms(p[:, col:col + HEAD_DIM], gain)
            if rope:
                y = y * cos_ref[r, :] + pltpu.roll(y, HEAD_DIM // 2, axis=1) * sin_ref[r, :]
            if post_scale != 1.0:
                y = y * post_scale
            return y.astype(BF16)

        for i in range(N_Q_HEADS):
            q_ref[r, i * HEAD_DIM:(i + 1) * HEAD_DIM] = head(i * HEAD_DIM, qn_ref[...], Q_SCALE)
        for i in range(N_KV_HEADS):
            k_ref[r, i * HEAD_DIM:(i + 1) * HEAD_DIM] = head(Q_END + i * HEAD_DIM, kn_ref[...], 1.0)
        v_ref[r, :] = p[:, K_END:V_END].astype(BF16)
        for g in range(N_FOURIER_GROUPS):
            lo = V_END + g * FOURIER_GROUP
            ab = jnp.dot(p[:, lo:lo + FOURIER_GROUP].astype(BF16), m_ref[g], preferred_element_type=F32)
            ab_ref[r, g * FOURIER_GROUP:(g + 1) * FOURIER_GROUP] = ab[:, :FOURIER_GROUP].astype(BF16)
            ab_ref[r, FOURIER_WIDTH + g * FOURIER_GROUP:FOURIER_WIDTH + (g + 1) * FOURIER_GROUP] = (
                ab[:, FOURIER_GROUP:].astype(BF16))


def _in_proj(x2d, mod_all, g_pre, w_in, qn, kn, cos_t, sin_t, m_fold, *, layer, mod_row, seq, rope, tm,
             n_sub=1, cast=()):
    t, d = x2d.shape
    per_seq = seq // tm
    rope_map = (lambda i: (i % per_seq, 0)) if rope else (lambda i: (0, 0))
    cast_in, cast_out, cast_shapes = _side_casts(cast, layer, t // tm, lambda i: i)
    return pl.pallas_call(
        functools.partial(_in_proj_kernel, rope=rope, n_sub=n_sub, n_cast=len(cast)),
        grid=(t // tm,),
        in_specs=[
            pl.BlockSpec((tm, d), lambda i: (i, 0)),
            _mod_spec(d, layer, mod_row),
            _layer_spec((1, d), layer),
            _layer_spec((d, IN_WIDTH), layer),
            _layer_spec((1, HEAD_DIM), layer),
            _layer_spec((1, HEAD_DIM), layer),
            pl.BlockSpec((tm, HEAD_DIM), rope_map),
            pl.BlockSpec((tm, HEAD_DIM), rope_map),
            _layer_spec((N_FOURIER_GROUPS, FOURIER_GROUP, 2 * FOURIER_GROUP), layer),
        ] + cast_in,
        out_specs=[
            pl.BlockSpec((tm, ATTN_WIDTH), lambda i: (i, 0)),
            pl.BlockSpec((tm, KV_WIDTH), lambda i: (i, 0)),
            pl.BlockSpec((tm, KV_WIDTH), lambda i: (i, 0)),
            pl.BlockSpec((tm, 2 * FOURIER_WIDTH), lambda i: (i, 0)),
        ] + cast_out,
        out_shape=[
            jax.ShapeDtypeStruct((t, ATTN_WIDTH), BF16),
            jax.ShapeDtypeStruct((t, KV_WIDTH), BF16),
            jax.ShapeDtypeStruct((t, KV_WIDTH), BF16),
            jax.ShapeDtypeStruct((t, 2 * FOURIER_WIDTH), BF16),
        ] + cast_shapes,
        compiler_params=_params("arbitrary"),
        name="in_proj",
    )(x2d, mod_all, g_pre, w_in, qn, kn, cos_t, sin_t, m_fold, *cast)


def _ctx_kv_kernel(x_ref, mod_ref, gpre_ref, w_ref, kn_ref, k_ref, v_ref):
    mod = mod_ref[0]
    h = _rms(x_ref[...], (1.0 + mod[1:2]) * gpre_ref[...]) + mod[0:1]
    p = jnp.dot(h.astype(BF16), w_ref[...], preferred_element_type=F32)
    for i in range(N_KV_HEADS):
        cols = slice(i * HEAD_DIM, (i + 1) * HEAD_DIM)
        k_ref[:, cols] = _rms(p[:, cols], kn_ref[...]).astype(BF16)
    v_ref[...] = p[:, KV_WIDTH:].astype(BF16)


def _ctx_kv(x2d, mod_all, g_pre, w_in, kn, *, layer, mod_row, tm):
    t, d = x2d.shape
    assert Q_END % (2 * KV_WIDTH) == 0
    return pl.pallas_call(
        _ctx_kv_kernel,
        grid=(t // tm,),
        in_specs=[
            pl.BlockSpec((tm, d), lambda i: (i, 0)),
            _mod_spec(d, layer, mod_row),
            _layer_spec((1, d), layer),
            pl.BlockSpec((None, d, 2 * KV_WIDTH), lambda i: (layer, 0, Q_END // (2 * KV_WIDTH)),
                         pipeline_mode=pl.Buffered(1)),
            _layer_spec((1, HEAD_DIM), layer),
        ],
        out_specs=[pl.BlockSpec((tm, KV_WIDTH), lambda i: (i, 0))] * 2,
        out_shape=[jax.ShapeDtypeStruct((t, KV_WIDTH), BF16)] * 2,
        compiler_params=_params("arbitrary"),
        name="ctx_kv",
    )(x2d, mod_all, g_pre, w_in, kn)


def _attn_kernel(*refs, n_seg, n_cast):
    q_ref = refs[0]
    kv_refs = refs[1:1 + 2 * n_seg]
    g_ref = refs[1 + 2 * n_seg]
    rest = refs[2 + 2 * n_seg:]
    cast_src, o_ref, cast_dst, acc_ref = rest[:n_cast], rest[n_cast], rest[n_cast + 1:-1], rest[-1]
    _run_side_casts(cast_src, cast_dst)
    for h in range(N_Q_HEADS):
        kv = (h // Q_PER_KV) * HEAD_DIM
        qh = q_ref[:, h * HEAD_DIM:(h + 1) * HEAD_DIM]
        scores = [
            lax.dot_general(qh, kv_refs[2 * i][:, kv:kv + HEAD_DIM], (((1,), (1,)), ((), ())),
                            preferred_element_type=F32)
            for i in range(n_seg)
        ]
        m = functools.reduce(jnp.maximum, [jnp.max(s, axis=-1, keepdims=True) for s in scores])
        probs = [jnp.exp2(s - m) for s in scores]
        denom = functools.reduce(jnp.add, [jnp.sum(p, axis=-1, keepdims=True) for p in probs])
        out = functools.reduce(jnp.add, [
            jnp.dot(probs[i].astype(BF16), kv_refs[2 * i + 1][:, kv:kv + HEAD_DIM],
                    preferred_element_type=F32)
            for i in range(n_seg)
        ])
        acc_ref[:, h * HEAD_DIM:(h + 1) * HEAD_DIM] = out / denom
    o_ref[...] = _rms(acc_ref[...], g_ref[...]).astype(BF16)


def _attn_t_kernel(*refs, n_seg, n_cast):
    q_ref = refs[0]
    kv_refs = refs[1:1 + 2 * n_seg]
    g_ref = refs[1 + 2 * n_seg]
    rest = refs[2 + 2 * n_seg:]
    cast_src, o_ref, cast_dst, acc_ref = rest[:n_cast], rest[n_cast], rest[n_cast + 1:-1], rest[-1]
    _run_side_casts(cast_src, cast_dst)

    def head_scores(h):
        kv = (h // Q_PER_KV) * HEAD_DIM
        qh = q_ref[:, h * HEAD_DIM:(h + 1) * HEAD_DIM]
        return [
            lax.dot_general(kv_refs[2 * i][:, kv:kv + HEAD_DIM], qh, (((1,), (1,)), ((), ())),
                            preferred_element_type=F32)
            for i in range(n_seg)
        ]

    def head_softmax(scores):
        m = functools.reduce(jnp.maximum, [jnp.max(s, axis=0, keepdims=True) for s in scores])
        probs = [jnp.exp2(s - m) for s in scores]
        denom = functools.reduce(jnp.add, [jnp.sum(p, axis=0, keepdims=True) for p in probs])
        return [p.astype(BF16) for p in probs], denom

    def head_output(h, probs, denom):
        kv = (h // Q_PER_KV) * HEAD_DIM
        out_t = functools.reduce(jnp.add, [
            lax.dot_general(kv_refs[2 * i + 1][:, kv:kv + HEAD_DIM], probs[i],
                            (((0,), (0,)), ((), ())), preferred_element_type=F32)
            for i in range(n_seg)
        ])
        acc_ref[:, h * HEAD_DIM:(h + 1) * HEAD_DIM] = (out_t / denom).T

    scores = {0: head_scores(0)}
    pending = None
    for h in range(N_Q_HEADS):
        if pending is not None:
            head_output(h - 1, *pending)
        if h + 1 < N_Q_HEADS:
            scores[h + 1] = head_scores(h + 1)
        pending = head_softmax(scores.pop(h))
    head_output(N_Q_HEADS - 1, *pending)
    o_ref[...] = _rms(acc_ref[...], g_ref[...]).astype(BF16)


def _attention(q, segments, g_attn, *, layer, q_seq, tq, cast=(), transposed=False):
    t = q.shape[0]
    per_seq = q_seq // tq
    in_specs = [pl.BlockSpec((tq, ATTN_WIDTH), lambda b, j: (b * per_seq + j, 0))]
    args = [q]
    for k, v, rows in segments:
        in_specs += [pl.BlockSpec((rows, KV_WIDTH), lambda b, j: (b, 0))] * 2
        args += [k, v]
    in_specs.append(_layer_spec((1, ATTN_WIDTH), layer))
    args.append(g_attn)
    cast_in, cast_out, cast_shapes = _side_casts(cast, layer, t // tq, lambda b, j: b * per_seq + j)
    return pl.pallas_call(
        functools.partial(_attn_t_kernel if transposed else _attn_kernel,
                          n_seg=len(segments), n_cast=len(cast)),
        grid=(t // q_seq, per_seq),
        in_specs=in_specs + cast_in,
        out_specs=[pl.BlockSpec((tq, ATTN_WIDTH), lambda b, j: (b * per_seq + j, 0))] + cast_out,
        out_shape=[jax.ShapeDtypeStruct((t, ATTN_WIDTH), BF16)] + cast_shapes,
        scratch_shapes=[pltpu.VMEM((tq, ATTN_WIDTH), F32)],
        compiler_params=_params("arbitrary", "arbitrary"),
        name="attention",
    )(*args, *cast)


def _pos_dft_kernel(c_ref, s_ref, a_ref, b_ref, g_ref, o_ref):
    y = (jnp.dot(c_ref[...], a_ref[...], preferred_element_type=F32)
         - jnp.dot(s_ref[...], b_ref[...], preferred_element_type=F32))
    o_ref[...] = _rms(y, g_ref[...]).astype(BF16)


def _pos_dft(cos_m, sin_m, ab, g_four, *, layer, tm):
    seq = cos_m.shape[0]
    t = ab.shape[0]
    per_seq = seq // tm
    return pl.pallas_call(
        _pos_dft_kernel,
        grid=(t // seq, per_seq),
        in_specs=[
            pl.BlockSpec((tm, seq), lambda b, j: (j, 0)),
            pl.BlockSpec((tm, seq), lambda b, j: (j, 0)),
            pl.BlockSpec((seq, FOURIER_WIDTH), lambda b, j: (b, 0)),
            pl.BlockSpec((seq, FOURIER_WIDTH), lambda b, j: (b, 1)),
            _layer_spec((1, FOURIER_WIDTH), layer),
        ],
        out_specs=pl.BlockSpec((tm, FOURIER_WIDTH), lambda b, j: (b * per_seq + j, 0)),
        out_shape=jax.ShapeDtypeStruct((t, FOURIER_WIDTH), BF16),
        compiler_params=_params("arbitrary", "arbitrary"),
        name="pos_dft",
    )(cos_m, sin_m, ab, ab, g_four)


FLIP_BLOCK = 256
DFT_PAD_ROWS = 16


def _pos_dft_sym_kernel(alo_ref, ahi_ref, blo_ref, bhi_ref, c_ref, s_ref, q_ref, sgn_ref, g_ref,
                        o_ref, e_ref, d_ref, h_ref):
    half = alo_ref.shape[0]
    blk = q_ref.shape[0]
    nblk = half // blk
    qm = q_ref[...]
    is_row0 = lax.broadcasted_iota(jnp.int32, (blk, 1), 0) == 0

    def flipped_block(src_ref, bt, first_row):
        lo = (nblk - 1 - bt) * blk
        y = jnp.dot(qm, src_ref[lo:lo + blk, :], preferred_element_type=F32)
        if bt > 0:
            first_row = src_ref[lo + blk:lo + blk + 1, :].astype(F32)
        return jnp.where(is_row0, first_row, y)

    for bt in range(nblk):
        r = slice(bt * blk, (bt + 1) * blk)
        e_ref[r, :] = (alo_ref[r, :].astype(F32) + flipped_block(ahi_ref, bt, 0.0)).astype(BF16)
        d_ref[r, :] = (blo_ref[r, :].astype(F32) - flipped_block(bhi_ref, bt, 0.0)).astype(BF16)

    a_mid = ahi_ref[0:1, :].astype(F32)
    g = g_ref[...]
    mid_row = None
    for kb in range(nblk):
        r = slice(kb * blk, (kb + 1) * blk)
        rows = blk + (DFT_PAD_ROWS if kb == nblk - 1 else 0)
        cr = slice(kb * blk, kb * blk + rows)
        yc = jnp.dot(c_ref[cr, :], e_ref[...], preferred_element_type=F32) + sgn_ref[cr, :] * a_mid
        ys = jnp.dot(s_ref[r, :], d_ref[...], preferred_element_type=F32)
        o_ref[r, :] = _rms(yc[:blk] - ys, g).astype(BF16)
        h_ref[r, :] = _rms(yc[:blk] + ys, g).astype(BF16)
        if kb == nblk - 1:
            mid_row = _rms(yc[blk:blk + 1], g)

    for bs in range(nblk):
        r = slice(half + bs * blk, half + (bs + 1) * blk)
        o_ref[r, :] = flipped_block(h_ref, bs, mid_row).astype(BF16)


def _pos_dft_sym(tables, ab, g_four, *, layer):
    cos_m, sin_m, flip_m, sgn = tables
    half = sin_m.shape[0]
    t = ab.shape[0]
    blocks = lambda row, col: pl.BlockSpec((half, FOURIER_WIDTH), lambda b: (2 * b + row, col))
    return pl.pallas_call(
        _pos_dft_sym_kernel,
        grid=(t // (2 * half),),
        in_specs=[
            blocks(0, 0), blocks(1, 0), blocks(0, 1), blocks(1, 1),
            _const_spec(cos_m.shape),
            _const_spec(sin_m.shape),
            _const_spec(flip_m.shape),
            _const_spec(sgn.shape),
            _layer_spec((1, FOURIER_WIDTH), layer),
        ],
        out_specs=pl.BlockSpec((2 * half, FOURIER_WIDTH), lambda b: (b, 0)),
        out_shape=jax.ShapeDtypeStruct((t, FOURIER_WIDTH), BF16),
        scratch_shapes=[pltpu.VMEM((half, FOURIER_WIDTH), BF16)] * 3,
        compiler_params=_params("arbitrary"),
        name="pos_dft_sym",
    )(ab, ab, ab, ab, cos_m, sin_m, flip_m, sgn, g_four)


def _out_proj_kernel(a_ref, f_ref, w_ref, x_ref, mod_ref, gpost_ref, gpre_ref, xo_ref, h_ref, *, n_sub):
    mod = mod_ref[0]
    gate_g = mod[2:3] * gpost_ref[...]
    scale_g = (1.0 + mod[4:5]) * gpre_ref[...]
    rows = x_ref.shape[0] // n_sub
    def matmuls(s):
        r = slice(s * rows, (s + 1) * rows)
        return (jnp.dot(a_ref[r, :], w_ref[:ATTN_WIDTH, :], preferred_element_type=F32)
                + jnp.dot(f_ref[r, :], w_ref[ATTN_WIDTH:, :], preferred_element_type=F32))

    pending = matmuls(0)
    for s in range(n_sub):
        r = slice(s * rows, (s + 1) * rows)
        mix = pending
        if s + 1 < n_sub:
            pending = matmuls(s + 1)
        xn = x_ref[r, :] + _rms(mix, gate_g)
        xo_ref[r, :] = xn
        h_ref[r, :] = (_rms(xn, scale_g) + mod[3:4]).astype(BF16)


def _out_proj(a, f, w_out, x2d, mod_all, g_post, g_pre_ffn, *, layer, mod_row, tm, n_sub=4):
    t, d = x2d.shape
    return pl.pallas_call(
        functools.partial(_out_proj_kernel, n_sub=n_sub),
        grid=(t // tm,),
        in_specs=[
            pl.BlockSpec((tm, ATTN_WIDTH), lambda i: (i, 0)),
            pl.BlockSpec((tm, FOURIER_WIDTH), lambda i: (i, 0)),
            _const_spec((ATTN_WIDTH + FOURIER_WIDTH, d)),
            pl.BlockSpec((tm, d), lambda i: (i, 0)),
            _mod_spec(d, layer, mod_row),
            _layer_spec((1, d), layer),
            _layer_spec((1, d), layer),
        ],
        out_specs=[
            pl.BlockSpec((tm, d), lambda i: (i, 0)),
            pl.BlockSpec((tm, d), lambda i: (i, 0)),
        ],
        out_shape=[
            jax.ShapeDtypeStruct((t, d), F32),
            jax.ShapeDtypeStruct((t, d), BF16),
        ],
        compiler_params=_params("arbitrary"),
        name="out_proj",
    )(a, f, w_out, x2d, mod_all, g_post, g_pre_ffn)


CONV_HALO = 8


def _ffn_up_kernel(h_ref, wg_ref, wv_ref, cw_ref, cb_ref, o_ref, *, seg_len, n_sub):
    rows, tn = o_ref.shape
    chunk = rows // n_sub
    pos = lax.broadcasted_iota(jnp.int32, (rows, 1), 0) % seg_len
    first, final = pos == 0, pos == seg_len - 1
    cw = cw_ref[...]
    cb = cb_ref[...]

    def span(s):
        return max(s * chunk - CONV_HALO, 0), min((s + 1) * chunk + CONV_HALO, rows)

    def matmuls(s):
        lo, hi = span(s)
        h = h_ref[lo:hi, :]
        return (jnp.dot(h, wg_ref[...], preferred_element_type=F32),
                jnp.dot(h, wv_ref[...], preferred_element_type=F32))

    def epilogue(s, gate, val):
        lo, hi = span(s)
        prev = jnp.where(first[lo:hi], 0.0, pltpu.roll(gate, 1, axis=0))
        nxt = jnp.where(final[lo:hi], 0.0, pltpu.roll(gate, hi - lo - 1, axis=0))
        c = prev * cw[0:1] + gate * cw[1:2] + nxt * cw[2:3] + cb
        act = 0.5 * c * (1.0 + jnp.tanh(math.sqrt(2.0 / math.pi) * (c + 0.044715 * (c * c * c))))
        own = slice(s * chunk - lo, (s + 1) * chunk - lo)
        o_ref[s * chunk:(s + 1) * chunk, :] = (act * val)[own].astype(BF16)

    pending = matmuls(0)
    for s in range(n_sub):
        ready = pending
        if s + 1 < n_sub:
            pending = matmuls(s + 1)
        epilogue(s, *ready)


def _ffn_up(h, w_up, conv_w, conv_b, *, layer, seg_len, tm, tn, n_sub=2):
    t, d = h.shape
    ffn = conv_w.shape[-1]
    n_col = ffn // tn
    return pl.pallas_call(
        functools.partial(_ffn_up_kernel, seg_len=seg_len, n_sub=n_sub),
        grid=(t // tm, n_col),
        in_specs=[
            pl.BlockSpec((tm, d), lambda i, j: (i, 0)),
            pl.BlockSpec((d, tn), lambda i, j: (0, j)),
            pl.BlockSpec((d, tn), lambda i, j: (0, j + n_col)),
            pl.BlockSpec((None, 3, tn), lambda i, j: (layer, 0, j)),
            pl.BlockSpec((None, 1, tn), lambda i, j: (layer, 0, j)),
        ],
        out_specs=pl.BlockSpec((tm, tn), lambda i, j: (i, j)),
        out_shape=jax.ShapeDtypeStruct((t, ffn), BF16),
        compiler_params=_params("arbitrary", "arbitrary"),
        name="ffn_up",
    )(h, w_up, w_up, conv_w, conv_b)


def _ffn_down_kernel(u_ref, w_ref, x_ref, mod_ref, gpost_ref, xo_ref, y_ref):
    j = pl.program_id(1)
    n_col, _, tn = y_ref.shape
    y_ref[j] = jnp.dot(u_ref[...], w_ref[...], preferred_element_type=F32)

    @pl.when(j == n_col - 1)
    def _():
        d = n_col * tn
        ssq = functools.reduce(
            jnp.add, [jnp.sum(y_ref[c] * y_ref[c], axis=-1, keepdims=True) for c in range(n_col)])
        inv = lax.rsqrt(ssq / d + EPS)
        gate = mod_ref[0][5:6]
        for c in range(n_col):
            cols = slice(c * tn, (c + 1) * tn)
            xo_ref[:, cols] = x_ref[:, cols] + gate[:, cols] * (y_ref[c] * inv * gpost_ref[:, cols])


def _ffn_down(u, w_down, x2d, mod_all, g_post, *, layer, mod_row, tm, tn):
    t, d = x2d.shape
    ffn = u.shape[1]
    n_col = d // tn
    return pl.pallas_call(
        _ffn_down_kernel,
        grid=(t // tm, n_col),
        in_specs=[
            pl.BlockSpec((tm, ffn), lambda i, j: (i, 0)),
            pl.BlockSpec((ffn, tn), lambda i, j: (0, j)),
            pl.BlockSpec((tm, d), lambda i, j: (i, 0)),
            _mod_spec(d, layer, mod_row),
            _layer_spec((1, d), layer),
        ],
        out_specs=pl.BlockSpec((tm, d), lambda i, j: (i, 0)),
        out_shape=jax.ShapeDtypeStruct((t, d), F32),
        scratch_shapes=[pltpu.VMEM((n_col, tm, tn), F32)],
        compiler_params=_params("arbitrary", "arbitrary"),
        name="ffn_down",
    )(u, w_down, x2d, mod_all, g_post)


def _ffn_down_skew_kernel(u_ref, w_ref, x_ref, mod_ref, gpost_ref, xo_ref, y_ref, inv_ref):
    i, j = pl.program_id(0), pl.program_id(1)
    n_col, _, tn = y_ref.shape

    @pl.when((i == 0) & (j == 0))
    def _():
        y_ref[...] = jnp.zeros_like(y_ref)
        inv_ref[...] = jnp.zeros_like(inv_ref)

    gate_g = mod_ref[0][N_MOD - 1:N_MOD] * gpost_ref[...]
    xo_ref[...] = x_ref[...] + (y_ref[j] * inv_ref[...]) * gate_g
    y_ref[j] = jnp.dot(u_ref[...], w_ref[...], preferred_element_type=F32)

    @pl.when(j == n_col - 1)
    def _():
        ssq = functools.reduce(
            jnp.add, [jnp.sum(y_ref[c] * y_ref[c], axis=-1, keepdims=True) for c in range(n_col)])
        inv_ref[...] = lax.rsqrt(ssq / (n_col * tn) + EPS)


def _ffn_down_skew(u, w_down, x2d, mod_all, g_post, *, layer, mod_row, tm, tn):
    t, d = x2d.shape
    ffn = u.shape[1]
    n_tiles = t // tm
    prev = lambda i: jnp.maximum(i - 1, 0)
    col = lambda i, j: jnp.where(i > 0, j, 0)
    return pl.pallas_call(
        _ffn_down_skew_kernel,
        grid=(n_tiles + 1, d // tn),
        in_specs=[
            pl.BlockSpec((tm, ffn), lambda i, j: (jnp.minimum(i, n_tiles - 1), 0)),
            pl.BlockSpec((ffn, tn), lambda i, j: (0, j)),
            pl.BlockSpec((tm, tn), lambda i, j: (prev(i), col(i, j))),
            pl.BlockSpec((None, 1, N_MOD, tn), lambda i, j: (layer, mod_row(prev(i)), 0, col(i, j))),
            pl.BlockSpec((None, 1, tn), lambda i, j: (layer, 0, col(i, j))),
        ],
        out_specs=pl.BlockSpec((tm, tn), lambda i, j: (prev(i), col(i, j))),
        out_shape=jax.ShapeDtypeStruct((t, d), F32),
        scratch_shapes=[pltpu.VMEM((d // tn, tm, tn), F32), pltpu.VMEM((tm, 1), F32)],
        compiler_params=_params("arbitrary", "arbitrary"),
        name="ffn_down_skew",
    )(u, w_down, x2d, mod_all, g_post)


def _rope_tables(n):
    pos = np.arange(n)
    n_pairs_axis = HEAD_DIM // 4
    freqs = ROPE_THETA ** (-np.arange(n_pairs_axis, dtype=np.float64) / n_pairs_axis)
    ang = np.concatenate([(pos // GRID_W)[:, None] * freqs, (pos % GRID_W)[:, None] * freqs], axis=-1)
    c, s = np.cos(ang), np.sin(ang)
    return (np.concatenate([c, c], axis=-1).astype(np.float32),
            np.concatenate([-s, s], axis=-1).astype(np.float32))


def _dft_matrices(n):
    idx = np.arange(n)
    ang = ((idx[:, None] * idx[None, :]) % n) * (2.0 * math.pi / n)
    return np.cos(ang) * n ** -0.5, np.sin(ang) * n ** -0.5


def _sym_dft_tables(n):
    half = n // 2
    k = np.arange(half + DFT_PAD_ROWS)[:, None]
    ang = ((k * np.arange(half)[None, :]) % n) * (2.0 * math.pi / n)
    scale = n ** -0.5
    valid = k <= half
    cos_m = np.where(valid, np.cos(ang) * scale, 0.0).astype(BF16)
    sin_m = (np.sin(ang[:half]) * scale).astype(BF16)
    sgn = np.where(valid, (1 - 2 * (k % 2)) * scale, 0.0).astype(np.float32)
    i = np.arange(FLIP_BLOCK)
    flip_m = (i[:, None] + i[None, :] == FLIP_BLOCK).astype(BF16)
    return cos_m, sin_m, flip_m, sgn


def _deinterleave(v):
    return jnp.concatenate([v[..., 0::2], v[..., 1::2]], axis=-1)


def kernel(x, c, ctx, c_ctx, w_mod, b_mod, g_pre_mix, g_post_mix, g_pre_ffn, g_post_ffn,
           w_in, q_norm, k_norm, w_four, g_attn_out, g_four_out, w_out,
           w_up, conv_w, conv_b, w_down):
    b, n, d = x.shape
    ctx_len = ctx.shape[1]
    depth = w_mod.shape[0]

    perm_matrix = _deinterleave(np.eye(HEAD_DIM, dtype=BF16))
    w_in_b = _prep_w_in(w_in, perm_matrix)
    qn = _deinterleave(q_norm).reshape(depth, 1, HEAD_DIM)
    kn = _deinterleave(k_norm).reshape(depth, 1, HEAD_DIM)
    cos_t, sin_t = _rope_tables(n)
    dft_tables = _sym_dft_tables(n)
    cos_c, sin_c = (m.astype(BF16) for m in _dft_matrices(ctx_len))
    cs_ch = np.stack(_dft_matrices(FOURIER_GROUP)).astype(np.float32)
    m_fold = _fold_channel_dft(cs_ch, w_four)
    rows3 = lambda v: v.reshape(depth, 1, -1)
    g_pre_mix, g_post_mix, g_pre_ffn, g_post_ffn, g_attn_out, g_four_out, conv_b = map(
        rows3, (g_pre_mix, g_post_mix, g_pre_ffn, g_post_ffn, g_attn_out, g_four_out, conv_b))

    cc = jnp.zeros((MOD_ROWS, d), F32).at[:b].set(c).at[b].set(c_ctx)
    mod_all = _mod_proj(cc, w_mod, b_mod).reshape(depth, MOD_ROWS, N_MOD, d)

    xl = x.reshape(b * n, d)
    xc = ctx.reshape(b * ctx_len, d)
    tm = 512
    lat_row = lambda i: i // (n // tm)
    ctx_row = lambda i: b

    for i in range(depth):
        last = i == depth - 1
        q_l, k_l, v_l, ab_l, w_up_b = _in_proj(
            xl, mod_all, g_pre_mix, w_in_b, qn, kn, cos_t, sin_t, m_fold,
            layer=i, mod_row=lat_row, seq=n, rope=True, tm=tm, cast=(w_up,))
        if last:
            k_c, v_c = _ctx_kv(xc, mod_all, g_pre_mix, w_in_b, kn, layer=i, mod_row=ctx_row, tm=tm)
        else:
            q_c, k_c, v_c, ab_c = _in_proj(
                xc, mod_all, g_pre_mix, w_in_b, qn, kn, cos_t, sin_t, m_fold,
                layer=i, mod_row=ctx_row, seq=ctx_len, rope=False, tm=ctx_len)

        attn_l, w_out_b, w_down_b = _attention(
            q_l, [(k_l, v_l, n), (k_c, v_c, ctx_len)], g_attn_out,
            layer=i, q_seq=n, tq=512, cast=(w_out, w_down), transposed=True)
        four_l = _pos_dft_sym(dft_tables, ab_l, g_four_out, layer=i)
        xl, h_l = _out_proj(attn_l, four_l, w_out_b, xl, mod_all, g_post_mix, g_pre_ffn,
                            layer=i, mod_row=lat_row, tm=tm)
        u_l = _ffn_up(h_l, w_up_b, conv_w, conv_b, layer=i, seg_len=n, tm=n, tn=512)
        xl = _ffn_down_skew(u_l, w_down_b, xl, mod_all, g_post_ffn, layer=i,
                            mod_row=lambda r: r // (n // 1024), tm=1024, tn=512)

        if not last:
            attn_c, = _attention(q_c, [(k_c, v_c, ctx_len)], g_attn_out,
                                 layer=i, q_seq=ctx_len, tq=ctx_len)
            four_c = _pos_dft(cos_c, sin_c, ab_c, g_four_out, layer=i, tm=ctx_len)
            xc, h_c = _out_proj(attn_c, four_c, w_out_b, xc, mod_all, g_post_mix, g_pre_ffn,
                                layer=i, mod_row=ctx_row, tm=tm)
            u_c = _ffn_up(h_c, w_up_b, conv_w, conv_b, layer=i, seg_len=ctx_len,
                          tm=b * ctx_len, tn=512)
            xc = _ffn_down(u_c, w_down_b, xc, mod_all, g_post_ffn, layer=i, mod_row=ctx_row,
                           tm=tm, tn=512)

    return xl.reshape(b, n, d)
```

```python
import functools
import math

import jax
import jax.numpy as jnp
import numpy as np
from jax import lax
from jax.experimental import pallas as pl
from jax.experimental.pallas import tpu as pltpu

GRID_W = 64
HEAD_DIM = 128
N_Q_HEADS = 8
N_KV_HEADS = 2
Q_PER_KV = N_Q_HEADS // N_KV_HEADS
FOURIER_GROUP = 128
N_FOURIER_GROUPS = 8
ATTN_WIDTH = N_Q_HEADS * HEAD_DIM
KV_WIDTH = N_KV_HEADS * HEAD_DIM
FOURIER_WIDTH = N_FOURIER_GROUPS * FOURIER_GROUP
Q_END = ATTN_WIDTH
K_END = Q_END + KV_WIDTH
V_END = K_END + KV_WIDTH
IN_WIDTH = V_END + FOURIER_WIDTH
ROPE_THETA = 10000.0
EPS = 1e-6
N_MOD = 6
Q_SCALE = HEAD_DIM ** -0.5 * math.log2(math.e)
MOD_ROWS = 16

VMEM_LIMIT_BYTES = 56 * 1024 * 1024

F32 = jnp.float32
BF16 = jnp.bfloat16


def _params(*semantics):
    return pltpu.CompilerParams(dimension_semantics=semantics, vmem_limit_bytes=VMEM_LIMIT_BYTES)


def _rms(x, g):
    ms = jnp.mean(x * x, axis=-1, keepdims=True)
    return x * lax.rsqrt(ms + EPS) * g


def _const_spec(shape):
    return pl.BlockSpec(shape, lambda *_: (0,) * len(shape), pipeline_mode=pl.Buffered(1))


def _layer_spec(shape, layer):
    zeros = (0,) * len(shape)
    return pl.BlockSpec((None,) + shape, lambda *_: (layer,) + zeros, pipeline_mode=pl.Buffered(1))


def _side_casts(sources, layer, n_steps, step_of):
    in_specs, out_specs, out_shapes = [], [], []
    for src in sources:
        _, r, c = src.shape
        rows = r // n_steps
        in_specs.append(pl.BlockSpec((None, rows, c), lambda *g: (layer, step_of(*g), 0)))
        out_specs.append(pl.BlockSpec((rows, c), lambda *g: (step_of(*g), 0)))
        out_shapes.append(jax.ShapeDtypeStruct((r, c), BF16))
    return in_specs, out_specs, out_shapes


def _run_side_casts(src_refs, dst_refs):
    for src, dst in zip(src_refs, dst_refs, strict=True):
        dst[...] = src[...].astype(BF16)


def _mod_spec(d, layer, row_of):
    return pl.BlockSpec((None, 1, N_MOD, d), lambda i, *_: (layer, row_of(i), 0, 0))


def _mod_kernel(c_ref, w_ref, b_ref, o_ref):
    c = c_ref[...]
    s = (c * jax.nn.sigmoid(c)).astype(BF16)
    o_ref[0] = jnp.dot(s, w_ref[0].astype(BF16), preferred_element_type=F32) + b_ref[0]


def _mod_proj(cc, w_mod, b_mod, tn=1024):
    depth, d, width = w_mod.shape
    return pl.pallas_call(
        _mod_kernel,
        grid=(depth, width // tn),
        in_specs=[
            pl.BlockSpec((MOD_ROWS, d), lambda l, j: (0, 0)),
            pl.BlockSpec((1, d, tn), lambda l, j: (l, 0, j)),
            pl.BlockSpec((1, 1, tn), lambda l, j: (l, 0, j)),
        ],
        out_specs=pl.BlockSpec((1, MOD_ROWS, tn), lambda l, j: (l, 0, j)),
        out_shape=jax.ShapeDtypeStruct((depth, MOD_ROWS, width), F32),
        compiler_params=_params("arbitrary", "arbitrary"),
        name="mod_proj",
    )(cc, w_mod, b_mod.reshape(depth, 1, width))


def _fold_kernel(cs_ref, w_ref, o_ref):
    w = w_ref[0, 0]
    o_ref[0, 0, :, :FOURIER_GROUP] = jnp.dot(
        cs_ref[0], w, preferred_element_type=F32, precision=lax.Precision.HIGHEST).astype(BF16)
    o_ref[0, 0, :, FOURIER_GROUP:] = jnp.dot(
        cs_ref[1], w, preferred_element_type=F32, precision=lax.Precision.HIGHEST).astype(BF16)


def _fold_channel_dft(cs, w_four):
    depth, groups, c, _ = w_four.shape
    return pl.pallas_call(
        _fold_kernel,
        grid=(depth, groups),
        in_specs=[
            pl.BlockSpec((2, c, c), lambda l, g: (0, 0, 0)),
            pl.BlockSpec((1, 1, c, c), lambda l, g: (l, g, 0, 0)),
        ],
        out_specs=pl.BlockSpec((1, 1, c, 2 * c), lambda l, g: (l, g, 0, 0)),
        out_shape=jax.ShapeDtypeStruct((depth, groups, c, 2 * c), BF16),
        compiler_params=_params("arbitrary", "arbitrary"),
        name="fold_channel_dft",
    )(cs, w_four)


def _prep_w_in_kernel(w_ref, p_ref, o_ref):
    j = pl.program_id(1)
    w = w_ref[0].astype(BF16)

    @pl.when(j < N_Q_HEADS + N_KV_HEADS)
    def _():
        o_ref[0] = jnp.dot(w, p_ref[...], preferred_element_type=F32).astype(BF16)

    @pl.when(j >= N_Q_HEADS + N_KV_HEADS)
    def _():
        o_ref[0] = w


def _prep_w_in(w_in, perm_matrix):
    depth, d, width = w_in.shape
    return pl.pallas_call(
        _prep_w_in_kernel,
        grid=(depth, width // HEAD_DIM),
        in_specs=[
            pl.BlockSpec((1, d, HEAD_DIM), lambda l, j: (l, 0, j)),
            pl.BlockSpec((HEAD_DIM, HEAD_DIM), lambda l, j: (0, 0)),
        ],
        out_specs=pl.BlockSpec((1, d, HEAD_DIM), lambda l, j: (l, 0, j)),
        out_shape=jax.ShapeDtypeStruct((depth, d, width), BF16),
        compiler_params=_params("arbitrary", "arbitrary"),
        name="prep_w_in",
    )(w_in, perm_matrix)


def _in_proj_kernel(x_ref, mod_ref, gpre_ref, w_ref, qn_ref, kn_ref, cos_ref, sin_ref, m_ref, *rest,
                    rope, n_sub, n_cast):
    cast_src, (q_ref, k_ref, v_ref, ab_ref), cast_dst = rest[:n_cast], rest[n_cast:n_cast + 4], rest[n_cast + 4:]
    _run_side_casts(cast_src, cast_dst)
    mod = mod_ref[0]
    scale_g = (1.0 + mod[1:2]) * gpre_ref[...]
    rows = x_ref.shape[0] // n_sub
    for s in range(n_sub):
        r = slice(s * rows, (s + 1) * rows)
        h = _rms(x_ref[r, :], scale_g) + mod[0:1]
        p = jnp.dot(h.astype(BF16), w_ref[...], preferred_element_type=F32)

        def head(col, gain, post_scale):
            y = _rms(p[:, col:col + HEAD_DIM], gain)
            if rope:
                y = y * cos_ref[r, :] + pltpu.roll(y, HEAD_DIM // 2, axis=1) * sin_ref[r, :]
            if post_scale != 1.0:
                y = y * post_scale
            return y.astype(BF16)

        for i in range(N_Q_HEADS):
            q_ref[r, i * HEAD_DIM:(i + 1) * HEAD_DIM] = head(i * HEAD_DIM, qn_ref[...], Q_SCALE)
        for i in range(N_KV_HEADS):
            k_ref[r, i * HEAD_DIM:(i + 1) * HEAD_DIM] = head(Q_END + i * HEAD_DIM, kn_ref[...], 1.0)
        v_ref[r, :] = p[:, K_END:V_END].astype(BF16)
        for g in range(N_FOURIER_GROUPS):
            lo = V_END + g * FOURIER_GROUP
            ab = jnp.dot(p[:, lo:lo + FOURIER_GROUP].astype(BF16), m_ref[g], preferred_element_type=F32)
            ab_ref[r, g * FOURIER_GROUP:(g + 1) * FOURIER_GROUP] = ab[:, :FOURIER_GROUP].astype(BF16)
            ab_ref[r, FOURIER_WIDTH + g * FOURIER_GROUP:FOURIER_WIDTH + (g + 1) * FOURIER_GROUP] = (
                ab[:, FOURIER_GROUP:].astype(BF16))


def _in_proj(x2d, mod_all, g_pre, w_in, qn, kn, cos_t, sin_t, m_fold, *, layer, mod_row, seq, rope, tm,
             n_sub=1, cast=()):
    t, d = x2d.shape
    per_seq = seq // tm
    rope_map = (lambda i: (i % per_seq, 0)) if rope else (lambda i: (0, 0))
    cast_in, cast_out, cast_shapes = _side_casts(cast, layer, t // tm, lambda i: i)
    return pl.pallas_call(
        functools.partial(_in_proj_kernel, rope=rope, n_sub=n_sub, n_cast=len(cast)),
        grid=(t // tm,),
        in_specs=[
            pl.BlockSpec((tm, d), lambda i: (i, 0)),
            _mod_spec(d, layer, mod_row),
            _layer_spec((1, d), layer),
            _layer_spec((d, IN_WIDTH), layer),
            _layer_spec((1, HEAD_DIM), layer),
            _layer_spec((1, HEAD_DIM), layer),
            pl.BlockSpec((tm, HEAD_DIM), rope_map),
            pl.BlockSpec((tm, HEAD_DIM), rope_map),
            _layer_spec((N_FOURIER_GROUPS, FOURIER_GROUP, 2 * FOURIER_GROUP), layer),
        ] + cast_in,
        out_specs=[
            pl.BlockSpec((tm, ATTN_WIDTH), lambda i: (i, 0)),
            pl.BlockSpec((tm, KV_WIDTH), lambda i: (i, 0)),
            pl.BlockSpec((tm, KV_WIDTH), lambda i: (i, 0)),
            pl.BlockSpec((tm, 2 * FOURIER_WIDTH), lambda i: (i, 0)),
        ] + cast_out,
        out_shape=[
            jax.ShapeDtypeStruct((t, ATTN_WIDTH), BF16),
            jax.ShapeDtypeStruct((t, KV_WIDTH), BF16),
            jax.ShapeDtypeStruct((t, KV_WIDTH), BF16),
            jax.ShapeDtypeStruct((t, 2 * FOURIER_WIDTH), BF16),
        ] + cast_shapes,
        compiler_params=_params("arbitrary"),
        name="in_proj",
    )(x2d, mod_all, g_pre, w_in, qn, kn, cos_t, sin_t, m_fold, *cast)


def _ctx_kv_kernel(x_ref, mod_ref, gpre_ref, w_ref, kn_ref, k_ref, v_ref):
    mod = mod_ref[0]
    h = _rms(x_ref[...], (1.0 + mod[1:2]) * gpre_ref[...]) + mod[0:1]
    p = jnp.dot(h.astype(BF16), w_ref[...], preferred_element_type=F32)
    for i in range(N_KV_HEADS):
        cols = slice(i * HEAD_DIM, (i + 1) * HEAD_DIM)
        k_ref[:, cols] = _rms(p[:, cols], kn_ref[...]).astype(BF16)
    v_ref[...] = p[:, KV_WIDTH:].astype(BF16)


def _ctx_kv(x2d, mod_all, g_pre, w_in, kn, *, layer, mod_row, tm):
    t, d = x2d.shape
    assert Q_END % (2 * KV_WIDTH) == 0
    return pl.pallas_call(
        _ctx_kv_kernel,
        grid=(t // tm,),
        in_specs=[
            pl.BlockSpec((tm, d), lambda i: (i, 0)),
            _mod_spec(d, layer, mod_row),
            _layer_spec((1, d), layer),
            pl.BlockSpec((None, d, 2 * KV_WIDTH), lambda i: (layer, 0, Q_END // (2 * KV_WIDTH)),
                         pipeline_mode=pl.Buffered(1)),
            _layer_spec((1, HEAD_DIM), layer),
        ],
        out_specs=[pl.BlockSpec((tm, KV_WIDTH), lambda i: (i, 0))] * 2,
        out_shape=[jax.ShapeDtypeStruct((t, KV_WIDTH), BF16)] * 2,
        compiler_params=_params("arbitrary"),
        name="ctx_kv",
    )(x2d, mod_all, g_pre, w_in, kn)


def _attn_kernel(*refs, n_seg, n_cast, key_chunk):
    q_ref = refs[0]
    kv_refs = refs[1:1 + 2 * n_seg]
    g_ref = refs[1 + 2 * n_seg]
    rest = refs[2 + 2 * n_seg:]
    cast_src, o_ref, cast_dst, acc_ref = rest[:n_cast], rest[n_cast], rest[n_cast + 1:-1], rest[-1]
    _run_side_casts(cast_src, cast_dst)

    chunks = []
    for i in range(n_seg):
        total = kv_refs[2 * i].shape[0]
        for lo in range(0, total, key_chunk):
            chunks.append((kv_refs[2 * i], kv_refs[2 * i + 1], lo, min(key_chunk, total - lo)))

    def chunk_scores(h, c):
        k_ref, _, lo, size = chunks[c]
        kv = (h // Q_PER_KV) * HEAD_DIM
        return lax.dot_general(k_ref[lo:lo + size, kv:kv + HEAD_DIM], q_ref[:, h * HEAD_DIM:(h + 1) * HEAD_DIM],
                               (((1,), (1,)), ((), ())), preferred_element_type=F32)

    def chunk_output(h, c, p):
        _, v_ref, lo, size = chunks[c]
        kv = (h // Q_PER_KV) * HEAD_DIM
        return lax.dot_general(v_ref[lo:lo + size, kv:kv + HEAD_DIM], p,
                               (((0,), (0,)), ((), ())), preferred_element_type=F32)

    def col_max(m, s):
        cm = jnp.max(s, axis=0, keepdims=True)
        return cm if m is None else jnp.maximum(m, cm)

    scores, m = [], None
    for c in range(len(chunks)):
        scores.append(chunk_scores(0, c))
        m = col_max(m, scores[-1])
    for h in range(N_Q_HEADS):
        next_scores, next_m = [], None
        denom, out_t = None, None
        for c in range(len(chunks)):
            if h + 1 < N_Q_HEADS:
                next_scores.append(chunk_scores(h + 1, c))
                next_m = col_max(next_m, next_scores[-1])
            p = jnp.exp2(scores[c] - m)
            ps = jnp.sum(p, axis=0, keepdims=True)
            po = chunk_output(h, c, p.astype(BF16))
            denom = ps if denom is None else denom + ps
            out_t = po if out_t is None else out_t + po
        acc_ref[:, h * HEAD_DIM:(h + 1) * HEAD_DIM] = (out_t / denom).T
        scores, m = next_scores, next_m
    o_ref[...] = _rms(acc_ref[...], g_ref[...]).astype(BF16)


def _attention(q, segments, g_attn, *, layer, q_seq, tq, cast=(), key_chunk=1024):
    t = q.shape[0]
    per_seq = q_seq // tq
    in_specs = [pl.BlockSpec((tq, ATTN_WIDTH), lambda b, j: (b * per_seq + j, 0))]
    args = [q]
    for k, v, rows in segments:
        in_specs += [pl.BlockSpec((rows, KV_WIDTH), lambda b, j: (b, 0))] * 2
        args += [k, v]
    in_specs.append(_layer_spec((1, ATTN_WIDTH), layer))
    args.append(g_attn)
    cast_in, cast_out, cast_shapes = _side_casts(cast, layer, t // tq, lambda b, j: b * per_seq + j)
    return pl.pallas_call(
        functools.partial(_attn_kernel, n_seg=len(segments), n_cast=len(cast), key_chunk=key_chunk),
        grid=(t // q_seq, per_seq),
        in_specs=in_specs + cast_in,
        out_specs=[pl.BlockSpec((tq, ATTN_WIDTH), lambda b, j: (b * per_seq + j, 0))] + cast_out,
        out_shape=[jax.ShapeDtypeStruct((t, ATTN_WIDTH), BF16)] + cast_shapes,
        scratch_shapes=[pltpu.VMEM((tq, ATTN_WIDTH), F32)],
        compiler_params=_params("arbitrary", "arbitrary"),
        name="attention",
    )(*args, *cast)


def _pos_dft_kernel(c_ref, s_ref, a_ref, b_ref, g_ref, o_ref):
    y = (jnp.dot(c_ref[...], a_ref[...], preferred_element_type=F32)
         - jnp.dot(s_ref[...], b_ref[...], preferred_element_type=F32))
    o_ref[...] = _rms(y, g_ref[...]).astype(BF16)


def _pos_dft(cos_m, sin_m, ab, g_four, *, layer, tm):
    seq = cos_m.shape[0]
    t = ab.shape[0]
    per_seq = seq // tm
    return pl.pallas_call(
        _pos_dft_kernel,
        grid=(t // seq, per_seq),
        in_specs=[
            pl.BlockSpec((tm, seq), lambda b, j: (j, 0)),
            pl.BlockSpec((tm, seq), lambda b, j: (j, 0)),
            pl.BlockSpec((seq, FOURIER_WIDTH), lambda b, j: (b, 0)),
            pl.BlockSpec((seq, FOURIER_WIDTH), lambda b, j: (b, 1)),
            _layer_spec((1, FOURIER_WIDTH), layer),
        ],
        out_specs=pl.BlockSpec((tm, FOURIER_WIDTH), lambda b, j: (b * per_seq + j, 0)),
        out_shape=jax.ShapeDtypeStruct((t, FOURIER_WIDTH), BF16),
        compiler_params=_params("arbitrary", "arbitrary"),
        name="pos_dft",
    )(cos_m, sin_m, ab, ab, g_four)


FLIP_BLOCK = 256
DFT_PAD_ROWS = 16


def _pos_dft_sym_kernel(alo_ref, ahi_ref, blo_ref, bhi_ref, c_ref, s_ref, q_ref, sgn_ref, g_ref,
                        o_ref, e_ref, d_ref, h_ref):
    half = alo_ref.shape[0]
    blk = q_ref.shape[0]
    nblk = half // blk
    qm = q_ref[...]
    is_row0 = lax.broadcasted_iota(jnp.int32, (blk, 1), 0) == 0

    def flipped_block(src_ref, bt, first_row):
        lo = (nblk - 1 - bt) * blk
        y = jnp.dot(qm, src_ref[lo:lo + blk, :], preferred_element_type=F32)
        if bt > 0:
            first_row = src_ref[lo + blk:lo + blk + 1, :].astype(F32)
        return jnp.where(is_row0, first_row, y)

    for bt in range(nblk):
        r = slice(bt * blk, (bt + 1) * blk)
        e_ref[r, :] = (alo_ref[r, :].astype(F32) + flipped_block(ahi_ref, bt, 0.0)).astype(BF16)
        d_ref[r, :] = (blo_ref[r, :].astype(F32) - flipped_block(bhi_ref, bt, 0.0)).astype(BF16)

    a_mid = ahi_ref[0:1, :].astype(F32)
    g = g_ref[...]
    mid_row = None
    for kb in range(nblk):
        r = slice(kb * blk, (kb + 1) * blk)
        rows = blk + (DFT_PAD_ROWS if kb == nblk - 1 else 0)
        cr = slice(kb * blk, kb * blk + rows)
        yc = jnp.dot(c_ref[cr, :], e_ref[...], preferred_element_type=F32) + sgn_ref[cr, :] * a_mid
        ys = jnp.dot(s_ref[r, :], d_ref[...], preferred_element_type=F32)
        o_ref[r, :] = _rms(yc[:blk] - ys, g).astype(BF16)
        h_ref[r, :] = _rms(yc[:blk] + ys, g).astype(BF16)
        if kb == nblk - 1:
            mid_row = _rms(yc[blk:blk + 1], g)

    for bs in range(nblk):
        r = slice(half + bs * blk, half + (bs + 1) * blk)
        o_ref[r, :] = flipped_block(h_ref, bs, mid_row).astype(BF16)


def _pos_dft_sym(tables, ab, g_four, *, layer):
    cos_m, sin_m, flip_m, sgn = tables
    half = sin_m.shape[0]
    t = ab.shape[0]
    blocks = lambda row, col: pl.BlockSpec((half, FOURIER_WIDTH), lambda b: (2 * b + row, col))
    return pl.pallas_call(
        _pos_dft_sym_kernel,
        grid=(t // (2 * half),),
        in_specs=[
            blocks(0, 0), blocks(1, 0), blocks(0, 1), blocks(1, 1),
            _const_spec(cos_m.shape),
            _const_spec(sin_m.shape),
            _const_spec(flip_m.shape),
            _const_spec(sgn.shape),
            _layer_spec((1, FOURIER_WIDTH), layer),
        ],
        out_specs=pl.BlockSpec((2 * half, FOURIER_WIDTH), lambda b: (b, 0)),
        out_shape=jax.ShapeDtypeStruct((t, FOURIER_WIDTH), BF16),
        scratch_shapes=[pltpu.VMEM((half, FOURIER_WIDTH), BF16)] * 3,
        compiler_params=_params("arbitrary"),
        name="pos_dft_sym",
    )(ab, ab, ab, ab, cos_m, sin_m, flip_m, sgn, g_four)


def _out_proj_kernel(a_ref, f_ref, w_ref, x_ref, mod_ref, gpost_ref, gpre_ref, xo_ref, h_ref, *, n_sub):
    mod = mod_ref[0]
    gate_g = mod[2:3] * gpost_ref[...]
    scale_g = (1.0 + mod[4:5]) * gpre_ref[...]
    rows = x_ref.shape[0] // n_sub
    def matmuls(s):
        r = slice(s * rows, (s + 1) * rows)
        return (jnp.dot(a_ref[r, :], w_ref[:ATTN_WIDTH, :], preferred_element_type=F32)
                + jnp.dot(f_ref[r, :], w_ref[ATTN_WIDTH:, :], preferred_element_type=F32))

    pending = matmuls(0)
    for s in range(n_sub):
        r = slice(s * rows, (s + 1) * rows)
        mix = pending
        if s + 1 < n_sub:
            pending = matmuls(s + 1)
        xn = x_ref[r, :] + _rms(mix, gate_g)
        xo_ref[r, :] = xn
        h_ref[r, :] = (_rms(xn, scale_g) + mod[3:4]).astype(BF16)


def _out_proj(a, f, w_out, x2d, mod_all, g_post, g_pre_ffn, *, layer, mod_row, tm, n_sub=4):
    t, d = x2d.shape
    return pl.pallas_call(
        functools.partial(_out_proj_kernel, n_sub=n_sub),
        grid=(t // tm,),
        in_specs=[
            pl.BlockSpec((tm, ATTN_WIDTH), lambda i: (i, 0)),
            pl.BlockSpec((tm, FOURIER_WIDTH), lambda i: (i, 0)),
            _const_spec((ATTN_WIDTH + FOURIER_WIDTH, d)),
            pl.BlockSpec((tm, d), lambda i: (i, 0)),
            _mod_spec(d, layer, mod_row),
            _layer_spec((1, d), layer),
            _layer_spec((1, d), layer),
        ],
        out_specs=[
            pl.BlockSpec((tm, d), lambda i: (i, 0)),
            pl.BlockSpec((tm, d), lambda i: (i, 0)),
        ],
        out_shape=[
            jax.ShapeDtypeStruct((t, d), F32),
            jax.ShapeDtypeStruct((t, d), BF16),
        ],
        compiler_params=_params("arbitrary"),
        name="out_proj",
    )(a, f, w_out, x2d, mod_all, g_post, g_pre_ffn)


CONV_HALO = 8


def _ffn_up_kernel(h_ref, wg_ref, wv_ref, cw_ref, cb_ref, o_ref, *, seg_len, n_sub):
    rows, tn = o_ref.shape
    chunk = rows // n_sub
    pos = lax.broadcasted_iota(jnp.int32, (rows, 1), 0) % seg_len
    first, final = pos == 0, pos == seg_len - 1
    cw = cw_ref[...]
    cb = cb_ref[...]

    def span(s):
        return max(s * chunk - CONV_HALO, 0), min((s + 1) * chunk + CONV_HALO, rows)

    def matmuls(s):
        lo, hi = span(s)
        h = h_ref[lo:hi, :]
        return (jnp.dot(h, wg_ref[...], preferred_element_type=F32),
                jnp.dot(h, wv_ref[...], preferred_element_type=F32))

    def epilogue(s, gate, val):
        lo, hi = span(s)
        prev = jnp.where(first[lo:hi], 0.0, pltpu.roll(gate, 1, axis=0))
        nxt = jnp.where(final[lo:hi], 0.0, pltpu.roll(gate, hi - lo - 1, axis=0))
        c = prev * cw[0:1] + gate * cw[1:2] + nxt * cw[2:3] + cb
        act = 0.5 * c * (1.0 + jnp.tanh(math.sqrt(2.0 / math.pi) * (c + 0.044715 * (c * c * c))))
        own = slice(s * chunk - lo, (s + 1) * chunk - lo)
        o_ref[s * chunk:(s + 1) * chunk, :] = (act * val)[own].astype(BF16)

    pending = matmuls(0)
    for s in range(n_sub):
        ready = pending
        if s + 1 < n_sub:
            pending = matmuls(s + 1)
        epilogue(s, *ready)


def _ffn_up(h, w_up, conv_w, conv_b, *, layer, seg_len, tm, tn, n_sub=2):
    t, d = h.shape
    ffn = conv_w.shape[-1]
    n_col = ffn // tn
    return pl.pallas_call(
        functools.partial(_ffn_up_kernel, seg_len=seg_len, n_sub=n_sub),
        grid=(t // tm, n_col),
        in_specs=[
            pl.BlockSpec((tm, d), lambda i, j: (i, 0)),
            pl.BlockSpec((d, tn), lambda i, j: (0, j)),
            pl.BlockSpec((d, tn), lambda i, j: (0, j + n_col)),
            pl.BlockSpec((None, 3, tn), lambda i, j: (layer, 0, j)),
            pl.BlockSpec((None, 1, tn), lambda i, j: (layer, 0, j)),
        ],
        out_specs=pl.BlockSpec((tm, tn), lambda i, j: (i, j)),
        out_shape=jax.ShapeDtypeStruct((t, ffn), BF16),
        compiler_params=_params("arbitrary", "arbitrary"),
        name="ffn_up",
    )(h, w_up, w_up, conv_w, conv_b)


LANES = 128


def _ffn_down_skew_kernel(u_ref, w_ref, x_ref, mod_ref, gpost_ref, xo_ref, y_ref, inv_ref, ssq_ref):
    i, j = pl.program_id(0), pl.program_id(1)
    n_tiles = pl.num_programs(0) - 1
    n_col, _, tn = y_ref.shape

    @pl.when((i == 0) & (j == 0))
    def _():
        y_ref[...] = jnp.zeros_like(y_ref)
        inv_ref[...] = jnp.zeros_like(inv_ref)

    @pl.when(j == 0)
    def _():
        ssq_ref[...] = jnp.zeros_like(ssq_ref)

    def finish_previous():
        gate_g = mod_ref[0][N_MOD - 1:N_MOD] * gpost_ref[...]
        xo_ref[...] = x_ref[...] + (y_ref[j] * inv_ref[...]) * gate_g

    @pl.when(i < n_tiles)
    def _():
        finish_previous()
        y = jnp.dot(u_ref[...], w_ref[...], preferred_element_type=F32)
        y_ref[j] = y
        sq = y * y
        ssq_ref[...] += functools.reduce(
            jnp.add, [sq[:, c * LANES:(c + 1) * LANES] for c in range(tn // LANES)])

    @pl.when(i == n_tiles)
    def _():
        finish_previous()

    @pl.when((j == n_col - 1) & (i < n_tiles))
    def _():
        ssq = jnp.sum(ssq_ref[...], axis=-1, keepdims=True)
        inv_ref[...] = lax.rsqrt(ssq / (n_col * tn) + EPS)


def _ffn_down_skew(u, w_down, x2d, mod_all, g_post, *, layer, mod_row, tm, tn):
    t, d = x2d.shape
    ffn = u.shape[1]
    n_tiles = t // tm
    n_col = d // tn
    prev = lambda i: jnp.maximum(i - 1, 0)
    col = lambda i, j: jnp.where(i > 0, j, 0)
    return pl.pallas_call(
        _ffn_down_skew_kernel,
        grid=(n_tiles + 1, n_col),
        in_specs=[
            pl.BlockSpec((tm, ffn), lambda i, j: (jnp.minimum(i, n_tiles - 1), 0)),
            pl.BlockSpec((ffn, tn), lambda i, j: (0, jnp.where(i < n_tiles, j, n_col - 1))),
            pl.BlockSpec((tm, tn), lambda i, j: (prev(i), col(i, j))),
            pl.BlockSpec((None, 1, N_MOD, tn), lambda i, j: (layer, mod_row(prev(i)), 0, col(i, j))),
            pl.BlockSpec((None, 1, tn), lambda i, j: (layer, 0, col(i, j))),
        ],
        out_specs=pl.BlockSpec((tm, tn), lambda i, j: (prev(i), col(i, j))),
        out_shape=jax.ShapeDtypeStruct((t, d), F32),
        scratch_shapes=[pltpu.VMEM((n_col, tm, tn), F32), pltpu.VMEM((tm, 1), F32),
                        pltpu.VMEM((tm, LANES), F32)],
        compiler_params=_params("arbitrary", "arbitrary"),
        name="ffn_down_skew",
    )(u, w_down, x2d, mod_all, g_post)


def _rope_tables(n):
    pos = np.arange(n)
    n_pairs_axis = HEAD_DIM // 4
    freqs = ROPE_THETA ** (-np.arange(n_pairs_axis, dtype=np.float64) / n_pairs_axis)
    ang = np.concatenate([(pos // GRID_W)[:, None] * freqs, (pos % GRID_W)[:, None] * freqs], axis=-1)
    c, s = np.cos(ang), np.sin(ang)
    return (np.concatenate([c, c], axis=-1).astype(np.float32),
            np.concatenate([-s, s], axis=-1).astype(np.float32))


def _dft_matrices(n):
    idx = np.arange(n)
    ang = ((idx[:, None] * idx[None, :]) % n) * (2.0 * math.pi / n)
    return np.cos(ang) * n ** -0.5, np.sin(ang) * n ** -0.5


def _sym_dft_tables(n):
    half = n // 2
    k = np.arange(half + DFT_PAD_ROWS)[:, None]
    ang = ((k * np.arange(half)[None, :]) % n) * (2.0 * math.pi / n)
    scale = n ** -0.5
    valid = k <= half
    cos_m = np.where(valid, np.cos(ang) * scale, 0.0).astype(BF16)
    sin_m = (np.sin(ang[:half]) * scale).astype(BF16)
    sgn = np.where(valid, (1 - 2 * (k % 2)) * scale, 0.0).astype(np.float32)
    i = np.arange(FLIP_BLOCK)
    flip_m = (i[:, None] + i[None, :] == FLIP_BLOCK).astype(BF16)
    return cos_m, sin_m, flip_m, sgn


def _deinterleave(v):
    return jnp.concatenate([v[..., 0::2], v[..., 1::2]], axis=-1)


def kernel(x, c, ctx, c_ctx, w_mod, b_mod, g_pre_mix, g_post_mix, g_pre_ffn, g_post_ffn,
           w_in, q_norm, k_norm, w_four, g_attn_out, g_four_out, w_out,
           w_up, conv_w, conv_b, w_down):
    b, n, d = x.shape
    ctx_len = ctx.shape[1]
    depth = w_mod.shape[0]

    perm_matrix = _deinterleave(np.eye(HEAD_DIM, dtype=BF16))
    w_in_b = _prep_w_in(w_in, perm_matrix)
    qn = _deinterleave(q_norm).reshape(depth, 1, HEAD_DIM)
    kn = _deinterleave(k_norm).reshape(depth, 1, HEAD_DIM)
    cos_t, sin_t = _rope_tables(n)
    dft_tables = _sym_dft_tables(n)
    cos_c, sin_c = (m.astype(BF16) for m in _dft_matrices(ctx_len))
    cs_ch = np.stack(_dft_matrices(FOURIER_GROUP)).astype(np.float32)
    m_fold = _fold_channel_dft(cs_ch, w_four)
    rows3 = lambda v: v.reshape(depth, 1, -1)
    g_pre_mix, g_post_mix, g_pre_ffn, g_post_ffn, g_attn_out, g_four_out, conv_b = map(
        rows3, (g_pre_mix, g_post_mix, g_pre_ffn, g_post_ffn, g_attn_out, g_four_out, conv_b))

    cc = jnp.zeros((MOD_ROWS, d), F32).at[:b].set(c).at[b].set(c_ctx)
    mod_all = _mod_proj(cc, w_mod, b_mod).reshape(depth, MOD_ROWS, N_MOD, d)

    xl = x.reshape(b * n, d)
    xc = ctx.reshape(b * ctx_len, d)
    tm = 512
    lat_row = lambda i: i // (n // tm)
    ctx_row = lambda i: b

    for i in range(depth):
        last = i == depth - 1
        q_l, k_l, v_l, ab_l, w_up_b = _in_proj(
            xl, mod_all, g_pre_mix, w_in_b, qn, kn, cos_t, sin_t, m_fold,
            layer=i, mod_row=lat_row, seq=n, rope=True, tm=tm, cast=(w_up,))
        if last:
            k_c, v_c = _ctx_kv(xc, mod_all, g_pre_mix, w_in_b, kn, layer=i, mod_row=ctx_row, tm=tm)
        else:
            q_c, k_c, v_c, ab_c = _in_proj(
                xc, mod_all, g_pre_mix, w_in_b, qn, kn, cos_t, sin_t, m_fold,
                layer=i, mod_row=ctx_row, seq=ctx_len, rope=False, tm=ctx_len)

        attn_l, w_out_b, w_down_b = _attention(
            q_l, [(k_l, v_l, n), (k_c, v_c, ctx_len)], g_attn_out,
            layer=i, q_seq=n, tq=512, cast=(w_out, w_down))
        four_l = _pos_dft_sym(dft_tables, ab_l, g_four_out, layer=i)
        xl, h_l = _out_proj(attn_l, four_l, w_out_b, xl, mod_all, g_post_mix, g_pre_ffn,
                            layer=i, mod_row=lat_row, tm=tm)
        u_l = _ffn_up(h_l, w_up_b, conv_w, conv_b, layer=i, seg_len=n, tm=n, tn=512)
        xl = _ffn_down_skew(u_l, w_down_b, xl, mod_all, g_post_ffn, layer=i,
                            mod_row=lambda r: r // (n // 1024), tm=1024, tn=512)

        if not last:
            attn_c, = _attention(q_c, [(k_c, v_c, ctx_len)], g_attn_out,
                                 layer=i, q_seq=ctx_len, tq=ctx_len)
            four_c = _pos_dft(cos_c, sin_c, ab_c, g_four_out, layer=i, tm=ctx_len)
            xc, h_c = _out_proj(attn_c, four_c, w_out_b, xc, mod_all, g_post_mix, g_pre_ffn,
                                layer=i, mod_row=ctx_row, tm=tm)
            u_c = _ffn_up(h_c, w_up_b, conv_w, conv_b, layer=i, seg_len=ctx_len,
                          tm=b * ctx_len, tn=512)
            xc = _ffn_down_skew(u_c, w_down_b, xc, mod_all, g_post_ffn, layer=i, mod_row=ctx_row,
                                tm=1024, tn=512)

    return xl.reshape(b, n, d)
```

```python
import functools
import math

import jax
import jax.numpy as jnp
import numpy as np
from jax import lax
from jax.experimental import pallas as pl
from jax.experimental.pallas import tpu as pltpu

GRID_W = 64
HEAD_DIM = 128
N_Q_HEADS = 8
N_KV_HEADS = 2
Q_PER_KV = N_Q_HEADS // N_KV_HEADS
FOURIER_GROUP = 128
N_FOURIER_GROUPS = 8
ATTN_WIDTH = N_Q_HEADS * HEAD_DIM
KV_WIDTH = N_KV_HEADS * HEAD_DIM
FOURIER_WIDTH = N_FOURIER_GROUPS * FOURIER_GROUP
Q_END = ATTN_WIDTH
K_END = Q_END + KV_WIDTH
V_END = K_END + KV_WIDTH
IN_WIDTH = V_END + FOURIER_WIDTH
ROPE_THETA = 10000.0
EPS = 1e-6
N_MOD = 6
Q_SCALE = HEAD_DIM ** -0.5 * math.log2(math.e)
MOD_ROWS = 16

VMEM_LIMIT_BYTES = 56 * 1024 * 1024

F32 = jnp.float32
BF16 = jnp.bfloat16


def _params(*semantics):
    return pltpu.CompilerParams(dimension_semantics=semantics, vmem_limit_bytes=VMEM_LIMIT_BYTES)


def _rms(x, g):
    ms = jnp.mean(x * x, axis=-1, keepdims=True)
    return x * lax.rsqrt(ms + EPS) * g


def _const_spec(shape):
    return pl.BlockSpec(shape, lambda *_: (0,) * len(shape), pipeline_mode=pl.Buffered(1))


def _layer_spec(shape, layer):
    zeros = (0,) * len(shape)
    return pl.BlockSpec((None,) + shape, lambda *_: (layer,) + zeros, pipeline_mode=pl.Buffered(1))


def _side_casts(sources, layer, n_steps, step_of):
    in_specs, out_specs, out_shapes = [], [], []
    for src in sources:
        _, r, c = src.shape
        rows = r // n_steps
        in_specs.append(pl.BlockSpec((None, rows, c), lambda *g: (layer, step_of(*g), 0)))
        out_specs.append(pl.BlockSpec((rows, c), lambda *g: (step_of(*g), 0)))
        out_shapes.append(jax.ShapeDtypeStruct((r, c), BF16))
    return in_specs, out_specs, out_shapes


def _run_side_casts(src_refs, dst_refs):
    for src, dst in zip(src_refs, dst_refs, strict=True):
        dst[...] = src[...].astype(BF16)


def _mod_spec(d, layer, row_of):
    return pl.BlockSpec((None, 1, N_MOD, d), lambda i, *_: (layer, row_of(i), 0, 0))


def _mod_kernel(c_ref, w_ref, b_ref, o_ref):
    c = c_ref[...]
    s = (c * jax.nn.sigmoid(c)).astype(BF16)
    o_ref[0] = jnp.dot(s, w_ref[0].astype(BF16), preferred_element_type=F32) + b_ref[0]


def _mod_proj(cc, w_mod, b_mod, tn=1024):
    depth, d, width = w_mod.shape
    return pl.pallas_call(
        _mod_kernel,
        grid=(depth, width // tn),
        in_specs=[
            pl.BlockSpec((MOD_ROWS, d), lambda l, j: (0, 0)),
            pl.BlockSpec((1, d, tn), lambda l, j: (l, 0, j)),
            pl.BlockSpec((1, 1, tn), lambda l, j: (l, 0, j)),
        ],
        out_specs=pl.BlockSpec((1, MOD_ROWS, tn), lambda l, j: (l, 0, j)),
        out_shape=jax.ShapeDtypeStruct((depth, MOD_ROWS, width), F32),
        compiler_params=_params("arbitrary", "arbitrary"),
        name="mod_proj",
    )(cc, w_mod, b_mod.reshape(depth, 1, width))


def _fold_kernel(cs_ref, w_ref, o_ref):
    for g in range(w_ref.shape[1]):
        w = w_ref[0, g]
        o_ref[0, g, :, :FOURIER_GROUP] = jnp.dot(
            cs_ref[0], w, preferred_element_type=F32, precision=lax.Precision.HIGHEST).astype(BF16)
        o_ref[0, g, :, FOURIER_GROUP:] = jnp.dot(
            cs_ref[1], w, preferred_element_type=F32, precision=lax.Precision.HIGHEST).astype(BF16)


def _fold_channel_dft(cs, w_four):
    depth, groups, c, _ = w_four.shape
    return pl.pallas_call(
        _fold_kernel,
        grid=(depth,),
        in_specs=[
            pl.BlockSpec((2, c, c), lambda l: (0, 0, 0)),
            pl.BlockSpec((1, groups, c, c), lambda l: (l, 0, 0, 0)),
        ],
        out_specs=pl.BlockSpec((1, groups, c, 2 * c), lambda l: (l, 0, 0, 0)),
        out_shape=jax.ShapeDtypeStruct((depth, groups, c, 2 * c), BF16),
        compiler_params=_params("arbitrary"),
        name="fold_channel_dft",
    )(cs, w_four)


def _prep_w_in_kernel(w_ref, p_ref, o_ref):
    for head in range(N_Q_HEADS + N_KV_HEADS):
        cols = slice(head * HEAD_DIM, (head + 1) * HEAD_DIM)
        o_ref[0, :, cols] = jnp.dot(w_ref[0, :, cols].astype(BF16), p_ref[...],
                                    preferred_element_type=F32).astype(BF16)
    o_ref[0, :, K_END:] = w_ref[0, :, K_END:].astype(BF16)


def _prep_w_in(w_in, perm_matrix, rows=256):
    depth, d, width = w_in.shape
    return pl.pallas_call(
        _prep_w_in_kernel,
        grid=(depth, d // rows),
        in_specs=[
            pl.BlockSpec((1, rows, width), lambda l, j: (l, j, 0)),
            pl.BlockSpec((HEAD_DIM, HEAD_DIM), lambda l, j: (0, 0)),
        ],
        out_specs=pl.BlockSpec((1, rows, width), lambda l, j: (l, j, 0)),
        out_shape=jax.ShapeDtypeStruct((depth, d, width), BF16),
        compiler_params=_params("arbitrary", "arbitrary"),
        name="prep_w_in",
    )(w_in, perm_matrix)


def _in_proj_kernel(x_ref, mod_ref, gpre_ref, w_ref, qn_ref, kn_ref, cos_ref, sin_ref, m_ref, *rest,
                    rope, n_sub, n_cast):
    cast_src, (q_ref, k_ref, v_ref, ab_ref), cast_dst = rest[:n_cast], rest[n_cast:n_cast + 4], rest[n_cast + 4:]
    _run_side_casts(cast_src, cast_dst)
    mod = mod_ref[0]
    scale_g = (1.0 + mod[1:2]) * gpre_ref[...]
    rows = x_ref.shape[0] // n_sub
    for s in range(n_sub):
        r = slice(s * rows, (s + 1) * rows)
        h = _rms(x_ref[r, :], scale_g) + mod[0:1]
        p = jnp.dot(h.astype(BF16), w_ref[...], preferred_element_type=F32)

        def head(col, gain, post_scale):
            y = _rms(p[:, col:col + HEAD_DIM], gain)
            if rope:
                y = y * cos_ref[r, :] + pltpu.roll(y, HEAD_DIM // 2, axis=1) * sin_ref[r, :]
            if post_scale != 1.0:
                y = y * post_scale
            return y.astype(BF16)

        for i in range(N_Q_HEADS):
            q_ref[r, i * HEAD_DIM:(i + 1) * HEAD_DIM] = head(i * HEAD_DIM, qn_ref[...], Q_SCALE)
        for i in range(N_KV_HEADS):
            k_ref[r, i * HEAD_DIM:(i + 1) * HEAD_DIM] = head(Q_END + i * HEAD_DIM, kn_ref[...], 1.0)
        v_ref[r, :] = p[:, K_END:V_END].astype(BF16)
        for g in range(N_FOURIER_GROUPS):
            lo = V_END + g * FOURIER_GROUP
            ab = jnp.dot(p[:, lo:lo + FOURIER_GROUP].astype(BF16), m_ref[g], preferred_element_type=F32)
            ab_ref[r, g * FOURIER_GROUP:(g + 1) * FOURIER_GROUP] = ab[:, :FOURIER_GROUP].astype(BF16)
            ab_ref[r, FOURIER_WIDTH + g * FOURIER_GROUP:FOURIER_WIDTH + (g + 1) * FOURIER_GROUP] = (
                ab[:, FOURIER_GROUP:].astype(BF16))


def _in_proj(x2d, mod_all, g_pre, w_in, qn, kn, cos_t, sin_t, m_fold, *, layer, mod_row, seq, rope, tm,
             n_sub=1, cast=()):
    t, d = x2d.shape
    per_seq = seq // tm
    rope_map = (lambda i: (i % per_seq, 0)) if rope else (lambda i: (0, 0))
    cast_in, cast_out, cast_shapes = _side_casts(cast, layer, t // tm, lambda i: i)
    return pl.pallas_call(
        functools.partial(_in_proj_kernel, rope=rope, n_sub=n_sub, n_cast=len(cast)),
        grid=(t // tm,),
        in_specs=[
            pl.BlockSpec((tm, d), lambda i: (i, 0)),
            _mod_spec(d, layer, mod_row),
            _layer_spec((1, d), layer),
            _layer_spec((d, IN_WIDTH), layer),
            _layer_spec((1, HEAD_DIM), layer),
            _layer_spec((1, HEAD_DIM), layer),
            pl.BlockSpec((tm, HEAD_DIM), rope_map),
            pl.BlockSpec((tm, HEAD_DIM), rope_map),
            _layer_spec((N_FOURIER_GROUPS, FOURIER_GROUP, 2 * FOURIER_GROUP), layer),
        ] + cast_in,
        out_specs=[
            pl.BlockSpec((tm, ATTN_WIDTH), lambda i: (i, 0)),
            pl.BlockSpec((tm, KV_WIDTH), lambda i: (i, 0)),
            pl.BlockSpec((tm, KV_WIDTH), lambda i: (i, 0)),
            pl.BlockSpec((tm, 2 * FOURIER_WIDTH), lambda i: (i, 0)),
        ] + cast_out,
        out_shape=[
            jax.ShapeDtypeStruct((t, ATTN_WIDTH), BF16),
            jax.ShapeDtypeStruct((t, KV_WIDTH), BF16),
            jax.ShapeDtypeStruct((t, KV_WIDTH), BF16),
            jax.ShapeDtypeStruct((t, 2 * FOURIER_WIDTH), BF16),
        ] + cast_shapes,
        compiler_params=_params("arbitrary"),
        name="in_proj",
    )(x2d, mod_all, g_pre, w_in, qn, kn, cos_t, sin_t, m_fold, *cast)


def _ctx_kv_kernel(x_ref, mod_ref, gpre_ref, w_ref, kn_ref, k_ref, v_ref):
    mod = mod_ref[0]
    h = _rms(x_ref[...], (1.0 + mod[1:2]) * gpre_ref[...]) + mod[0:1]
    p = jnp.dot(h.astype(BF16), w_ref[...], preferred_element_type=F32)
    for i in range(N_KV_HEADS):
        cols = slice(i * HEAD_DIM, (i + 1) * HEAD_DIM)
        k_ref[:, cols] = _rms(p[:, cols], kn_ref[...]).astype(BF16)
    v_ref[...] = p[:, KV_WIDTH:].astype(BF16)


def _ctx_kv(x2d, mod_all, g_pre, w_in, kn, *, layer, mod_row, tm):
    t, d = x2d.shape
    assert Q_END % (2 * KV_WIDTH) == 0
    return pl.pallas_call(
        _ctx_kv_kernel,
        grid=(t // tm,),
        in_specs=[
            pl.BlockSpec((tm, d), lambda i: (i, 0)),
            _mod_spec(d, layer, mod_row),
            _layer_spec((1, d), layer),
            pl.BlockSpec((None, d, 2 * KV_WIDTH), lambda i: (layer, 0, Q_END // (2 * KV_WIDTH)),
                         pipeline_mode=pl.Buffered(1)),
            _layer_spec((1, HEAD_DIM), layer),
        ],
        out_specs=[pl.BlockSpec((tm, KV_WIDTH), lambda i: (i, 0))] * 2,
        out_shape=[jax.ShapeDtypeStruct((t, KV_WIDTH), BF16)] * 2,
        compiler_params=_params("arbitrary"),
        name="ctx_kv",
    )(x2d, mod_all, g_pre, w_in, kn)


def _attn_kernel(*refs, n_seg, n_cast, key_chunk):
    q_ref = refs[0]
    kv_refs = refs[1:1 + 2 * n_seg]
    g_ref = refs[1 + 2 * n_seg]
    rest = refs[2 + 2 * n_seg:]
    cast_src, o_ref, cast_dst, acc_ref = rest[:n_cast], rest[n_cast], rest[n_cast + 1:-1], rest[-1]
    _run_side_casts(cast_src, cast_dst)

    chunks = []
    for i in range(n_seg):
        total = kv_refs[2 * i].shape[0]
        for lo in range(0, total, key_chunk):
            chunks.append((kv_refs[2 * i], kv_refs[2 * i + 1], lo, min(key_chunk, total - lo)))

    def chunk_scores(h, c):
        k_ref, _, lo, size = chunks[c]
        kv = (h // Q_PER_KV) * HEAD_DIM
        return lax.dot_general(k_ref[lo:lo + size, kv:kv + HEAD_DIM], q_ref[:, h * HEAD_DIM:(h + 1) * HEAD_DIM],
                               (((1,), (1,)), ((), ())), preferred_element_type=F32)

    def chunk_output(h, c, p):
        _, v_ref, lo, size = chunks[c]
        kv = (h // Q_PER_KV) * HEAD_DIM
        return lax.dot_general(v_ref[lo:lo + size, kv:kv + HEAD_DIM], p,
                               (((0,), (0,)), ((), ())), preferred_element_type=F32)

    def col_max(m, s):
        cm = jnp.max(s, axis=0, keepdims=True)
        return cm if m is None else jnp.maximum(m, cm)

    scores, m = [], None
    for c in range(len(chunks)):
        scores.append(chunk_scores(0, c))
        m = col_max(m, scores[-1])
    for h in range(N_Q_HEADS):
        next_scores, next_m = [], None
        denom, out_t = None, None
        for c in range(len(chunks)):
            if h + 1 < N_Q_HEADS:
                next_scores.append(chunk_scores(h + 1, c))
                next_m = col_max(next_m, next_scores[-1])
            p = jnp.exp2(scores[c] - m)
            ps = jnp.sum(p, axis=0, keepdims=True)
            po = chunk_output(h, c, p.astype(BF16))
            denom = ps if denom is None else denom + ps
            out_t = po if out_t is None else out_t + po
        acc_ref[:, h * HEAD_DIM:(h + 1) * HEAD_DIM] = (out_t / denom).T
        scores, m = next_scores, next_m
    o_ref[...] = _rms(acc_ref[...], g_ref[...]).astype(BF16)


def _attention(q, segments, g_attn, *, layer, q_seq, tq, cast=(), key_chunk=1024):
    t = q.shape[0]
    per_seq = q_seq // tq
    in_specs = [pl.BlockSpec((tq, ATTN_WIDTH), lambda b, j: (b * per_seq + j, 0))]
    args = [q]
    for k, v, rows in segments:
        in_specs += [pl.BlockSpec((rows, KV_WIDTH), lambda b, j: (b, 0))] * 2
        args += [k, v]
    in_specs.append(_layer_spec((1, ATTN_WIDTH), layer))
    args.append(g_attn)
    cast_in, cast_out, cast_shapes = _side_casts(cast, layer, t // tq, lambda b, j: b * per_seq + j)
    return pl.pallas_call(
        functools.partial(_attn_kernel, n_seg=len(segments), n_cast=len(cast), key_chunk=key_chunk),
        grid=(t // q_seq, per_seq),
        in_specs=in_specs + cast_in,
        out_specs=[pl.BlockSpec((tq, ATTN_WIDTH), lambda b, j: (b * per_seq + j, 0))] + cast_out,
        out_shape=[jax.ShapeDtypeStruct((t, ATTN_WIDTH), BF16)] + cast_shapes,
        scratch_shapes=[pltpu.VMEM((tq, ATTN_WIDTH), F32)],
        compiler_params=_params("arbitrary", "arbitrary"),
        name="attention",
    )(*args, *cast)


def _pos_dft_kernel(c_ref, s_ref, a_ref, b_ref, g_ref, o_ref):
    y = (jnp.dot(c_ref[...], a_ref[...], preferred_element_type=F32)
         - jnp.dot(s_ref[...], b_ref[...], preferred_element_type=F32))
    o_ref[...] = _rms(y, g_ref[...]).astype(BF16)


def _pos_dft(cos_m, sin_m, ab, g_four, *, layer, tm):
    seq = cos_m.shape[0]
    t = ab.shape[0]
    per_seq = seq // tm
    return pl.pallas_call(
        _pos_dft_kernel,
        grid=(t // seq, per_seq),
        in_specs=[
            pl.BlockSpec((tm, seq), lambda b, j: (j, 0)),
            pl.BlockSpec((tm, seq), lambda b, j: (j, 0)),
            pl.BlockSpec((seq, FOURIER_WIDTH), lambda b, j: (b, 0)),
            pl.BlockSpec((seq, FOURIER_WIDTH), lambda b, j: (b, 1)),
            _layer_spec((1, FOURIER_WIDTH), layer),
        ],
        out_specs=pl.BlockSpec((tm, FOURIER_WIDTH), lambda b, j: (b * per_seq + j, 0)),
        out_shape=jax.ShapeDtypeStruct((t, FOURIER_WIDTH), BF16),
        compiler_params=_params("arbitrary", "arbitrary"),
        name="pos_dft",
    )(cos_m, sin_m, ab, ab, g_four)


FLIP_BLOCK = 256
DFT_PAD_ROWS = 16


def _pos_dft_sym_kernel(alo_ref, ahi_ref, blo_ref, bhi_ref, c_ref, s_ref, q_ref, sgn_ref, g_ref,
                        o_ref, e_ref, d_ref, h_ref):
    half = alo_ref.shape[0]
    blk = q_ref.shape[0]
    nblk = half // blk
    qm = q_ref[...]
    is_row0 = lax.broadcasted_iota(jnp.int32, (blk, 1), 0) == 0

    def flipped_block(src_ref, bt, first_row):
        lo = (nblk - 1 - bt) * blk
        y = jnp.dot(qm, src_ref[lo:lo + blk, :], preferred_element_type=F32)
        if bt > 0:
            first_row = src_ref[lo + blk:lo + blk + 1, :].astype(F32)
        return jnp.where(is_row0, first_row, y)

    for bt in range(nblk):
        r = slice(bt * blk, (bt + 1) * blk)
        e_ref[r, :] = (alo_ref[r, :].astype(F32) + flipped_block(ahi_ref, bt, 0.0)).astype(BF16)
        d_ref[r, :] = (blo_ref[r, :].astype(F32) - flipped_block(bhi_ref, bt, 0.0)).astype(BF16)

    a_mid = ahi_ref[0:1, :].astype(F32)
    g = g_ref[...]
    mid_row = None
    for kb in range(nblk):
        r = slice(kb * blk, (kb + 1) * blk)
        rows = blk + (DFT_PAD_ROWS if kb == nblk - 1 else 0)
        cr = slice(kb * blk, kb * blk + rows)
        yc = jnp.dot(c_ref[cr, :], e_ref[...], preferred_element_type=F32) + sgn_ref[cr, :] * a_mid
        ys = jnp.dot(s_ref[r, :], d_ref[...], preferred_element_type=F32)
        o_ref[r, :] = _rms(yc[:blk] - ys, g).astype(BF16)
        h_ref[r, :] = _rms(yc[:blk] + ys, g).astype(BF16)
        if kb == nblk - 1:
            mid_row = _rms(yc[blk:blk + 1], g)

    for bs in range(nblk):
        r = slice(half + bs * blk, half + (bs + 1) * blk)
        o_ref[r, :] = flipped_block(h_ref, bs, mid_row).astype(BF16)


def _pos_dft_sym(tables, ab, g_four, *, layer):
    cos_m, sin_m, flip_m, sgn = tables
    half = sin_m.shape[0]
    t = ab.shape[0]
    blocks = lambda row, col: pl.BlockSpec((half, FOURIER_WIDTH), lambda b: (2 * b + row, col))
    return pl.pallas_call(
        _pos_dft_sym_kernel,
        grid=(t // (2 * half),),
        in_specs=[
            blocks(0, 0), blocks(1, 0), blocks(0, 1), blocks(1, 1),
            _const_spec(cos_m.shape),
            _const_spec(sin_m.shape),
            _const_spec(flip_m.shape),
            _const_spec(sgn.shape),
            _layer_spec((1, FOURIER_WIDTH), layer),
        ],
        out_specs=pl.BlockSpec((2 * half, FOURIER_WIDTH), lambda b: (b, 0)),
        out_shape=jax.ShapeDtypeStruct((t, FOURIER_WIDTH), BF16),
        scratch_shapes=[pltpu.VMEM((half, FOURIER_WIDTH), BF16)] * 3,
        compiler_params=_params("arbitrary"),
        name="pos_dft_sym",
    )(ab, ab, ab, ab, cos_m, sin_m, flip_m, sgn, g_four)


def _out_proj_kernel(a_ref, f_ref, w_ref, x_ref, mod_ref, gpost_ref, gpre_ref, xo_ref, h_ref, *, n_sub):
    mod = mod_ref[0]
    gate_g = mod[2:3] * gpost_ref[...]
    scale_g = (1.0 + mod[4:5]) * gpre_ref[...]
    rows = x_ref.shape[0] // n_sub
    def matmuls(s):
        r = slice(s * rows, (s + 1) * rows)
        return (jnp.dot(a_ref[r, :], w_ref[:ATTN_WIDTH, :], preferred_element_type=F32)
                + jnp.dot(f_ref[r, :], w_ref[ATTN_WIDTH:, :], preferred_element_type=F32))

    pending = matmuls(0)
    for s in range(n_sub):
        r = slice(s * rows, (s + 1) * rows)
        mix = pending
        if s + 1 < n_sub:
            pending = matmuls(s + 1)
        xn = x_ref[r, :] + _rms(mix, gate_g)
        xo_ref[r, :] = xn
        h_ref[r, :] = (_rms(xn, scale_g) + mod[3:4]).astype(BF16)


def _out_proj(a, f, w_out, x2d, mod_all, g_post, g_pre_ffn, *, layer, mod_row, tm, n_sub=4):
    t, d = x2d.shape
    return pl.pallas_call(
        functools.partial(_out_proj_kernel, n_sub=n_sub),
        grid=(t // tm,),
        in_specs=[
            pl.BlockSpec((tm, ATTN_WIDTH), lambda i: (i, 0)),
            pl.BlockSpec((tm, FOURIER_WIDTH), lambda i: (i, 0)),
            _const_spec((ATTN_WIDTH + FOURIER_WIDTH, d)),
            pl.BlockSpec((tm, d), lambda i: (i, 0)),
            _mod_spec(d, layer, mod_row),
            _layer_spec((1, d), layer),
            _layer_spec((1, d), layer),
        ],
        out_specs=[
            pl.BlockSpec((tm, d), lambda i: (i, 0)),
            pl.BlockSpec((tm, d), lambda i: (i, 0)),
        ],
        out_shape=[
            jax.ShapeDtypeStruct((t, d), F32),
            jax.ShapeDtypeStruct((t, d), BF16),
        ],
        compiler_params=_params("arbitrary"),
        name="out_proj",
    )(a, f, w_out, x2d, mod_all, g_post, g_pre_ffn)


CONV_HALO = 8


def _ffn_up_kernel(h_ref, wg_ref, wv_ref, cw_ref, cb_ref, o_ref, *, seg_len, n_sub):
    rows, tn = o_ref.shape
    chunk = rows // n_sub
    pos = lax.broadcasted_iota(jnp.int32, (rows, 1), 0) % seg_len
    first, final = pos == 0, pos == seg_len - 1
    cw = cw_ref[...]
    cb = cb_ref[...]

    def span(s):
        return max(s * chunk - CONV_HALO, 0), min((s + 1) * chunk + CONV_HALO, rows)

    def matmuls(s):
        lo, hi = span(s)
        h = h_ref[lo:hi, :]
        return (jnp.dot(h, wg_ref[...], preferred_element_type=F32),
                jnp.dot(h, wv_ref[...], preferred_element_type=F32))

    def epilogue(s, gate, val):
        lo, hi = span(s)
        prev = jnp.where(first[lo:hi], 0.0, pltpu.roll(gate, 1, axis=0))
        nxt = jnp.where(final[lo:hi], 0.0, pltpu.roll(gate, hi - lo - 1, axis=0))
        c = prev * cw[0:1] + gate * cw[1:2] + nxt * cw[2:3] + cb
        act = 0.5 * c * (1.0 + jnp.tanh(math.sqrt(2.0 / math.pi) * (c + 0.044715 * (c * c * c))))
        own = slice(s * chunk - lo, (s + 1) * chunk - lo)
        o_ref[s * chunk:(s + 1) * chunk, :] = (act * val)[own].astype(BF16)

    pending = matmuls(0)
    for s in range(n_sub):
        ready = pending
        if s + 1 < n_sub:
            pending = matmuls(s + 1)
        epilogue(s, *ready)


def _ffn_up(h, w_up, conv_w, conv_b, *, layer, seg_len, tm, tn, n_sub=2):
    t, d = h.shape
    ffn = conv_w.shape[-1]
    n_col = ffn // tn
    return pl.pallas_call(
        functools.partial(_ffn_up_kernel, seg_len=seg_len, n_sub=n_sub),
        grid=(t // tm, n_col),
        in_specs=[
            pl.BlockSpec((tm, d), lambda i, j: (i, 0)),
            pl.BlockSpec((d, tn), lambda i, j: (0, j)),
            pl.BlockSpec((d, tn), lambda i, j: (0, j + n_col)),
            pl.BlockSpec((None, 3, tn), lambda i, j: (layer, 0, j)),
            pl.BlockSpec((None, 1, tn), lambda i, j: (layer, 0, j)),
        ],
        out_specs=pl.BlockSpec((tm, tn), lambda i, j: (i, j)),
        out_shape=jax.ShapeDtypeStruct((t, ffn), BF16),
        compiler_params=_params("arbitrary", "arbitrary"),
        name="ffn_up",
    )(h, w_up, w_up, conv_w, conv_b)


LANES = 128


def _ffn_down_skew_kernel(u_ref, w_ref, x_ref, mod_ref, gpost_ref, xo_ref, y_ref, inv_ref, ssq_ref):
    i, j = pl.program_id(0), pl.program_id(1)
    n_tiles = pl.num_programs(0) - 1
    n_col, _, tn = y_ref.shape

    @pl.when((i == 0) & (j == 0))
    def _():
        y_ref[...] = jnp.zeros_like(y_ref)
        inv_ref[...] = jnp.zeros_like(inv_ref)

    @pl.when(j == 0)
    def _():
        ssq_ref[...] = jnp.zeros_like(ssq_ref)

    def finish_previous():
        gate_g = mod_ref[0][N_MOD - 1:N_MOD] * gpost_ref[...]
        xo_ref[...] = x_ref[...] + (y_ref[j] * inv_ref[...]) * gate_g

    @pl.when(i < n_tiles)
    def _():
        finish_previous()
        y = jnp.dot(u_ref[...], w_ref[...], preferred_element_type=F32)
        y_ref[j] = y
        sq = y * y
        ssq_ref[...] += functools.reduce(
            jnp.add, [sq[:, c * LANES:(c + 1) * LANES] for c in range(tn // LANES)])

    @pl.when(i == n_tiles)
    def _():
        finish_previous()

    @pl.when((j == n_col - 1) & (i < n_tiles))
    def _():
        ssq = jnp.sum(ssq_ref[...], axis=-1, keepdims=True)
        inv_ref[...] = lax.rsqrt(ssq / (n_col * tn) + EPS)


def _ffn_down_skew(u, w_down, x2d, mod_all, g_post, *, layer, mod_row, tm, tn):
    t, d = x2d.shape
    ffn = u.shape[1]
    n_tiles = t // tm
    n_col = d // tn
    prev = lambda i: jnp.maximum(i - 1, 0)
    col = lambda i, j: jnp.where(i > 0, j, 0)
    return pl.pallas_call(
        _ffn_down_skew_kernel,
        grid=(n_tiles + 1, n_col),
        in_specs=[
            pl.BlockSpec((tm, ffn), lambda i, j: (jnp.minimum(i, n_tiles - 1), 0)),
            pl.BlockSpec((ffn, tn), lambda i, j: (0, jnp.where(i < n_tiles, j, n_col - 1))),
            pl.BlockSpec((tm, tn), lambda i, j: (prev(i), col(i, j))),
            pl.BlockSpec((None, 1, N_MOD, tn), lambda i, j: (layer, mod_row(prev(i)), 0, col(i, j))),
            pl.BlockSpec((None, 1, tn), lambda i, j: (layer, 0, col(i, j))),
        ],
        out_specs=pl.BlockSpec((tm, tn), lambda i, j: (prev(i), col(i, j))),
        out_shape=jax.ShapeDtypeStruct((t, d), F32),
        scratch_shapes=[pltpu.VMEM((n_col, tm, tn), F32), pltpu.VMEM((tm, 1), F32),
                        pltpu.VMEM((tm, LANES), F32)],
        compiler_params=_params("arbitrary", "arbitrary"),
        name="ffn_down_skew",
    )(u, w_down, x2d, mod_all, g_post)


def _rope_tables(n):
    pos = np.arange(n)
    n_pairs_axis = HEAD_DIM // 4
    freqs = ROPE_THETA ** (-np.arange(n_pairs_axis, dtype=np.float64) / n_pairs_axis)
    ang = np.concatenate([(pos // GRID_W)[:, None] * freqs, (pos % GRID_W)[:, None] * freqs], axis=-1)
    c, s = np.cos(ang), np.sin(ang)
    return (np.concatenate([c, c], axis=-1).astype(np.float32),
            np.concatenate([-s, s], axis=-1).astype(np.float32))


def _dft_matrices(n):
    idx = np.arange(n)
    ang = ((idx[:, None] * idx[None, :]) % n) * (2.0 * math.pi / n)
    return np.cos(ang) * n ** -0.5, np.sin(ang) * n ** -0.5


def _sym_dft_tables(n):
    half = n // 2
    k = np.arange(half + DFT_PAD_ROWS)[:, None]
    ang = ((k * np.arange(half)[None, :]) % n) * (2.0 * math.pi / n)
    scale = n ** -0.5
    valid = k <= half
    cos_m = jnp.asarray(np.where(valid, np.cos(ang) * scale, 0.0), F32).astype(BF16)
    sin_m = jnp.asarray(np.sin(ang[:half]) * scale, F32).astype(BF16)
    sgn = np.where(valid, (1 - 2 * (k % 2)) * scale, 0.0).astype(np.float32)
    i = np.arange(FLIP_BLOCK)
    flip_m = (i[:, None] + i[None, :] == FLIP_BLOCK).astype(BF16)
    return cos_m, sin_m, flip_m, sgn


def _deinterleave(v):
    return jnp.concatenate([v[..., 0::2], v[..., 1::2]], axis=-1)


def kernel(x, c, ctx, c_ctx, w_mod, b_mod, g_pre_mix, g_post_mix, g_pre_ffn, g_post_ffn,
           w_in, q_norm, k_norm, w_four, g_attn_out, g_four_out, w_out,
           w_up, conv_w, conv_b, w_down):
    b, n, d = x.shape
    ctx_len = ctx.shape[1]
    depth = w_mod.shape[0]

    perm_matrix = _deinterleave(np.eye(HEAD_DIM, dtype=BF16))
    w_in_b = _prep_w_in(w_in, perm_matrix)
    qn = _deinterleave(q_norm).reshape(depth, 1, HEAD_DIM)
    kn = _deinterleave(k_norm).reshape(depth, 1, HEAD_DIM)
    cos_t, sin_t = _rope_tables(n)
    dft_tables = _sym_dft_tables(n)
    cos_c, sin_c = (jnp.asarray(m, F32).astype(BF16) for m in _dft_matrices(ctx_len))
    cs_ch = np.stack(_dft_matrices(FOURIER_GROUP)).astype(np.float32)
    m_fold = _fold_channel_dft(cs_ch, w_four)
    rows3 = lambda v: v.reshape(depth, 1, -1)
    g_pre_mix, g_post_mix, g_pre_ffn, g_post_ffn, g_attn_out, g_four_out, conv_b = map(
        rows3, (g_pre_mix, g_post_mix, g_pre_ffn, g_post_ffn, g_attn_out, g_four_out, conv_b))

    cc = jnp.zeros((MOD_ROWS, d), F32).at[:b].set(c).at[b].set(c_ctx)
    mod_all = _mod_proj(cc, w_mod, b_mod).reshape(depth, MOD_ROWS, N_MOD, d)

    xl = x.reshape(b * n, d)
    xc = ctx.reshape(b * ctx_len, d)
    tm = 512
    lat_row = lambda i: i // (n // tm)
    ctx_row = lambda i: b

    for i in range(depth):
        last = i == depth - 1
        q_l, k_l, v_l, ab_l, w_up_b = _in_proj(
            xl, mod_all, g_pre_mix, w_in_b, qn, kn, cos_t, sin_t, m_fold,
            layer=i, mod_row=lat_row, seq=n, rope=True, tm=tm, cast=(w_up,))
        if last:
            k_c, v_c = _ctx_kv(xc, mod_all, g_pre_mix, w_in_b, kn, layer=i, mod_row=ctx_row, tm=tm)
        else:
            q_c, k_c, v_c, ab_c = _in_proj(
                xc, mod_all, g_pre_mix, w_in_b, qn, kn, cos_t, sin_t, m_fold,
                layer=i, mod_row=ctx_row, seq=ctx_len, rope=False, tm=ctx_len)

        attn_l, w_out_b, w_down_b = _attention(
            q_l, [(k_l, v_l, n), (k_c, v_c, ctx_len)], g_attn_out,
            layer=i, q_seq=n, tq=1024, cast=(w_out, w_down))
        four_l = _pos_dft_sym(dft_tables, ab_l, g_four_out, layer=i)
        xl, h_l = _out_proj(attn_l, four_l, w_out_b, xl, mod_all, g_post_mix, g_pre_ffn,
                            layer=i, mod_row=lat_row, tm=tm)
        u_l = _ffn_up(h_l, w_up_b, conv_w, conv_b, layer=i, seg_len=n, tm=n, tn=512)
        xl = _ffn_down_skew(u_l, w_down_b, xl, mod_all, g_post_ffn, layer=i,
                            mod_row=lambda r: r // (n // 1024), tm=1024, tn=512)

        if not last:
            attn_c, = _attention(q_c, [(k_c, v_c, ctx_len)], g_attn_out,
                                 layer=i, q_seq=ctx_len, tq=ctx_len)
            four_c = _pos_dft(cos_c, sin_c, ab_c, g_four_out, layer=i, tm=ctx_len)
            xc, h_c = _out_proj(attn_c, four_c, w_out_b, xc, mod_all, g_post_mix, g_pre_ffn,
                                layer=i, mod_row=ctx_row, tm=tm)
            u_c = _ffn_up(h_c, w_up_b, conv_w, conv_b, layer=i, seg_len=ctx_len,
                          tm=b * ctx_len, tn=512)
            xc = _ffn_down_skew(u_c, w_down_b, xc, mod_all, g_post_ffn, layer=i, mod_row=ctx_row,
                                tm=1024, tn=512)

    return xl.reshape(b, n, d)
```

```python
import functools
import math

import jax
import jax.numpy as jnp
import numpy as np
from jax import lax
from jax.experimental import pallas as pl
from jax.experimental.pallas import tpu as pltpu

GRID_W = 64
HEAD_DIM = 128
N_Q_HEADS = 8
N_KV_HEADS = 2
Q_PER_KV = N_Q_HEADS // N_KV_HEADS
FOURIER_GROUP = 128
N_FOURIER_GROUPS = 8
ATTN_WIDTH = N_Q_HEADS * HEAD_DIM
KV_WIDTH = N_KV_HEADS * HEAD_DIM
FOURIER_WIDTH = N_FOURIER_GROUPS * FOURIER_GROUP
Q_END = ATTN_WIDTH
K_END = Q_END + KV_WIDTH
V_END = K_END + KV_WIDTH
IN_WIDTH = V_END + FOURIER_WIDTH
ROPE_THETA = 10000.0
EPS = 1e-6
N_MOD = 6
Q_SCALE = HEAD_DIM ** -0.5 * math.log2(math.e)
MOD_ROWS = 16

VMEM_LIMIT_BYTES = 56 * 1024 * 1024

ROW_TILE = 512
ATTN_Q_TILE = 1024
ATTN_KEY_CHUNK = 1024
FFN_COL_TILE = 512
FFN_DOWN_ROW_TILE = 1024
FFN_UP_ROW_CHUNKS = 2
OUT_PROJ_SUB_BLOCKS = 4
MOD_COL_TILE = 1024
PREP_ROW_TILE = 256

F32 = jnp.float32
BF16 = jnp.bfloat16


def _params(*semantics):
    return pltpu.CompilerParams(dimension_semantics=semantics, vmem_limit_bytes=VMEM_LIMIT_BYTES)


def _rms(x, g):
    ms = jnp.mean(x * x, axis=-1, keepdims=True)
    return x * lax.rsqrt(ms + EPS) * g


def _const_spec(shape):
    return pl.BlockSpec(shape, lambda *_: (0,) * len(shape), pipeline_mode=pl.Buffered(1))


def _layer_spec(shape, layer):
    zeros = (0,) * len(shape)
    return pl.BlockSpec((None,) + shape, lambda *_: (layer,) + zeros, pipeline_mode=pl.Buffered(1))


def _side_casts(sources, layer, n_steps, step_of):
    in_specs, out_specs, out_shapes = [], [], []
    for src in sources:
        _, r, c = src.shape
        rows = r // n_steps
        in_specs.append(pl.BlockSpec((None, rows, c), lambda *g: (layer, step_of(*g), 0)))
        out_specs.append(pl.BlockSpec((rows, c), lambda *g: (step_of(*g), 0)))
        out_shapes.append(jax.ShapeDtypeStruct((r, c), BF16))
    return in_specs, out_specs, out_shapes


def _run_side_casts(src_refs, dst_refs):
    for src, dst in zip(src_refs, dst_refs, strict=True):
        dst[...] = src[...].astype(BF16)


def _mod_spec(d, row_of):
    return pl.BlockSpec((1, N_MOD, d), lambda i, *_: (row_of(i), 0, 0))


def _mod_block(c_ref, w_ref, b_ref, o_ref):
    c = c_ref[...]
    s = (c * jax.nn.sigmoid(c)).astype(BF16)
    o_ref[...] = jnp.dot(s, w_ref[...].astype(BF16), preferred_element_type=F32) + b_ref[...]


def _mod_block_specs(d, width, layer, n_steps):
    tn = width // n_steps
    in_specs = [
        pl.BlockSpec((MOD_ROWS, d), lambda i, *_: (0, 0)),
        pl.BlockSpec((None, d, tn), lambda i, *_: (layer, 0, i)),
        pl.BlockSpec((None, 1, tn), lambda i, *_: (layer, 0, i)),
    ]
    out_spec = pl.BlockSpec((MOD_ROWS, tn), lambda i, *_: (0, i))
    return in_specs, out_spec, jax.ShapeDtypeStruct((MOD_ROWS, width), F32)


def _mod_proj(cc, w_mod, b_mod, *, layer):
    _, d, width = w_mod.shape
    n_steps = width // MOD_COL_TILE
    in_specs, out_spec, out_shape = _mod_block_specs(d, width, layer, n_steps)
    return pl.pallas_call(
        _mod_block,
        grid=(n_steps,),
        in_specs=in_specs,
        out_specs=out_spec,
        out_shape=out_shape,
        compiler_params=_params("arbitrary"),
        name="mod_proj",
    )(cc, w_mod, b_mod)


def _fold_kernel(cs_ref, w_ref, o_ref):
    for g in range(w_ref.shape[1]):
        w = w_ref[0, g]
        o_ref[0, g, :, :FOURIER_GROUP] = jnp.dot(
            cs_ref[0], w, preferred_element_type=F32, precision=lax.Precision.HIGHEST).astype(BF16)
        o_ref[0, g, :, FOURIER_GROUP:] = jnp.dot(
            cs_ref[1], w, preferred_element_type=F32, precision=lax.Precision.HIGHEST).astype(BF16)


def _fold_channel_dft(cs, w_four):
    depth, groups, c, _ = w_four.shape
    return pl.pallas_call(
        _fold_kernel,
        grid=(depth,),
        in_specs=[
            pl.BlockSpec((2, c, c), lambda l: (0, 0, 0)),
            pl.BlockSpec((1, groups, c, c), lambda l: (l, 0, 0, 0)),
        ],
        out_specs=pl.BlockSpec((1, groups, c, 2 * c), lambda l: (l, 0, 0, 0)),
        out_shape=jax.ShapeDtypeStruct((depth, groups, c, 2 * c), BF16),
        compiler_params=_params("arbitrary"),
        name="fold_channel_dft",
    )(cs, w_four)


def _prep_w_in_kernel(w_ref, p_ref, o_ref):
    for head in range(N_Q_HEADS + N_KV_HEADS):
        cols = slice(head * HEAD_DIM, (head + 1) * HEAD_DIM)
        o_ref[0, :, cols] = jnp.dot(w_ref[0, :, cols].astype(BF16), p_ref[...],
                                    preferred_element_type=F32).astype(BF16)
    o_ref[0, :, K_END:] = w_ref[0, :, K_END:].astype(BF16)


def _prep_w_in(w_in, perm_matrix):
    depth, d, width = w_in.shape
    rows = PREP_ROW_TILE
    return pl.pallas_call(
        _prep_w_in_kernel,
        grid=(depth, d // rows),
        in_specs=[
            pl.BlockSpec((1, rows, width), lambda l, j: (l, j, 0)),
            pl.BlockSpec((HEAD_DIM, HEAD_DIM), lambda l, j: (0, 0)),
        ],
        out_specs=pl.BlockSpec((1, rows, width), lambda l, j: (l, j, 0)),
        out_shape=jax.ShapeDtypeStruct((depth, d, width), BF16),
        compiler_params=_params("arbitrary", "arbitrary"),
        name="prep_w_in",
    )(w_in, perm_matrix)


def _in_proj_kernel(x_ref, mod_ref, gpre_ref, w_ref, qn_ref, kn_ref, cos_ref, sin_ref, m_ref, *rest,
                    rope, n_cast):
    cast_src, (q_ref, k_ref, v_ref, ab_ref), cast_dst = rest[:n_cast], rest[n_cast:n_cast + 4], rest[n_cast + 4:]
    _run_side_casts(cast_src, cast_dst)
    mod = mod_ref[0]
    h = _rms(x_ref[...], (1.0 + mod[1:2]) * gpre_ref[...]) + mod[0:1]
    p = jnp.dot(h.astype(BF16), w_ref[...], preferred_element_type=F32)

    def head(col, gain, post_scale):
        y = _rms(p[:, col:col + HEAD_DIM], gain)
        if rope:
            y = y * cos_ref[...] + pltpu.roll(y, HEAD_DIM // 2, axis=1) * sin_ref[...]
        if post_scale != 1.0:
            y = y * post_scale
        return y.astype(BF16)

    for i in range(N_Q_HEADS):
        q_ref[:, i * HEAD_DIM:(i + 1) * HEAD_DIM] = head(i * HEAD_DIM, qn_ref[...], Q_SCALE)
    for i in range(N_KV_HEADS):
        k_ref[:, i * HEAD_DIM:(i + 1) * HEAD_DIM] = head(Q_END + i * HEAD_DIM, kn_ref[...], 1.0)
    v_ref[...] = p[:, K_END:V_END].astype(BF16)
    for g in range(N_FOURIER_GROUPS):
        lo = V_END + g * FOURIER_GROUP
        ab = jnp.dot(p[:, lo:lo + FOURIER_GROUP].astype(BF16), m_ref[g], preferred_element_type=F32)
        ab_ref[:, g * FOURIER_GROUP:(g + 1) * FOURIER_GROUP] = ab[:, :FOURIER_GROUP].astype(BF16)
        ab_ref[:, FOURIER_WIDTH + g * FOURIER_GROUP:FOURIER_WIDTH + (g + 1) * FOURIER_GROUP] = (
            ab[:, FOURIER_GROUP:].astype(BF16))


def _in_proj(x2d, mod, g_pre, w_in, qn, kn, cos_t, sin_t, m_fold, *, layer, mod_row, seq, rope, tm,
             cast=()):
    t, d = x2d.shape
    per_seq = seq // tm
    rope_map = (lambda i: (i % per_seq, 0)) if rope else (lambda i: (0, 0))
    cast_in, cast_out, cast_shapes = _side_casts(cast, layer, t // tm, lambda i: i)
    return pl.pallas_call(
        functools.partial(_in_proj_kernel, rope=rope, n_cast=len(cast)),
        grid=(t // tm,),
        in_specs=[
            pl.BlockSpec((tm, d), lambda i: (i, 0)),
            _mod_spec(d, mod_row),
            _layer_spec((1, d), layer),
            _layer_spec((d, IN_WIDTH), layer),
            _layer_spec((1, HEAD_DIM), layer),
            _layer_spec((1, HEAD_DIM), layer),
            pl.BlockSpec((tm, HEAD_DIM), rope_map),
            pl.BlockSpec((tm, HEAD_DIM), rope_map),
            _layer_spec((N_FOURIER_GROUPS, FOURIER_GROUP, 2 * FOURIER_GROUP), layer),
        ] + cast_in,
        out_specs=[
            pl.BlockSpec((tm, ATTN_WIDTH), lambda i: (i, 0)),
            pl.BlockSpec((tm, KV_WIDTH), lambda i: (i, 0)),
            pl.BlockSpec((tm, KV_WIDTH), lambda i: (i, 0)),
            pl.BlockSpec((tm, 2 * FOURIER_WIDTH), lambda i: (i, 0)),
        ] + cast_out,
        out_shape=[
            jax.ShapeDtypeStruct((t, ATTN_WIDTH), BF16),
            jax.ShapeDtypeStruct((t, KV_WIDTH), BF16),
            jax.ShapeDtypeStruct((t, KV_WIDTH), BF16),
            jax.ShapeDtypeStruct((t, 2 * FOURIER_WIDTH), BF16),
        ] + cast_shapes,
        compiler_params=_params("arbitrary"),
        name="in_proj",
    )(x2d, mod, g_pre, w_in, qn, kn, cos_t, sin_t, m_fold, *cast)


def _ctx_kv_kernel(x_ref, mod_ref, gpre_ref, w_ref, kn_ref, k_ref, v_ref):
    mod = mod_ref[0]
    h = _rms(x_ref[...], (1.0 + mod[1:2]) * gpre_ref[...]) + mod[0:1]
    p = jnp.dot(h.astype(BF16), w_ref[...], preferred_element_type=F32)
    for i in range(N_KV_HEADS):
        cols = slice(i * HEAD_DIM, (i + 1) * HEAD_DIM)
        k_ref[:, cols] = _rms(p[:, cols], kn_ref[...]).astype(BF16)
    v_ref[...] = p[:, KV_WIDTH:].astype(BF16)


def _ctx_kv(x2d, mod, g_pre, w_in, kn, *, layer, mod_row, tm):
    t, d = x2d.shape
    assert Q_END % (2 * KV_WIDTH) == 0
    return pl.pallas_call(
        _ctx_kv_kernel,
        grid=(t // tm,),
        in_specs=[
            pl.BlockSpec((tm, d), lambda i: (i, 0)),
            _mod_spec(d, mod_row),
            _layer_spec((1, d), layer),
            pl.BlockSpec((None, d, 2 * KV_WIDTH), lambda i: (layer, 0, Q_END // (2 * KV_WIDTH)),
                         pipeline_mode=pl.Buffered(1)),
            _layer_spec((1, HEAD_DIM), layer),
        ],
        out_specs=[pl.BlockSpec((tm, KV_WIDTH), lambda i: (i, 0))] * 2,
        out_shape=[jax.ShapeDtypeStruct((t, KV_WIDTH), BF16)] * 2,
        compiler_params=_params("arbitrary"),
        name="ctx_kv",
    )(x2d, mod, g_pre, w_in, kn)


def _attn_kernel(*refs, n_seg, n_cast):
    q_ref = refs[0]
    kv_refs = refs[1:1 + 2 * n_seg]
    g_ref = refs[1 + 2 * n_seg]
    rest = refs[2 + 2 * n_seg:]
    cast_src, o_ref, cast_dst, acc_ref = rest[:n_cast], rest[n_cast], rest[n_cast + 1:-1], rest[-1]
    _run_side_casts(cast_src, cast_dst)

    chunks = []
    for i in range(n_seg):
        total = kv_refs[2 * i].shape[0]
        for lo in range(0, total, ATTN_KEY_CHUNK):
            chunks.append((kv_refs[2 * i], kv_refs[2 * i + 1], lo, min(ATTN_KEY_CHUNK, total - lo)))

    def chunk_scores(h, c):
        k_ref, _, lo, size = chunks[c]
        kv = (h // Q_PER_KV) * HEAD_DIM
        return lax.dot_general(k_ref[lo:lo + size, kv:kv + HEAD_DIM], q_ref[:, h * HEAD_DIM:(h + 1) * HEAD_DIM],
                               (((1,), (1,)), ((), ())), preferred_element_type=F32)

    def chunk_output(h, c, p):
        _, v_ref, lo, size = chunks[c]
        kv = (h // Q_PER_KV) * HEAD_DIM
        return lax.dot_general(v_ref[lo:lo + size, kv:kv + HEAD_DIM], p,
                               (((0,), (0,)), ((), ())), preferred_element_type=F32)

    def col_max(m, s):
        cm = jnp.max(s, axis=0, keepdims=True)
        return cm if m is None else jnp.maximum(m, cm)

    scores, m = [], None
    for c in range(len(chunks)):
        scores.append(chunk_scores(0, c))
        m = col_max(m, scores[-1])
    for h in range(N_Q_HEADS):
        next_scores, next_m = [], None
        denom, out_t = None, None
        for c in range(len(chunks)):
            if h + 1 < N_Q_HEADS:
                next_scores.append(chunk_scores(h + 1, c))
                next_m = col_max(next_m, next_scores[-1])
            p = jnp.exp2(scores[c] - m)
            ps = jnp.sum(p, axis=0, keepdims=True)
            po = chunk_output(h, c, p.astype(BF16))
            denom = ps if denom is None else denom + ps
            out_t = po if out_t is None else out_t + po
        acc_ref[:, h * HEAD_DIM:(h + 1) * HEAD_DIM] = (out_t / denom).T
        scores, m = next_scores, next_m
    o_ref[...] = _rms(acc_ref[...], g_ref[...]).astype(BF16)


def _attention(q, segments, g_attn, *, layer, q_seq, tq, cast=()):
    t = q.shape[0]
    per_seq = q_seq // tq
    in_specs = [pl.BlockSpec((tq, ATTN_WIDTH), lambda b, j: (b * per_seq + j, 0))]
    args = [q]
    for k, v, rows in segments:
        in_specs += [pl.BlockSpec((rows, KV_WIDTH), lambda b, j: (b, 0))] * 2
        args += [k, v]
    in_specs.append(_layer_spec((1, ATTN_WIDTH), layer))
    args.append(g_attn)
    cast_in, cast_out, cast_shapes = _side_casts(cast, layer, t // tq, lambda b, j: b * per_seq + j)
    return pl.pallas_call(
        functools.partial(_attn_kernel, n_seg=len(segments), n_cast=len(cast)),
        grid=(t // q_seq, per_seq),
        in_specs=in_specs + cast_in,
        out_specs=[pl.BlockSpec((tq, ATTN_WIDTH), lambda b, j: (b * per_seq + j, 0))] + cast_out,
        out_shape=[jax.ShapeDtypeStruct((t, ATTN_WIDTH), BF16)] + cast_shapes,
        scratch_shapes=[pltpu.VMEM((tq, ATTN_WIDTH), F32)],
        compiler_params=_params("arbitrary", "arbitrary"),
        name="attention",
    )(*args, *cast)


def _pos_dft_kernel(c_ref, s_ref, a_ref, b_ref, g_ref, o_ref):
    y = (jnp.dot(c_ref[...], a_ref[...], preferred_element_type=F32)
         - jnp.dot(s_ref[...], b_ref[...], preferred_element_type=F32))
    o_ref[...] = _rms(y, g_ref[...]).astype(BF16)


def _pos_dft(cos_m, sin_m, ab, g_four, *, layer, tm):
    seq = cos_m.shape[0]
    t = ab.shape[0]
    per_seq = seq // tm
    return pl.pallas_call(
        _pos_dft_kernel,
        grid=(t // seq, per_seq),
        in_specs=[
            pl.BlockSpec((tm, seq), lambda b, j: (j, 0)),
            pl.BlockSpec((tm, seq), lambda b, j: (j, 0)),
            pl.BlockSpec((seq, FOURIER_WIDTH), lambda b, j: (b, 0)),
            pl.BlockSpec((seq, FOURIER_WIDTH), lambda b, j: (b, 1)),
            _layer_spec((1, FOURIER_WIDTH), layer),
        ],
        out_specs=pl.BlockSpec((tm, FOURIER_WIDTH), lambda b, j: (b * per_seq + j, 0)),
        out_shape=jax.ShapeDtypeStruct((t, FOURIER_WIDTH), BF16),
        compiler_params=_params("arbitrary", "arbitrary"),
        name="pos_dft",
    )(cos_m, sin_m, ab, ab, g_four)


FLIP_BLOCK = 256
DFT_PAD_ROWS = 16


def _pos_dft_sym_kernel(alo_ref, ahi_ref, blo_ref, bhi_ref, c_ref, s_ref, q_ref, sgn_ref, g_ref,
                        o_ref, e_ref, d_ref, h_ref):
    half = alo_ref.shape[0]
    blk = q_ref.shape[0]
    nblk = half // blk
    qm = q_ref[...]
    is_row0 = lax.broadcasted_iota(jnp.int32, (blk, 1), 0) == 0

    def flipped_block(src_ref, bt, first_row):
        lo = (nblk - 1 - bt) * blk
        y = jnp.dot(qm, src_ref[lo:lo + blk, :], preferred_element_type=F32)
        if bt > 0:
            first_row = src_ref[lo + blk:lo + blk + 1, :].astype(F32)
        return jnp.where(is_row0, first_row, y)

    for bt in range(nblk):
        r = slice(bt * blk, (bt + 1) * blk)
        e_ref[r, :] = (alo_ref[r, :].astype(F32) + flipped_block(ahi_ref, bt, 0.0)).astype(BF16)
        d_ref[r, :] = (blo_ref[r, :].astype(F32) - flipped_block(bhi_ref, bt, 0.0)).astype(BF16)

    a_mid = ahi_ref[0:1, :].astype(F32)
    g = g_ref[...]
    mid_row = None
    for kb in range(nblk):
        r = slice(kb * blk, (kb + 1) * blk)
        rows = blk + (DFT_PAD_ROWS if kb == nblk - 1 else 0)
        cr = slice(kb * blk, kb * blk + rows)
        yc = jnp.dot(c_ref[cr, :], e_ref[...], preferred_element_type=F32) + sgn_ref[cr, :] * a_mid
        ys = jnp.dot(s_ref[r, :], d_ref[...], preferred_element_type=F32)
        o_ref[r, :] = _rms(yc[:blk] - ys, g).astype(BF16)
        h_ref[r, :] = _rms(yc[:blk] + ys, g).astype(BF16)
        if kb == nblk - 1:
            mid_row = _rms(yc[blk:blk + 1], g)

    for bs in range(nblk):
        r = slice(half + bs * blk, half + (bs + 1) * blk)
        o_ref[r, :] = flipped_block(h_ref, bs, mid_row).astype(BF16)


def _pos_dft_sym(tables, ab, g_four, *, layer):
    cos_m, sin_m, flip_m, sgn = tables
    half = sin_m.shape[0]
    t = ab.shape[0]
    blocks = lambda row, col: pl.BlockSpec((half, FOURIER_WIDTH), lambda b: (2 * b + row, col))
    return pl.pallas_call(
        _pos_dft_sym_kernel,
        grid=(t // (2 * half),),
        in_specs=[
            blocks(0, 0), blocks(1, 0), blocks(0, 1), blocks(1, 1),
            _const_spec(cos_m.shape),
            _const_spec(sin_m.shape),
            _const_spec(flip_m.shape),
            _const_spec(sgn.shape),
            _layer_spec((1, FOURIER_WIDTH), layer),
        ],
        out_specs=pl.BlockSpec((2 * half, FOURIER_WIDTH), lambda b: (b, 0)),
        out_shape=jax.ShapeDtypeStruct((t, FOURIER_WIDTH), BF16),
        scratch_shapes=[pltpu.VMEM((half, FOURIER_WIDTH), BF16)] * 3,
        compiler_params=_params("arbitrary"),
        name="pos_dft_sym",
    )(ab, ab, ab, ab, cos_m, sin_m, flip_m, sgn, g_four)


def _out_proj_kernel(a_ref, f_ref, w_ref, x_ref, mod_ref, gpost_ref, gpre_ref, *rest, next_mod):
    if next_mod:
        (c_ref, wm_ref, bm_ref), rest = rest[:3], rest[3:]
        _mod_block(c_ref, wm_ref, bm_ref, rest[2])
    xo_ref, h_ref = rest[:2]
    mod = mod_ref[0]
    gate_g = mod[2:3] * gpost_ref[...]
    scale_g = (1.0 + mod[4:5]) * gpre_ref[...]
    rows = x_ref.shape[0] // OUT_PROJ_SUB_BLOCKS

    def matmuls(s):
        r = slice(s * rows, (s + 1) * rows)
        return (jnp.dot(a_ref[r, :], w_ref[:ATTN_WIDTH, :], preferred_element_type=F32)
                + jnp.dot(f_ref[r, :], w_ref[ATTN_WIDTH:, :], preferred_element_type=F32))

    pending = matmuls(0)
    for s in range(OUT_PROJ_SUB_BLOCKS):
        r = slice(s * rows, (s + 1) * rows)
        mix = pending
        if s + 1 < OUT_PROJ_SUB_BLOCKS:
            pending = matmuls(s + 1)
        xn = x_ref[r, :] + _rms(mix, gate_g)
        xo_ref[r, :] = xn
        h_ref[r, :] = (_rms(xn, scale_g) + mod[3:4]).astype(BF16)


def _out_proj(a, f, w_out, x2d, mod, g_post, g_pre_ffn, *, layer, mod_row, tm, next_mod=None):
    t, d = x2d.shape
    n_steps = t // tm
    in_specs = [
        pl.BlockSpec((tm, ATTN_WIDTH), lambda i: (i, 0)),
        pl.BlockSpec((tm, FOURIER_WIDTH), lambda i: (i, 0)),
        _const_spec((ATTN_WIDTH + FOURIER_WIDTH, d)),
        pl.BlockSpec((tm, d), lambda i: (i, 0)),
        _mod_spec(d, mod_row),
        _layer_spec((1, d), layer),
        _layer_spec((1, d), layer),
    ]
    out_specs = [pl.BlockSpec((tm, d), lambda i: (i, 0)), pl.BlockSpec((tm, d), lambda i: (i, 0))]
    out_shape = [jax.ShapeDtypeStruct((t, d), F32), jax.ShapeDtypeStruct((t, d), BF16)]
    args = [a, f, w_out, x2d, mod, g_post, g_pre_ffn]
    if next_mod is not None:
        mod_in, mod_out, mod_shape = _mod_block_specs(d, next_mod[1].shape[-1], layer + 1, n_steps)
        in_specs += mod_in
        out_specs.append(mod_out)
        out_shape.append(mod_shape)
        args += list(next_mod)
    return pl.pallas_call(
        functools.partial(_out_proj_kernel, next_mod=next_mod is not None),
        grid=(n_steps,),
        in_specs=in_specs,
        out_specs=out_specs,
        out_shape=out_shape,
        compiler_params=_params("arbitrary"),
        name="out_proj",
    )(*args)


CONV_HALO = 8


def _ffn_up_kernel(h_ref, wg_ref, wv_ref, cw_ref, cb_ref, o_ref, *, seg_len):
    rows, tn = o_ref.shape
    chunk = rows // FFN_UP_ROW_CHUNKS
    pos = lax.broadcasted_iota(jnp.int32, (rows, 1), 0) % seg_len
    first, final = pos == 0, pos == seg_len - 1
    cw = cw_ref[...]
    cb = cb_ref[...]

    def span(s):
        return max(s * chunk - CONV_HALO, 0), min((s + 1) * chunk + CONV_HALO, rows)

    def matmuls(s):
        lo, hi = span(s)
        h = h_ref[lo:hi, :]
        return (jnp.dot(h, wg_ref[...], preferred_element_type=F32),
                jnp.dot(h, wv_ref[...], preferred_element_type=F32))

    def epilogue(s, gate, val):
        lo, hi = span(s)
        prev = jnp.where(first[lo:hi], 0.0, pltpu.roll(gate, 1, axis=0))
        nxt = jnp.where(final[lo:hi], 0.0, pltpu.roll(gate, hi - lo - 1, axis=0))
        c = prev * cw[0:1] + gate * cw[1:2] + nxt * cw[2:3] + cb
        act = 0.5 * c * (1.0 + jnp.tanh(math.sqrt(2.0 / math.pi) * (c + 0.044715 * (c * c * c))))
        own = slice(s * chunk - lo, (s + 1) * chunk - lo)
        o_ref[s * chunk:(s + 1) * chunk, :] = (act * val)[own].astype(BF16)

    pending = matmuls(0)
    for s in range(FFN_UP_ROW_CHUNKS):
        ready = pending
        if s + 1 < FFN_UP_ROW_CHUNKS:
            pending = matmuls(s + 1)
        epilogue(s, *ready)


def _ffn_up(h, w_up, conv_w, conv_b, *, layer, seg_len, tm, tn):
    t, d = h.shape
    ffn = conv_w.shape[-1]
    n_col = ffn // tn
    return pl.pallas_call(
        functools.partial(_ffn_up_kernel, seg_len=seg_len),
        grid=(t // tm, n_col),
        in_specs=[
            pl.BlockSpec((tm, d), lambda i, j: (i, 0)),
            pl.BlockSpec((d, tn), lambda i, j: (0, j)),
            pl.BlockSpec((d, tn), lambda i, j: (0, j + n_col)),
            pl.BlockSpec((None, 3, tn), lambda i, j: (layer, 0, j)),
            pl.BlockSpec((None, 1, tn), lambda i, j: (layer, 0, j)),
        ],
        out_specs=pl.BlockSpec((tm, tn), lambda i, j: (i, j)),
        out_shape=jax.ShapeDtypeStruct((t, ffn), BF16),
        compiler_params=_params("arbitrary", "arbitrary"),
        name="ffn_up",
    )(h, w_up, w_up, conv_w, conv_b)


LANES = 128


def _ffn_down_skew_kernel(u_ref, w_ref, x_ref, mod_ref, gpost_ref, xo_ref, y_ref, inv_ref, ssq_ref):
    i, j = pl.program_id(0), pl.program_id(1)
    n_tiles = pl.num_programs(0) - 1
    n_col, _, tn = y_ref.shape

    @pl.when((i == 0) & (j == 0))
    def _():
        y_ref[...] = jnp.zeros_like(y_ref)
        inv_ref[...] = jnp.zeros_like(inv_ref)

    @pl.when(j == 0)
    def _():
        ssq_ref[...] = jnp.zeros_like(ssq_ref)

    def finish_previous():
        gate_g = mod_ref[0][N_MOD - 1:N_MOD] * gpost_ref[...]
        xo_ref[...] = x_ref[...] + (y_ref[j] * inv_ref[...]) * gate_g

    @pl.when(i < n_tiles)
    def _():
        finish_previous()
        y = jnp.dot(u_ref[...], w_ref[...], preferred_element_type=F32)
        y_ref[j] = y
        sq = y * y
        ssq_ref[...] += functools.reduce(
            jnp.add, [sq[:, c * LANES:(c + 1) * LANES] for c in range(tn // LANES)])

    @pl.when(i == n_tiles)
    def _():
        finish_previous()

    @pl.when((j == n_col - 1) & (i < n_tiles))
    def _():
        ssq = jnp.sum(ssq_ref[...], axis=-1, keepdims=True)
        inv_ref[...] = lax.rsqrt(ssq / (n_col * tn) + EPS)


def _ffn_down_skew(u, w_down, x2d, mod, g_post, *, layer, mod_row, tm, tn):
    t, d = x2d.shape
    ffn = u.shape[1]
    n_tiles = t // tm
    n_col = d // tn
    prev = lambda i: jnp.maximum(i - 1, 0)
    col = lambda i, j: jnp.where(i > 0, j, 0)
    return pl.pallas_call(
        _ffn_down_skew_kernel,
        grid=(n_tiles + 1, n_col),
        in_specs=[
            pl.BlockSpec((tm, ffn), lambda i, j: (jnp.minimum(i, n_tiles - 1), 0)),
            pl.BlockSpec((ffn, tn), lambda i, j: (0, jnp.where(i < n_tiles, j, n_col - 1))),
            pl.BlockSpec((tm, tn), lambda i, j: (prev(i), col(i, j))),
            pl.BlockSpec((1, N_MOD, tn), lambda i, j: (mod_row(prev(i)), 0, col(i, j))),
            pl.BlockSpec((None, 1, tn), lambda i, j: (layer, 0, col(i, j))),
        ],
        out_specs=pl.BlockSpec((tm, tn), lambda i, j: (prev(i), col(i, j))),
        out_shape=jax.ShapeDtypeStruct((t, d), F32),
        scratch_shapes=[pltpu.VMEM((n_col, tm, tn), F32), pltpu.VMEM((tm, 1), F32),
                        pltpu.VMEM((tm, LANES), F32)],
        compiler_params=_params("arbitrary", "arbitrary"),
        name="ffn_down_skew",
    )(u, w_down, x2d, mod, g_post)


def _rope_tables(n):
    pos = np.arange(n)
    n_pairs_axis = HEAD_DIM // 4
    freqs = ROPE_THETA ** (-np.arange(n_pairs_axis, dtype=np.float64) / n_pairs_axis)
    ang = np.concatenate([(pos // GRID_W)[:, None] * freqs, (pos % GRID_W)[:, None] * freqs], axis=-1)
    c, s = np.cos(ang), np.sin(ang)
    return (np.concatenate([c, c], axis=-1).astype(np.float32),
            np.concatenate([-s, s], axis=-1).astype(np.float32))


def _dft_matrices(n):
    idx = np.arange(n)
    ang = ((idx[:, None] * idx[None, :]) % n) * (2.0 * math.pi / n)
    return np.cos(ang) * n ** -0.5, np.sin(ang) * n ** -0.5


def _sym_dft_tables(n):
    half = n // 2
    k = np.arange(half + DFT_PAD_ROWS)[:, None]
    ang = ((k * np.arange(half)[None, :]) % n) * (2.0 * math.pi / n)
    scale = n ** -0.5
    valid = k <= half
    cos_m = jnp.asarray(np.where(valid, np.cos(ang) * scale, 0.0), F32).astype(BF16)
    sin_m = jnp.asarray(np.sin(ang[:half]) * scale, F32).astype(BF16)
    sgn = np.where(valid, (1 - 2 * (k % 2)) * scale, 0.0).astype(np.float32)
    i = np.arange(FLIP_BLOCK)
    flip_m = (i[:, None] + i[None, :] == FLIP_BLOCK).astype(BF16)
    return cos_m, sin_m, flip_m, sgn


def _deinterleave(v):
    return jnp.concatenate([v[..., 0::2], v[..., 1::2]], axis=-1)


def kernel(x, c, ctx, c_ctx, w_mod, b_mod, g_pre_mix, g_post_mix, g_pre_ffn, g_post_ffn,
           w_in, q_norm, k_norm, w_four, g_attn_out, g_four_out, w_out,
           w_up, conv_w, conv_b, w_down):
    b, n, d = x.shape
    ctx_len = ctx.shape[1]
    depth = w_mod.shape[0]

    perm_matrix = _deinterleave(np.eye(HEAD_DIM, dtype=BF16))
    w_in_b = _prep_w_in(w_in, perm_matrix)
    qn = _deinterleave(q_norm).reshape(depth, 1, HEAD_DIM)
    kn = _deinterleave(k_norm).reshape(depth, 1, HEAD_DIM)
    cos_t, sin_t = _rope_tables(n)
    dft_tables = _sym_dft_tables(n)
    cos_c, sin_c = (jnp.asarray(m, F32).astype(BF16) for m in _dft_matrices(ctx_len))
    cs_ch = np.stack(_dft_matrices(FOURIER_GROUP)).astype(np.float32)
    m_fold = _fold_channel_dft(cs_ch, w_four)
    rows3 = lambda v: v.reshape(depth, 1, -1)
    g_pre_mix, g_post_mix, g_pre_ffn, g_post_ffn, g_attn_out, g_four_out, conv_b, b_mod = map(
        rows3, (g_pre_mix, g_post_mix, g_pre_ffn, g_post_ffn, g_attn_out, g_four_out, conv_b, b_mod))

    cc = jnp.zeros((MOD_ROWS, d), F32).at[:b].set(c).at[b].set(c_ctx)
    as_table = lambda m: m.reshape(MOD_ROWS, N_MOD, d)
    mod = as_table(_mod_proj(cc, w_mod, b_mod, layer=0))

    xl = x.reshape(b * n, d)
    xc = ctx.reshape(b * ctx_len, d)
    tm = ROW_TILE
    lat_row = lambda i: i // (n // tm)
    ctx_row = lambda i: b

    for i in range(depth):
        last = i == depth - 1
        q_l, k_l, v_l, ab_l, w_up_b = _in_proj(
            xl, mod, g_pre_mix, w_in_b, qn, kn, cos_t, sin_t, m_fold,
            layer=i, mod_row=lat_row, seq=n, rope=True, tm=tm, cast=(w_up,))
        if last:
            k_c, v_c = _ctx_kv(xc, mod, g_pre_mix, w_in_b, kn, layer=i, mod_row=ctx_row, tm=tm)
        else:
            q_c, k_c, v_c, ab_c = _in_proj(
                xc, mod, g_pre_mix, w_in_b, qn, kn, cos_t, sin_t, m_fold,
                layer=i, mod_row=ctx_row, seq=ctx_len, rope=False, tm=ctx_len)

        attn_l, w_out_b, w_down_b = _attention(
            q_l, [(k_l, v_l, n), (k_c, v_c, ctx_len)], g_attn_out,
            layer=i, q_seq=n, tq=ATTN_Q_TILE, cast=(w_out, w_down))
        four_l = _pos_dft_sym(dft_tables, ab_l, g_four_out, layer=i)
        xl, h_l, *next_mod = _out_proj(
            attn_l, four_l, w_out_b, xl, mod, g_post_mix, g_pre_ffn, layer=i, mod_row=lat_row, tm=tm,
            next_mod=None if last else (cc, w_mod, b_mod))
        u_l = _ffn_up(h_l, w_up_b, conv_w, conv_b, layer=i, seg_len=n, tm=n, tn=FFN_COL_TILE)
        xl = _ffn_down_skew(u_l, w_down_b, xl, mod, g_post_ffn, layer=i,
                            mod_row=lambda r: r // (n // FFN_DOWN_ROW_TILE),
                            tm=FFN_DOWN_ROW_TILE, tn=FFN_COL_TILE)

        if not last:
            attn_c, = _attention(q_c, [(k_c, v_c, ctx_len)], g_attn_out,
                                 layer=i, q_seq=ctx_len, tq=ctx_len)
            four_c = _pos_dft(cos_c, sin_c, ab_c, g_four_out, layer=i, tm=ctx_len)
            xc, h_c = _out_proj(attn_c, four_c, w_out_b, xc, mod, g_post_mix, g_pre_ffn,
                                layer=i, mod_row=ctx_row, tm=tm)
            u_c = _ffn_up(h_c, w_up_b, conv_w, conv_b, layer=i, seg_len=ctx_len,
                          tm=b * ctx_len, tn=FFN_COL_TILE)
            xc = _ffn_down_skew(u_c, w_down_b, xc, mod, g_post_ffn, layer=i, mod_row=ctx_row,
                                tm=FFN_DOWN_ROW_TILE, tn=FFN_COL_TILE)
            mod = as_table(next_mod[0])

    return xl.reshape(b, n, d)
```

```python
import functools
import math

import jax
import jax.numpy as jnp
import numpy as np
from jax import lax
from jax.experimental import pallas as pl
from jax.experimental.pallas import tpu as pltpu

GRID_W = 64
HEAD_DIM = 128
N_Q_HEADS = 8
N_KV_HEADS = 2
Q_PER_KV = N_Q_HEADS // N_KV_HEADS
FOURIER_GROUP = 128
N_FOURIER_GROUPS = 8
ATTN_WIDTH = N_Q_HEADS * HEAD_DIM
KV_WIDTH = N_KV_HEADS * HEAD_DIM
FOURIER_WIDTH = N_FOURIER_GROUPS * FOURIER_GROUP
Q_END = ATTN_WIDTH
K_END = Q_END + KV_WIDTH
V_END = K_END + KV_WIDTH
IN_WIDTH = V_END + FOURIER_WIDTH
ROPE_THETA = 10000.0
EPS = 1e-6
N_MOD = 6
Q_SCALE = HEAD_DIM ** -0.5 * math.log2(math.e)
MOD_ROWS = 16

VMEM_LIMIT_BYTES = 56 * 1024 * 1024

ROW_TILE = 512
ATTN_Q_TILE = 1024
ATTN_KEY_CHUNK = 512
FFN_COL_TILE = 512
FFN_DOWN_ROW_TILE = 1024
FFN_UP_ROW_CHUNKS = 2
OUT_PROJ_SUB_BLOCKS = 4
MOD_COL_TILE = 1024
PREP_ROW_TILE = 256

F32 = jnp.float32
BF16 = jnp.bfloat16


def _params(*semantics):
    return pltpu.CompilerParams(dimension_semantics=semantics, vmem_limit_bytes=VMEM_LIMIT_BYTES)


def _rms(x, g):
    ms = jnp.mean(x * x, axis=-1, keepdims=True)
    return x * lax.rsqrt(ms + EPS) * g


def _const_spec(shape):
    return pl.BlockSpec(shape, lambda *_: (0,) * len(shape), pipeline_mode=pl.Buffered(1))


def _layer_spec(shape, layer):
    zeros = (0,) * len(shape)
    return pl.BlockSpec((None,) + shape, lambda *_: (layer,) + zeros, pipeline_mode=pl.Buffered(1))


def _side_casts(sources, layer, n_steps, step_of):
    in_specs, out_specs, out_shapes = [], [], []
    for src in sources:
        _, r, c = src.shape
        rows = r // n_steps
        in_specs.append(pl.BlockSpec((None, rows, c), lambda *g: (layer, step_of(*g), 0)))
        out_specs.append(pl.BlockSpec((rows, c), lambda *g: (step_of(*g), 0)))
        out_shapes.append(jax.ShapeDtypeStruct((r, c), BF16))
    return in_specs, out_specs, out_shapes


def _run_side_casts(src_refs, dst_refs):
    for src, dst in zip(src_refs, dst_refs, strict=True):
        dst[...] = src[...].astype(BF16)


def _mod_spec(d, row_of):
    return pl.BlockSpec((1, N_MOD, d), lambda i, *_: (row_of(i), 0, 0))


def _mod_block(c_ref, w_ref, b_ref, o_ref):
    c = c_ref[...]
    s = (c * jax.nn.sigmoid(c)).astype(BF16)
    o_ref[...] = jnp.dot(s, w_ref[...].astype(BF16), preferred_element_type=F32) + b_ref[...]


def _mod_block_specs(d, width, layer, n_steps):
    tn = width // n_steps
    in_specs = [
        pl.BlockSpec((MOD_ROWS, d), lambda i, *_: (0, 0)),
        pl.BlockSpec((None, d, tn), lambda i, *_: (layer, 0, i)),
        pl.BlockSpec((None, 1, tn), lambda i, *_: (layer, 0, i)),
    ]
    out_spec = pl.BlockSpec((MOD_ROWS, tn), lambda i, *_: (0, i))
    return in_specs, out_spec, jax.ShapeDtypeStruct((MOD_ROWS, width), F32)


def _mod_proj(cc, w_mod, b_mod, *, layer):
    _, d, width = w_mod.shape
    n_steps = width // MOD_COL_TILE
    in_specs, out_spec, out_shape = _mod_block_specs(d, width, layer, n_steps)
    return pl.pallas_call(
        _mod_block,
        grid=(n_steps,),
        in_specs=in_specs,
        out_specs=out_spec,
        out_shape=out_shape,
        compiler_params=_params("arbitrary"),
        name="mod_proj",
    )(cc, w_mod, b_mod)


def _fold_kernel(cs_ref, w_ref, o_ref):
    for g in range(w_ref.shape[1]):
        w = w_ref[0, g]
        o_ref[0, g, :, :FOURIER_GROUP] = jnp.dot(
            cs_ref[0], w, preferred_element_type=F32, precision=lax.Precision.HIGHEST).astype(BF16)
        o_ref[0, g, :, FOURIER_GROUP:] = jnp.dot(
            cs_ref[1], w, preferred_element_type=F32, precision=lax.Precision.HIGHEST).astype(BF16)


def _fold_channel_dft(cs, w_four):
    depth, groups, c, _ = w_four.shape
    return pl.pallas_call(
        _fold_kernel,
        grid=(depth,),
        in_specs=[
            pl.BlockSpec((2, c, c), lambda l: (0, 0, 0)),
            pl.BlockSpec((1, groups, c, c), lambda l: (l, 0, 0, 0)),
        ],
        out_specs=pl.BlockSpec((1, groups, c, 2 * c), lambda l: (l, 0, 0, 0)),
        out_shape=jax.ShapeDtypeStruct((depth, groups, c, 2 * c), BF16),
        compiler_params=_params("arbitrary"),
        name="fold_channel_dft",
    )(cs, w_four)


def _prep_w_in_kernel(w_ref, p_ref, o_ref):
    for head in range(N_Q_HEADS + N_KV_HEADS):
        cols = slice(head * HEAD_DIM, (head + 1) * HEAD_DIM)
        o_ref[0, :, cols] = jnp.dot(w_ref[0, :, cols].astype(BF16), p_ref[...],
                                    preferred_element_type=F32).astype(BF16)
    o_ref[0, :, K_END:] = w_ref[0, :, K_END:].astype(BF16)


def _prep_w_in(w_in, perm_matrix):
    depth, d, width = w_in.shape
    rows = PREP_ROW_TILE
    return pl.pallas_call(
        _prep_w_in_kernel,
        grid=(depth, d // rows),
        in_specs=[
            pl.BlockSpec((1, rows, width), lambda l, j: (l, j, 0)),
            pl.BlockSpec((HEAD_DIM, HEAD_DIM), lambda l, j: (0, 0)),
        ],
        out_specs=pl.BlockSpec((1, rows, width), lambda l, j: (l, j, 0)),
        out_shape=jax.ShapeDtypeStruct((depth, d, width), BF16),
        compiler_params=_params("arbitrary", "arbitrary"),
        name="prep_w_in",
    )(w_in, perm_matrix)


def _in_proj_kernel(x_ref, mod_ref, gpre_ref, w_ref, qn_ref, kn_ref, cos_ref, sin_ref, m_ref, *rest,
                    rope, n_cast):
    cast_src, (q_ref, k_ref, v_ref, ab_ref), cast_dst = rest[:n_cast], rest[n_cast:n_cast + 4], rest[n_cast + 4:]
    _run_side_casts(cast_src, cast_dst)
    mod = mod_ref[0]
    h = _rms(x_ref[...], (1.0 + mod[1:2]) * gpre_ref[...]) + mod[0:1]
    p = jnp.dot(h.astype(BF16), w_ref[...], preferred_element_type=F32)

    def head(col, gain, post_scale):
        y = _rms(p[:, col:col + HEAD_DIM], gain)
        if rope:
            y = y * cos_ref[...] + pltpu.roll(y, HEAD_DIM // 2, axis=1) * sin_ref[...]
        if post_scale != 1.0:
            y = y * post_scale
        return y.astype(BF16)

    for i in range(N_Q_HEADS):
        q_ref[:, i * HEAD_DIM:(i + 1) * HEAD_DIM] = head(i * HEAD_DIM, qn_ref[...], Q_SCALE)
    for i in range(N_KV_HEADS):
        k_ref[:, i * HEAD_DIM:(i + 1) * HEAD_DIM] = head(Q_END + i * HEAD_DIM, kn_ref[...], 1.0)
    v_ref[...] = p[:, K_END:V_END].astype(BF16)
    for g in range(N_FOURIER_GROUPS):
        lo = V_END + g * FOURIER_GROUP
        ab = jnp.dot(p[:, lo:lo + FOURIER_GROUP].astype(BF16), m_ref[g], preferred_element_type=F32)
        ab_ref[:, g * FOURIER_GROUP:(g + 1) * FOURIER_GROUP] = ab[:, :FOURIER_GROUP].astype(BF16)
        ab_ref[:, FOURIER_WIDTH + g * FOURIER_GROUP:FOURIER_WIDTH + (g + 1) * FOURIER_GROUP] = (
            ab[:, FOURIER_GROUP:].astype(BF16))


def _in_proj(x2d, mod, g_pre, w_in, qn, kn, cos_t, sin_t, m_fold, *, layer, mod_row, seq, rope, tm,
             cast=()):
    t, d = x2d.shape
    per_seq = seq // tm
    rope_map = (lambda i: (i % per_seq, 0)) if rope else (lambda i: (0, 0))
    cast_in, cast_out, cast_shapes = _side_casts(cast, layer, t // tm, lambda i: i)
    return pl.pallas_call(
        functools.partial(_in_proj_kernel, rope=rope, n_cast=len(cast)),
        grid=(t // tm,),
        in_specs=[
            pl.BlockSpec((tm, d), lambda i: (i, 0)),
            _mod_spec(d, mod_row),
            _layer_spec((1, d), layer),
            _layer_spec((d, IN_WIDTH), layer),
            _layer_spec((1, HEAD_DIM), layer),
            _layer_spec((1, HEAD_DIM), layer),
            pl.BlockSpec((tm, HEAD_DIM), rope_map),
            pl.BlockSpec((tm, HEAD_DIM), rope_map),
            _layer_spec((N_FOURIER_GROUPS, FOURIER_GROUP, 2 * FOURIER_GROUP), layer),
        ] + cast_in,
        out_specs=[
            pl.BlockSpec((tm, ATTN_WIDTH), lambda i: (i, 0)),
            pl.BlockSpec((tm, KV_WIDTH), lambda i: (i, 0)),
            pl.BlockSpec((tm, KV_WIDTH), lambda i: (i, 0)),
            pl.BlockSpec((tm, 2 * FOURIER_WIDTH), lambda i: (i, 0)),
        ] + cast_out,
        out_shape=[
            jax.ShapeDtypeStruct((t, ATTN_WIDTH), BF16),
            jax.ShapeDtypeStruct((t, KV_WIDTH), BF16),
            jax.ShapeDtypeStruct((t, KV_WIDTH), BF16),
            jax.ShapeDtypeStruct((t, 2 * FOURIER_WIDTH), BF16),
        ] + cast_shapes,
        compiler_params=_params("arbitrary"),
        name="in_proj",
    )(x2d, mod, g_pre, w_in, qn, kn, cos_t, sin_t, m_fold, *cast)


def _ctx_kv_kernel(x_ref, mod_ref, gpre_ref, w_ref, kn_ref, k_ref, v_ref):
    mod = mod_ref[0]
    h = _rms(x_ref[...], (1.0 + mod[1:2]) * gpre_ref[...]) + mod[0:1]
    p = jnp.dot(h.astype(BF16), w_ref[...], preferred_element_type=F32)
    for i in range(N_KV_HEADS):
        cols = slice(i * HEAD_DIM, (i + 1) * HEAD_DIM)
        k_ref[:, cols] = _rms(p[:, cols], kn_ref[...]).astype(BF16)
    v_ref[...] = p[:, KV_WIDTH:].astype(BF16)


def _ctx_kv(x2d, mod, g_pre, w_in, kn, *, layer, mod_row, tm):
    t, d = x2d.shape
    assert Q_END % (2 * KV_WIDTH) == 0
    return pl.pallas_call(
        _ctx_kv_kernel,
        grid=(t // tm,),
        in_specs=[
            pl.BlockSpec((tm, d), lambda i: (i, 0)),
            _mod_spec(d, mod_row),
            _layer_spec((1, d), layer),
            pl.BlockSpec((None, d, 2 * KV_WIDTH), lambda i: (layer, 0, Q_END // (2 * KV_WIDTH)),
                         pipeline_mode=pl.Buffered(1)),
            _layer_spec((1, HEAD_DIM), layer),
        ],
        out_specs=[pl.BlockSpec((tm, KV_WIDTH), lambda i: (i, 0))] * 2,
        out_shape=[jax.ShapeDtypeStruct((t, KV_WIDTH), BF16)] * 2,
        compiler_params=_params("arbitrary"),
        name="ctx_kv",
    )(x2d, mod, g_pre, w_in, kn)


def _attn_kernel(*refs, n_seg, n_cast):
    q_ref = refs[0]
    kv_refs = refs[1:1 + 2 * n_seg]
    g_ref = refs[1 + 2 * n_seg]
    rest = refs[2 + 2 * n_seg:]
    cast_src, o_ref, cast_dst, acc_ref = rest[:n_cast], rest[n_cast], rest[n_cast + 1:-1], rest[-1]
    _run_side_casts(cast_src, cast_dst)

    chunks = []
    for i in range(n_seg):
        total = kv_refs[2 * i].shape[0]
        for lo in range(0, total, ATTN_KEY_CHUNK):
            chunks.append((kv_refs[2 * i], kv_refs[2 * i + 1], lo, min(ATTN_KEY_CHUNK, total - lo)))

    def chunk_scores(h, c):
        k_ref, _, lo, size = chunks[c]
        kv = (h // Q_PER_KV) * HEAD_DIM
        return lax.dot_general(k_ref[lo:lo + size, kv:kv + HEAD_DIM], q_ref[:, h * HEAD_DIM:(h + 1) * HEAD_DIM],
                               (((1,), (1,)), ((), ())), preferred_element_type=F32)

    def chunk_output(h, c, p):
        _, v_ref, lo, size = chunks[c]
        kv = (h // Q_PER_KV) * HEAD_DIM
        return lax.dot_general(v_ref[lo:lo + size, kv:kv + HEAD_DIM], p,
                               (((0,), (0,)), ((), ())), preferred_element_type=F32)

    def col_max(m, s):
        cm = jnp.max(s, axis=0, keepdims=True)
        return cm if m is None else jnp.maximum(m, cm)

    scores, m = [], None
    for c in range(len(chunks)):
        scores.append(chunk_scores(0, c))
        m = col_max(m, scores[-1])
    for h in range(N_Q_HEADS):
        next_scores, next_m = [], None
        denom, out_t = None, None
        for c in range(len(chunks)):
            if h + 1 < N_Q_HEADS:
                next_scores.append(chunk_scores(h + 1, c))
                next_m = col_max(next_m, next_scores[-1])
            p = jnp.exp2(scores[c] - m)
            ps = jnp.sum(p, axis=0, keepdims=True)
            po = chunk_output(h, c, p.astype(BF16))
            denom = ps if denom is None else denom + ps
            out_t = po if out_t is None else out_t + po
        acc_ref[:, h * HEAD_DIM:(h + 1) * HEAD_DIM] = (out_t / denom).T
        scores, m = next_scores, next_m
    o_ref[...] = _rms(acc_ref[...], g_ref[...]).astype(BF16)


def _attention(q, segments, g_attn, *, layer, q_seq, tq, cast=()):
    t = q.shape[0]
    per_seq = q_seq // tq
    in_specs = [pl.BlockSpec((tq, ATTN_WIDTH), lambda b, j: (b * per_seq + j, 0))]
    args = [q]
    for k, v, rows in segments:
        in_specs += [pl.BlockSpec((rows, KV_WIDTH), lambda b, j: (b, 0))] * 2
        args += [k, v]
    in_specs.append(_layer_spec((1, ATTN_WIDTH), layer))
    args.append(g_attn)
    cast_in, cast_out, cast_shapes = _side_casts(cast, layer, t // tq, lambda b, j: b * per_seq + j)
    return pl.pallas_call(
        functools.partial(_attn_kernel, n_seg=len(segments), n_cast=len(cast)),
        grid=(t // q_seq, per_seq),
        in_specs=in_specs + cast_in,
        out_specs=[pl.BlockSpec((tq, ATTN_WIDTH), lambda b, j: (b * per_seq + j, 0))] + cast_out,
        out_shape=[jax.ShapeDtypeStruct((t, ATTN_WIDTH), BF16)] + cast_shapes,
        scratch_shapes=[pltpu.VMEM((tq, ATTN_WIDTH), F32)],
        compiler_params=_params("arbitrary", "arbitrary"),
        name="attention",
    )(*args, *cast)


def _pos_dft_kernel(c_ref, s_ref, a_ref, b_ref, g_ref, o_ref):
    y = (jnp.dot(c_ref[...], a_ref[...], preferred_element_type=F32)
         - jnp.dot(s_ref[...], b_ref[...], preferred_element_type=F32))
    o_ref[...] = _rms(y, g_ref[...]).astype(BF16)


def _pos_dft(cos_m, sin_m, ab, g_four, *, layer, tm):
    seq = cos_m.shape[0]
    t = ab.shape[0]
    per_seq = seq // tm
    return pl.pallas_call(
        _pos_dft_kernel,
        grid=(t // seq, per_seq),
        in_specs=[
            pl.BlockSpec((tm, seq), lambda b, j: (j, 0)),
            pl.BlockSpec((tm, seq), lambda b, j: (j, 0)),
            pl.BlockSpec((seq, FOURIER_WIDTH), lambda b, j: (b, 0)),
            pl.BlockSpec((seq, FOURIER_WIDTH), lambda b, j: (b, 1)),
            _layer_spec((1, FOURIER_WIDTH), layer),
        ],
        out_specs=pl.BlockSpec((tm, FOURIER_WIDTH), lambda b, j: (b * per_seq + j, 0)),
        out_shape=jax.ShapeDtypeStruct((t, FOURIER_WIDTH), BF16),
        compiler_params=_params("arbitrary", "arbitrary"),
        name="pos_dft",
    )(cos_m, sin_m, ab, ab, g_four)


FLIP_BLOCK = 256
DFT_PAD_ROWS = 16


def _pos_dft_sym_kernel(alo_ref, ahi_ref, blo_ref, bhi_ref, c_ref, s_ref, q_ref, sgn_ref, g_ref,
                        o_ref, e_ref, d_ref, h_ref):
    half = alo_ref.shape[0]
    blk = q_ref.shape[0]
    nblk = half // blk
    qm = q_ref[...]
    is_row0 = lax.broadcasted_iota(jnp.int32, (blk, 1), 0) == 0

    def flipped_block(src_ref, bt, first_row):
        lo = (nblk - 1 - bt) * blk
        y = jnp.dot(qm, src_ref[lo:lo + blk, :], preferred_element_type=F32)
        if bt > 0:
            first_row = src_ref[lo + blk:lo + blk + 1, :].astype(F32)
        return jnp.where(is_row0, first_row, y)

    for bt in range(nblk):
        r = slice(bt * blk, (bt + 1) * blk)
        e_ref[r, :] = (alo_ref[r, :].astype(F32) + flipped_block(ahi_ref, bt, 0.0)).astype(BF16)
        d_ref[r, :] = (blo_ref[r, :].astype(F32) - flipped_block(bhi_ref, bt, 0.0)).astype(BF16)

    a_mid = ahi_ref[0:1, :].astype(F32)
    g = g_ref[...]
    mid_row = None
    for kb in range(nblk):
        r = slice(kb * blk, (kb + 1) * blk)
        rows = blk + (DFT_PAD_ROWS if kb == nblk - 1 else 0)
        cr = slice(kb * blk, kb * blk + rows)
        yc = jnp.dot(c_ref[cr, :], e_ref[...], preferred_element_type=F32) + sgn_ref[cr, :] * a_mid
        ys = jnp.dot(s_ref[r, :], d_ref[...], preferred_element_type=F32)
        o_ref[r, :] = _rms(yc[:blk] - ys, g).astype(BF16)
        h_ref[r, :] = _rms(yc[:blk] + ys, g).astype(BF16)
        if kb == nblk - 1:
            mid_row = _rms(yc[blk:blk + 1], g)

    for bs in range(nblk):
        r = slice(half + bs * blk, half + (bs + 1) * blk)
        o_ref[r, :] = flipped_block(h_ref, bs, mid_row).astype(BF16)


def _pos_dft_sym(tables, ab, g_four, *, layer):
    cos_m, sin_m, flip_m, sgn = tables
    half = sin_m.shape[0]
    t = ab.shape[0]
    blocks = lambda row, col: pl.BlockSpec((half, FOURIER_WIDTH), lambda b: (2 * b + row, col))
    return pl.pallas_call(
        _pos_dft_sym_kernel,
        grid=(t // (2 * half),),
        in_specs=[
            blocks(0, 0), blocks(1, 0), blocks(0, 1), blocks(1, 1),
            _const_spec(cos_m.shape),
            _const_spec(sin_m.shape),
            _const_spec(flip_m.shape),
            _const_spec(sgn.shape),
            _layer_spec((1, FOURIER_WIDTH), layer),
        ],
        out_specs=pl.BlockSpec((2 * half, FOURIER_WIDTH), lambda b: (b, 0)),
        out_shape=jax.ShapeDtypeStruct((t, FOURIER_WIDTH), BF16),
        scratch_shapes=[pltpu.VMEM((half, FOURIER_WIDTH), BF16)] * 3,
        compiler_params=_params("arbitrary"),
        name="pos_dft_sym",
    )(ab, ab, ab, ab, cos_m, sin_m, flip_m, sgn, g_four)


def _out_proj_kernel(a_ref, f_ref, w_ref, x_ref, mod_ref, gpost_ref, gpre_ref, *rest, next_mod):
    if next_mod:
        (c_ref, wm_ref, bm_ref), rest = rest[:3], rest[3:]
        _mod_block(c_ref, wm_ref, bm_ref, rest[2])
    xo_ref, h_ref = rest[:2]
    mod = mod_ref[0]
    gate_g = mod[2:3] * gpost_ref[...]
    scale_g = (1.0 + mod[4:5]) * gpre_ref[...]
    rows = x_ref.shape[0] // OUT_PROJ_SUB_BLOCKS

    def matmuls(s):
        r = slice(s * rows, (s + 1) * rows)
        return (jnp.dot(a_ref[r, :], w_ref[:ATTN_WIDTH, :], preferred_element_type=F32)
                + jnp.dot(f_ref[r, :], w_ref[ATTN_WIDTH:, :], preferred_element_type=F32))

    pending = matmuls(0)
    for s in range(OUT_PROJ_SUB_BLOCKS):
        r = slice(s * rows, (s + 1) * rows)
        mix = pending
        if s + 1 < OUT_PROJ_SUB_BLOCKS:
            pending = matmuls(s + 1)
        xn = x_ref[r, :] + _rms(mix, gate_g)
        xo_ref[r, :] = xn
        h_ref[r, :] = (_rms(xn, scale_g) + mod[3:4]).astype(BF16)


def _out_proj(a, f, w_out, x2d, mod, g_post, g_pre_ffn, *, layer, mod_row, tm, next_mod=None):
    t, d = x2d.shape
    n_steps = t // tm
    in_specs = [
        pl.BlockSpec((tm, ATTN_WIDTH), lambda i: (i, 0)),
        pl.BlockSpec((tm, FOURIER_WIDTH), lambda i: (i, 0)),
        _const_spec((ATTN_WIDTH + FOURIER_WIDTH, d)),
        pl.BlockSpec((tm, d), lambda i: (i, 0)),
        _mod_spec(d, mod_row),
        _layer_spec((1, d), layer),
        _layer_spec((1, d), layer),
    ]
    out_specs = [pl.BlockSpec((tm, d), lambda i: (i, 0)), pl.BlockSpec((tm, d), lambda i: (i, 0))]
    out_shape = [jax.ShapeDtypeStruct((t, d), F32), jax.ShapeDtypeStruct((t, d), BF16)]
    args = [a, f, w_out, x2d, mod, g_post, g_pre_ffn]
    if next_mod is not None:
        mod_in, mod_out, mod_shape = _mod_block_specs(d, next_mod[1].shape[-1], layer + 1, n_steps)
        in_specs += mod_in
        out_specs.append(mod_out)
        out_shape.append(mod_shape)
        args += list(next_mod)
    return pl.pallas_call(
        functools.partial(_out_proj_kernel, next_mod=next_mod is not None),
        grid=(n_steps,),
        in_specs=in_specs,
        out_specs=out_specs,
        out_shape=out_shape,
        compiler_params=_params("arbitrary"),
        name="out_proj",
    )(*args)


CONV_HALO = 8


def _ffn_up_kernel(h_ref, wg_ref, wv_ref, cw_ref, cb_ref, o_ref, *, seg_len):
    rows, tn = o_ref.shape
    chunk = rows // FFN_UP_ROW_CHUNKS
    pos = lax.broadcasted_iota(jnp.int32, (rows, 1), 0) % seg_len
    first, final = pos == 0, pos == seg_len - 1
    cw = cw_ref[...]
    cb = cb_ref[...]

    def span(s):
        return max(s * chunk - CONV_HALO, 0), min((s + 1) * chunk + CONV_HALO, rows)

    def matmuls(s):
        lo, hi = span(s)
        h = h_ref[lo:hi, :]
        return (jnp.dot(h, wg_ref[...], preferred_element_type=F32),
                jnp.dot(h, wv_ref[...], preferred_element_type=F32))

    def epilogue(s, gate, val):
        lo, hi = span(s)
        prev = jnp.where(first[lo:hi], 0.0, pltpu.roll(gate, 1, axis=0))
        nxt = jnp.where(final[lo:hi], 0.0, pltpu.roll(gate, hi - lo - 1, axis=0))
        c = prev * cw[0:1] + gate * cw[1:2] + nxt * cw[2:3] + cb
        act = 0.5 * c * (1.0 + jnp.tanh(math.sqrt(2.0 / math.pi) * (c + 0.044715 * (c * c * c))))
        own = slice(s * chunk - lo, (s + 1) * chunk - lo)
        o_ref[s * chunk:(s + 1) * chunk, :] = (act * val)[own].astype(BF16)

    pending = matmuls(0)
    for s in range(FFN_UP_ROW_CHUNKS):
        ready = pending
        if s + 1 < FFN_UP_ROW_CHUNKS:
            pending = matmuls(s + 1)
        epilogue(s, *ready)


def _ffn_up(h, w_up, conv_w, conv_b, *, layer, seg_len, tm, tn):
    t, d = h.shape
    ffn = conv_w.shape[-1]
    n_col = ffn // tn
    return pl.pallas_call(
        functools.partial(_ffn_up_kernel, seg_len=seg_len),
        grid=(t // tm, n_col),
        in_specs=[
            pl.BlockSpec((tm, d), lambda i, j: (i, 0)),
            pl.BlockSpec((d, tn), lambda i, j: (0, j)),
            pl.BlockSpec((d, tn), lambda i, j: (0, j + n_col)),
            pl.BlockSpec((None, 3, tn), lambda i, j: (layer, 0, j)),
            pl.BlockSpec((None, 1, tn), lambda i, j: (layer, 0, j)),
        ],
        out_specs=pl.BlockSpec((tm, tn), lambda i, j: (i, j)),
        out_shape=jax.ShapeDtypeStruct((t, ffn), BF16),
        compiler_params=_params("arbitrary", "arbitrary"),
        name="ffn_up",
    )(h, w_up, w_up, conv_w, conv_b)


LANES = 128


def _ffn_down_skew_kernel(u_ref, w_ref, x_ref, mod_ref, gpost_ref, xo_ref, y_ref, inv_ref, ssq_ref):
    i, j = pl.program_id(0), pl.program_id(1)
    n_tiles = pl.num_programs(0) - 1
    n_col, _, tn = y_ref.shape

    @pl.when((i == 0) & (j == 0))
    def _():
        y_ref[...] = jnp.zeros_like(y_ref)
        inv_ref[...] = jnp.zeros_like(inv_ref)

    @pl.when(j == 0)
    def _():
        ssq_ref[...] = jnp.zeros_like(ssq_ref)

    def finish_previous():
        gate_g = mod_ref[0][N_MOD - 1:N_MOD] * gpost_ref[...]
        xo_ref[...] = x_ref[...] + (y_ref[j] * inv_ref[...]) * gate_g

    @pl.when(i < n_tiles)
    def _():
        finish_previous()
        y = jnp.dot(u_ref[...], w_ref[...], preferred_element_type=F32)
        y_ref[j] = y
        sq = y * y
        ssq_ref[...] += functools.reduce(
            jnp.add, [sq[:, c * LANES:(c + 1) * LANES] for c in range(tn // LANES)])

    @pl.when(i == n_tiles)
    def _():
        finish_previous()

    @pl.when((j == n_col - 1) & (i < n_tiles))
    def _():
        ssq = jnp.sum(ssq_ref[...], axis=-1, keepdims=True)
        inv_ref[...] = lax.rsqrt(ssq / (n_col * tn) + EPS)


def _ffn_down_skew(u, w_down, x2d, mod, g_post, *, layer, mod_row, tm, tn):
    t, d = x2d.shape
    ffn = u.shape[1]
    n_tiles = t // tm
    n_col = d // tn
    prev = lambda i: jnp.maximum(i - 1, 0)
    col = lambda i, j: jnp.where(i > 0, j, 0)
    return pl.pallas_call(
        _ffn_down_skew_kernel,
        grid=(n_tiles + 1, n_col),
        in_specs=[
            pl.BlockSpec((tm, ffn), lambda i, j: (jnp.minimum(i, n_tiles - 1), 0)),
            pl.BlockSpec((ffn, tn), lambda i, j: (0, jnp.where(i < n_tiles, j, n_col - 1))),
            pl.BlockSpec((tm, tn), lambda i, j: (prev(i), col(i, j))),
            pl.BlockSpec((1, N_MOD, tn), lambda i, j: (mod_row(prev(i)), 0, col(i, j))),
            pl.BlockSpec((None, 1, tn), lambda i, j: (layer, 0, col(i, j))),
        ],
        out_specs=pl.BlockSpec((tm, tn), lambda i, j: (prev(i), col(i, j))),
        out_shape=jax.ShapeDtypeStruct((t, d), F32),
        scratch_shapes=[pltpu.VMEM((n_col, tm, tn), F32), pltpu.VMEM((tm, 1), F32),
                        pltpu.VMEM((tm, LANES), F32)],
        compiler_params=_params("arbitrary", "arbitrary"),
        name="ffn_down_skew",
    )(u, w_down, x2d, mod, g_post)


def _rope_tables(n):
    pos = np.arange(n)
    n_pairs_axis = HEAD_DIM // 4
    freqs = ROPE_THETA ** (-np.arange(n_pairs_axis, dtype=np.float64) / n_pairs_axis)
    ang = np.concatenate([(pos // GRID_W)[:, None] * freqs, (pos % GRID_W)[:, None] * freqs], axis=-1)
    c, s = np.cos(ang), np.sin(ang)
    return (np.concatenate([c, c], axis=-1).astype(np.float32),
            np.concatenate([-s, s], axis=-1).astype(np.float32))


def _dft_matrices(n):
    idx = np.arange(n)
    ang = ((idx[:, None] * idx[None, :]) % n) * (2.0 * math.pi / n)
    return np.cos(ang) * n ** -0.5, np.sin(ang) * n ** -0.5


def _sym_dft_tables(n):
    half = n // 2
    k = np.arange(half + DFT_PAD_ROWS)[:, None]
    ang = ((k * np.arange(half)[None, :]) % n) * (2.0 * math.pi / n)
    scale = n ** -0.5
    valid = k <= half
    cos_m = jnp.asarray(np.where(valid, np.cos(ang) * scale, 0.0), F32).astype(BF16)
    sin_m = jnp.asarray(np.sin(ang[:half]) * scale, F32).astype(BF16)
    sgn = np.where(valid, (1 - 2 * (k % 2)) * scale, 0.0).astype(np.float32)
    i = np.arange(FLIP_BLOCK)
    flip_m = (i[:, None] + i[None, :] == FLIP_BLOCK).astype(BF16)
    return cos_m, sin_m, flip_m, sgn


def _deinterleave(v):
    return jnp.concatenate([v[..., 0::2], v[..., 1::2]], axis=-1)


def kernel(x, c, ctx, c_ctx, w_mod, b_mod, g_pre_mix, g_post_mix, g_pre_ffn, g_post_ffn,
           w_in, q_norm, k_norm, w_four, g_attn_out, g_four_out, w_out,
           w_up, conv_w, conv_b, w_down):
    b, n, d = x.shape
    ctx_len = ctx.shape[1]
    depth = w_mod.shape[0]

    perm_matrix = _deinterleave(np.eye(HEAD_DIM, dtype=BF16))
    w_in_b = _prep_w_in(w_in, perm_matrix)
    qn = _deinterleave(q_norm).reshape(depth, 1, HEAD_DIM)
    kn = _deinterleave(k_norm).reshape(depth, 1, HEAD_DIM)
    cos_t, sin_t = _rope_tables(n)
    dft_tables = _sym_dft_tables(n)
    cos_c, sin_c = (jnp.asarray(m, F32).astype(BF16) for m in _dft_matrices(ctx_len))
    cs_ch = np.stack(_dft_matrices(FOURIER_GROUP)).astype(np.float32)
    m_fold = _fold_channel_dft(cs_ch, w_four)
    rows3 = lambda v: v.reshape(depth, 1, -1)
    g_pre_mix, g_post_mix, g_pre_ffn, g_post_ffn, g_attn_out, g_four_out, conv_b, b_mod = map(
        rows3, (g_pre_mix, g_post_mix, g_pre_ffn, g_post_ffn, g_attn_out, g_four_out, conv_b, b_mod))

    cc = jnp.zeros((MOD_ROWS, d), F32).at[:b].set(c).at[b].set(c_ctx)
    as_table = lambda m: m.reshape(MOD_ROWS, N_MOD, d)
    mod = as_table(_mod_proj(cc, w_mod, b_mod, layer=0))

    xl = x.reshape(b * n, d)
    xc = ctx.reshape(b * ctx_len, d)
    tm = ROW_TILE
    lat_row = lambda i: i // (n // tm)
    ctx_row = lambda i: b

    for i in range(depth):
        last = i == depth - 1
        q_l, k_l, v_l, ab_l, w_up_b = _in_proj(
            xl, mod, g_pre_mix, w_in_b, qn, kn, cos_t, sin_t, m_fold,
            layer=i, mod_row=lat_row, seq=n, rope=True, tm=tm, cast=(w_up,))
        if last:
            k_c, v_c = _ctx_kv(xc, mod, g_pre_mix, w_in_b, kn, layer=i, mod_row=ctx_row, tm=tm)
        else:
            q_c, k_c, v_c, ab_c = _in_proj(
                xc, mod, g_pre_mix, w_in_b, qn, kn, cos_t, sin_t, m_fold,
                layer=i, mod_row=ctx_row, seq=ctx_len, rope=False, tm=tm)

        attn_l, w_out_b, w_down_b = _attention(
            q_l, [(k_l, v_l, n), (k_c, v_c, ctx_len)], g_attn_out,
            layer=i, q_seq=n, tq=ATTN_Q_TILE, cast=(w_out, w_down))
        four_l = _pos_dft_sym(dft_tables, ab_l, g_four_out, layer=i)
        xl, h_l, *next_mod = _out_proj(
            attn_l, four_l, w_out_b, xl, mod, g_post_mix, g_pre_ffn, layer=i, mod_row=lat_row, tm=tm,
            next_mod=None if last else (cc, w_mod, b_mod))
        u_l = _ffn_up(h_l, w_up_b, conv_w, conv_b, layer=i, seg_len=n, tm=n, tn=FFN_COL_TILE)
        xl = _ffn_down_skew(u_l, w_down_b, xl, mod, g_post_ffn, layer=i,
                            mod_row=lambda r: r // (n // FFN_DOWN_ROW_TILE),
                            tm=FFN_DOWN_ROW_TILE, tn=FFN_COL_TILE)

        if not last:
            attn_c, = _attention(q_c, [(k_c, v_c, ctx_len)], g_attn_out,
                                 layer=i, q_seq=ctx_len, tq=ctx_len)
            four_c = _pos_dft(cos_c, sin_c, ab_c, g_four_out, layer=i, tm=ctx_len)
            xc, h_c = _out_proj(attn_c, four_c, w_out_b, xc, mod, g_post_mix, g_pre_ffn,
                                layer=i, mod_row=ctx_row, tm=tm)
            u_c = _ffn_up(h_c, w_up_b, conv_w, conv_b, layer=i, seg_len=ctx_len,
                          tm=b * ctx_len, tn=FFN_COL_TILE)
            xc = _ffn_down_skew(u_c, w_down_b, xc, mod, g_post_ffn, layer=i, mod_row=ctx_row,
                                tm=FFN_DOWN_ROW_TILE, tn=FFN_COL_TILE)
            mod = as_table(next_mod[0])

    return xl.reshape(b, n, d)
```

```python
import functools
import math

import jax
import jax.numpy as jnp
import numpy as np
from jax import lax
from jax.experimental import pallas as pl
from jax.experimental.pallas import tpu as pltpu

GRID_W = 64
HEAD_DIM = 128
N_Q_HEADS = 8
N_KV_HEADS = 2
Q_PER_KV = N_Q_HEADS // N_KV_HEADS
FOURIER_GROUP = 128
N_FOURIER_GROUPS = 8
ATTN_WIDTH = N_Q_HEADS * HEAD_DIM
KV_WIDTH = N_KV_HEADS * HEAD_DIM
FOURIER_WIDTH = N_FOURIER_GROUPS * FOURIER_GROUP
Q_END = ATTN_WIDTH
K_END = Q_END + KV_WIDTH
V_END = K_END + KV_WIDTH
IN_WIDTH = V_END + FOURIER_WIDTH
ROPE_THETA = 10000.0
EPS = 1e-6
N_MOD = 6
Q_SCALE = HEAD_DIM ** -0.5 * math.log2(math.e)
MOD_ROWS = 16

VMEM_LIMIT_BYTES = 56 * 1024 * 1024

ROW_TILE = 512
ATTN_Q_TILE = 1024
ATTN_KEY_CHUNK = 256
FFN_COL_TILE = 512
FFN_DOWN_ROW_TILE = 1024
FFN_UP_ROW_CHUNKS = 2
OUT_PROJ_SUB_BLOCKS = 4
MOD_COL_TILE = 1024
PREP_ROW_TILE = 256

F32 = jnp.float32
BF16 = jnp.bfloat16


def _params(*semantics):
    return pltpu.CompilerParams(dimension_semantics=semantics, vmem_limit_bytes=VMEM_LIMIT_BYTES)


def _rms(x, g):
    ms = jnp.mean(x * x, axis=-1, keepdims=True)
    return x * lax.rsqrt(ms + EPS) * g


def _const_spec(shape):
    return pl.BlockSpec(shape, lambda *_: (0,) * len(shape), pipeline_mode=pl.Buffered(1))


def _layer_spec(shape, layer):
    zeros = (0,) * len(shape)
    return pl.BlockSpec((None,) + shape, lambda *_: (layer,) + zeros, pipeline_mode=pl.Buffered(1))


def _side_casts(sources, layer, n_steps, step_of):
    in_specs, out_specs, out_shapes = [], [], []
    for src in sources:
        _, r, c = src.shape
        rows = r // n_steps
        in_specs.append(pl.BlockSpec((None, rows, c), lambda *g: (layer, step_of(*g), 0)))
        out_specs.append(pl.BlockSpec((rows, c), lambda *g: (step_of(*g), 0)))
        out_shapes.append(jax.ShapeDtypeStruct((r, c), BF16))
    return in_specs, out_specs, out_shapes


def _run_side_casts(src_refs, dst_refs):
    for src, dst in zip(src_refs, dst_refs, strict=True):
        dst[...] = src[...].astype(BF16)


def _mod_spec(d, row_of):
    return pl.BlockSpec((1, N_MOD, d), lambda i, *_: (row_of(i), 0, 0))


def _mod_block(c_ref, w_ref, b_ref, o_ref):
    c = c_ref[...]
    s = (c * jax.nn.sigmoid(c)).astype(BF16)
    o_ref[...] = jnp.dot(s, w_ref[...].astype(BF16), preferred_element_type=F32) + b_ref[...]


def _mod_block_specs(d, width, layer, n_steps):
    tn = width // n_steps
    in_specs = [
        pl.BlockSpec((MOD_ROWS, d), lambda i, *_: (0, 0)),
        pl.BlockSpec((None, d, tn), lambda i, *_: (layer, 0, i)),
        pl.BlockSpec((None, 1, tn), lambda i, *_: (layer, 0, i)),
    ]
    out_spec = pl.BlockSpec((MOD_ROWS, tn), lambda i, *_: (0, i))
    return in_specs, out_spec, jax.ShapeDtypeStruct((MOD_ROWS, width), F32)


def _mod_proj(cc, w_mod, b_mod, *, layer):
    _, d, width = w_mod.shape
    n_steps = width // MOD_COL_TILE
    in_specs, out_spec, out_shape = _mod_block_specs(d, width, layer, n_steps)
    return pl.pallas_call(
        _mod_block,
        grid=(n_steps,),
        in_specs=in_specs,
        out_specs=out_spec,
        out_shape=out_shape,
        compiler_params=_params("arbitrary"),
        name="mod_proj",
    )(cc, w_mod, b_mod)


def _fold_kernel(cs_ref, w_ref, o_ref):
    for g in range(w_ref.shape[1]):
        w = w_ref[0, g]
        o_ref[0, g, :, :FOURIER_GROUP] = jnp.dot(
            cs_ref[0], w, preferred_element_type=F32, precision=lax.Precision.HIGHEST).astype(BF16)
        o_ref[0, g, :, FOURIER_GROUP:] = jnp.dot(
            cs_ref[1], w, preferred_element_type=F32, precision=lax.Precision.HIGHEST).astype(BF16)


def _fold_channel_dft(cs, w_four):
    depth, groups, c, _ = w_four.shape
    return pl.pallas_call(
        _fold_kernel,
        grid=(depth,),
        in_specs=[
            pl.BlockSpec((2, c, c), lambda l: (0, 0, 0)),
            pl.BlockSpec((1, groups, c, c), lambda l: (l, 0, 0, 0)),
        ],
        out_specs=pl.BlockSpec((1, groups, c, 2 * c), lambda l: (l, 0, 0, 0)),
        out_shape=jax.ShapeDtypeStruct((depth, groups, c, 2 * c), BF16),
        compiler_params=_params("arbitrary"),
        name="fold_channel_dft",
    )(cs, w_four)


def _prep_w_in_kernel(w_ref, p_ref, o_ref):
    for head in range(N_Q_HEADS + N_KV_HEADS):
        cols = slice(head * HEAD_DIM, (head + 1) * HEAD_DIM)
        o_ref[0, :, cols] = jnp.dot(w_ref[0, :, cols].astype(BF16), p_ref[...],
                                    preferred_element_type=F32).astype(BF16)
    o_ref[0, :, K_END:] = w_ref[0, :, K_END:].astype(BF16)


def _prep_w_in(w_in, perm_matrix):
    depth, d, width = w_in.shape
    rows = PREP_ROW_TILE
    return pl.pallas_call(
        _prep_w_in_kernel,
        grid=(depth, d // rows),
        in_specs=[
            pl.BlockSpec((1, rows, width), lambda l, j: (l, j, 0)),
            pl.BlockSpec((HEAD_DIM, HEAD_DIM), lambda l, j: (0, 0)),
        ],
        out_specs=pl.BlockSpec((1, rows, width), lambda l, j: (l, j, 0)),
        out_shape=jax.ShapeDtypeStruct((depth, d, width), BF16),
        compiler_params=_params("arbitrary", "arbitrary"),
        name="prep_w_in",
    )(w_in, perm_matrix)


def _in_proj_kernel(x_ref, mod_ref, gpre_ref, w_ref, qn_ref, kn_ref, cos_ref, sin_ref, m_ref, *rest,
                    rope, n_cast):
    cast_src, (q_ref, k_ref, v_ref, ab_ref), cast_dst = rest[:n_cast], rest[n_cast:n_cast + 4], rest[n_cast + 4:]
    _run_side_casts(cast_src, cast_dst)
    mod = mod_ref[0]
    h = _rms(x_ref[...], (1.0 + mod[1:2]) * gpre_ref[...]) + mod[0:1]
    p = jnp.dot(h.astype(BF16), w_ref[...], preferred_element_type=F32)

    def head(col, gain, post_scale):
        y = _rms(p[:, col:col + HEAD_DIM], gain)
        if rope:
            y = y * cos_ref[...] + pltpu.roll(y, HEAD_DIM // 2, axis=1) * sin_ref[...]
        if post_scale != 1.0:
            y = y * post_scale
        return y.astype(BF16)

    for i in range(N_Q_HEADS):
        q_ref[:, i * HEAD_DIM:(i + 1) * HEAD_DIM] = head(i * HEAD_DIM, qn_ref[...], Q_SCALE)
    for i in range(N_KV_HEADS):
        k_ref[:, i * HEAD_DIM:(i + 1) * HEAD_DIM] = head(Q_END + i * HEAD_DIM, kn_ref[...], 1.0)
    v_ref[...] = p[:, K_END:V_END].astype(BF16)
    for g in range(N_FOURIER_GROUPS):
        lo = V_END + g * FOURIER_GROUP
        ab = jnp.dot(p[:, lo:lo + FOURIER_GROUP].astype(BF16), m_ref[g], preferred_element_type=F32)
        ab_ref[:, g * FOURIER_GROUP:(g + 1) * FOURIER_GROUP] = ab[:, :FOURIER_GROUP].astype(BF16)
        ab_ref[:, FOURIER_WIDTH + g * FOURIER_GROUP:FOURIER_WIDTH + (g + 1) * FOURIER_GROUP] = (
            ab[:, FOURIER_GROUP:].astype(BF16))


def _in_proj(x2d, mod, g_pre, w_in, qn, kn, cos_t, sin_t, m_fold, *, layer, mod_row, seq, rope, tm,
             cast=()):
    t, d = x2d.shape
    per_seq = seq // tm
    rope_map = (lambda i: (i % per_seq, 0)) if rope else (lambda i: (0, 0))
    cast_in, cast_out, cast_shapes = _side_casts(cast, layer, t // tm, lambda i: i)
    return pl.pallas_call(
        functools.partial(_in_proj_kernel, rope=rope, n_cast=len(cast)),
        grid=(t // tm,),
        in_specs=[
            pl.BlockSpec((tm, d), lambda i: (i, 0)),
            _mod_spec(d, mod_row),
            _layer_spec((1, d), layer),
            _layer_spec((d, IN_WIDTH), layer),
            _layer_spec((1, HEAD_DIM), layer),
            _layer_spec((1, HEAD_DIM), layer),
            pl.BlockSpec((tm, HEAD_DIM), rope_map),
            pl.BlockSpec((tm, HEAD_DIM), rope_map),
            _layer_spec((N_FOURIER_GROUPS, FOURIER_GROUP, 2 * FOURIER_GROUP), layer),
        ] + cast_in,
        out_specs=[
            pl.BlockSpec((tm, ATTN_WIDTH), lambda i: (i, 0)),
            pl.BlockSpec((tm, KV_WIDTH), lambda i: (i, 0)),
            pl.BlockSpec((tm, KV_WIDTH), lambda i: (i, 0)),
            pl.BlockSpec((tm, 2 * FOURIER_WIDTH), lambda i: (i, 0)),
        ] + cast_out,
        out_shape=[
            jax.ShapeDtypeStruct((t, ATTN_WIDTH), BF16),
            jax.ShapeDtypeStruct((t, KV_WIDTH), BF16),
            jax.ShapeDtypeStruct((t, KV_WIDTH), BF16),
            jax.ShapeDtypeStruct((t, 2 * FOURIER_WIDTH), BF16),
        ] + cast_shapes,
        compiler_params=_params("arbitrary"),
        name="in_proj",
    )(x2d, mod, g_pre, w_in, qn, kn, cos_t, sin_t, m_fold, *cast)


def _ctx_kv_kernel(x_ref, mod_ref, gpre_ref, w_ref, kn_ref, k_ref, v_ref):
    mod = mod_ref[0]
    h = _rms(x_ref[...], (1.0 + mod[1:2]) * gpre_ref[...]) + mod[0:1]
    p = jnp.dot(h.astype(BF16), w_ref[...], preferred_element_type=F32)
    for i in range(N_KV_HEADS):
        cols = slice(i * HEAD_DIM, (i + 1) * HEAD_DIM)
        k_ref[:, cols] = _rms(p[:, cols], kn_ref[...]).astype(BF16)
    v_ref[...] = p[:, KV_WIDTH:].astype(BF16)


def _ctx_kv(x2d, mod, g_pre, w_in, kn, *, layer, mod_row, tm):
    t, d = x2d.shape
    assert Q_END % (2 * KV_WIDTH) == 0
    return pl.pallas_call(
        _ctx_kv_kernel,
        grid=(t // tm,),
        in_specs=[
            pl.BlockSpec((tm, d), lambda i: (i, 0)),
            _mod_spec(d, mod_row),
            _layer_spec((1, d), layer),
            pl.BlockSpec((None, d, 2 * KV_WIDTH), lambda i: (layer, 0, Q_END // (2 * KV_WIDTH)),
                         pipeline_mode=pl.Buffered(1)),
            _layer_spec((1, HEAD_DIM), layer),
        ],
        out_specs=[pl.BlockSpec((tm, KV_WIDTH), lambda i: (i, 0))] * 2,
        out_shape=[jax.ShapeDtypeStruct((t, KV_WIDTH), BF16)] * 2,
        compiler_params=_params("arbitrary"),
        name="ctx_kv",
    )(x2d, mod, g_pre, w_in, kn)


def _attn_kernel(*refs, n_seg, n_cast):
    q_ref = refs[0]
    kv_refs = refs[1:1 + 2 * n_seg]
    g_ref = refs[1 + 2 * n_seg]
    rest = refs[2 + 2 * n_seg:]
    cast_src, o_ref, cast_dst, acc_ref = rest[:n_cast], rest[n_cast], rest[n_cast + 1:-1], rest[-1]
    _run_side_casts(cast_src, cast_dst)

    chunks = []
    for i in range(n_seg):
        total = kv_refs[2 * i].shape[0]
        for lo in range(0, total, ATTN_KEY_CHUNK):
            chunks.append((kv_refs[2 * i], kv_refs[2 * i + 1], lo, min(ATTN_KEY_CHUNK, total - lo)))

    def chunk_scores(h, c):
        k_ref, _, lo, size = chunks[c]
        kv = (h // Q_PER_KV) * HEAD_DIM
        return lax.dot_general(k_ref[lo:lo + size, kv:kv + HEAD_DIM], q_ref[:, h * HEAD_DIM:(h + 1) * HEAD_DIM],
                               (((1,), (1,)), ((), ())), preferred_element_type=F32)

    def chunk_output(h, c, p):
        _, v_ref, lo, size = chunks[c]
        kv = (h // Q_PER_KV) * HEAD_DIM
        return lax.dot_general(v_ref[lo:lo + size, kv:kv + HEAD_DIM], p,
                               (((0,), (0,)), ((), ())), preferred_element_type=F32)

    def col_max(m, s):
        cm = jnp.max(s, axis=0, keepdims=True)
        return cm if m is None else jnp.maximum(m, cm)

    scores, m = [], None
    for c in range(len(chunks)):
        scores.append(chunk_scores(0, c))
        m = col_max(m, scores[-1])
    for h in range(N_Q_HEADS):
        next_scores, next_m = [], None
        denom, out_t = None, None
        for c in range(len(chunks)):
            if h + 1 < N_Q_HEADS:
                next_scores.append(chunk_scores(h + 1, c))
                next_m = col_max(next_m, next_scores[-1])
            p = jnp.exp2(scores[c] - m)
            ps = jnp.sum(p, axis=0, keepdims=True)
            po = chunk_output(h, c, p.astype(BF16))
            denom = ps if denom is None else denom + ps
            out_t = po if out_t is None else out_t + po
        acc_ref[:, h * HEAD_DIM:(h + 1) * HEAD_DIM] = (out_t / denom).T
        scores, m = next_scores, next_m
    o_ref[...] = _rms(acc_ref[...], g_ref[...]).astype(BF16)


def _attention(q, segments, g_attn, *, layer, q_seq, tq, cast=()):
    t = q.shape[0]
    per_seq = q_seq // tq
    in_specs = [pl.BlockSpec((tq, ATTN_WIDTH), lambda b, j: (b * per_seq + j, 0))]
    args = [q]
    for k, v, rows in segments:
        in_specs += [pl.BlockSpec((rows, KV_WIDTH), lambda b, j: (b, 0))] * 2
        args += [k, v]
    in_specs.append(_layer_spec((1, ATTN_WIDTH), layer))
    args.append(g_attn)
    cast_in, cast_out, cast_shapes = _side_casts(cast, layer, t // tq, lambda b, j: b * per_seq + j)
    return pl.pallas_call(
        functools.partial(_attn_kernel, n_seg=len(segments), n_cast=len(cast)),
        grid=(t // q_seq, per_seq),
        in_specs=in_specs + cast_in,
        out_specs=[pl.BlockSpec((tq, ATTN_WIDTH), lambda b, j: (b * per_seq + j, 0))] + cast_out,
        out_shape=[jax.ShapeDtypeStruct((t, ATTN_WIDTH), BF16)] + cast_shapes,
        scratch_shapes=[pltpu.VMEM((tq, ATTN_WIDTH), F32)],
        compiler_params=_params("arbitrary", "arbitrary"),
        name="attention",
    )(*args, *cast)


def _pos_dft_kernel(c_ref, s_ref, a_ref, b_ref, g_ref, o_ref):
    y = (jnp.dot(c_ref[...], a_ref[...], preferred_element_type=F32)
         - jnp.dot(s_ref[...], b_ref[...], preferred_element_type=F32))
    o_ref[...] = _rms(y, g_ref[...]).astype(BF16)


def _pos_dft(cos_m, sin_m, ab, g_four, *, layer, tm):
    seq = cos_m.shape[0]
    t = ab.shape[0]
    per_seq = seq // tm
    return pl.pallas_call(
        _pos_dft_kernel,
        grid=(t // seq, per_seq),
        in_specs=[
            pl.BlockSpec((tm, seq), lambda b, j: (j, 0)),
            pl.BlockSpec((tm, seq), lambda b, j: (j, 0)),
            pl.BlockSpec((seq, FOURIER_WIDTH), lambda b, j: (b, 0)),
            pl.BlockSpec((seq, FOURIER_WIDTH), lambda b, j: (b, 1)),
            _layer_spec((1, FOURIER_WIDTH), layer),
        ],
        out_specs=pl.BlockSpec((tm, FOURIER_WIDTH), lambda b, j: (b * per_seq + j, 0)),
        out_shape=jax.ShapeDtypeStruct((t, FOURIER_WIDTH), BF16),
        compiler_params=_params("arbitrary", "arbitrary"),
        name="pos_dft",
    )(cos_m, sin_m, ab, ab, g_four)


FLIP_BLOCK = 256
DFT_PAD_ROWS = 16


def _pos_dft_sym_kernel(alo_ref, ahi_ref, blo_ref, bhi_ref, c_ref, s_ref, q_ref, sgn_ref, g_ref,
                        o_ref, e_ref, d_ref, h_ref):
    half = alo_ref.shape[0]
    blk = q_ref.shape[0]
    nblk = half // blk
    qm = q_ref[...]
    is_row0 = lax.broadcasted_iota(jnp.int32, (blk, 1), 0) == 0

    def flipped_block(src_ref, bt, first_row):
        lo = (nblk - 1 - bt) * blk
        y = jnp.dot(qm, src_ref[lo:lo + blk, :], preferred_element_type=F32)
        if bt > 0:
            first_row = src_ref[lo + blk:lo + blk + 1, :].astype(F32)
        return jnp.where(is_row0, first_row, y)

    for bt in range(nblk):
        r = slice(bt * blk, (bt + 1) * blk)
        e_ref[r, :] = (alo_ref[r, :].astype(F32) + flipped_block(ahi_ref, bt, 0.0)).astype(BF16)
        d_ref[r, :] = (blo_ref[r, :].astype(F32) - flipped_block(bhi_ref, bt, 0.0)).astype(BF16)

    a_mid = ahi_ref[0:1, :].astype(F32)
    g = g_ref[...]
    mid_row = None
    for kb in range(nblk):
        r = slice(kb * blk, (kb + 1) * blk)
        rows = blk + (DFT_PAD_ROWS if kb == nblk - 1 else 0)
        cr = slice(kb * blk, kb * blk + rows)
        yc = jnp.dot(c_ref[cr, :], e_ref[...], preferred_element_type=F32) + sgn_ref[cr, :] * a_mid
        ys = jnp.dot(s_ref[r, :], d_ref[...], preferred_element_type=F32)
        o_ref[r, :] = _rms(yc[:blk] - ys, g).astype(BF16)
        h_ref[r, :] = _rms(yc[:blk] + ys, g).astype(BF16)
        if kb == nblk - 1:
            mid_row = _rms(yc[blk:blk + 1], g)

    for bs in range(nblk):
        r = slice(half + bs * blk, half + (bs + 1) * blk)
        o_ref[r, :] = flipped_block(h_ref, bs, mid_row).astype(BF16)


def _pos_dft_sym(tables, ab, g_four, *, layer):
    cos_m, sin_m, flip_m, sgn = tables
    half = sin_m.shape[0]
    t = ab.shape[0]
    blocks = lambda row, col: pl.BlockSpec((half, FOURIER_WIDTH), lambda b: (2 * b + row, col))
    return pl.pallas_call(
        _pos_dft_sym_kernel,
        grid=(t // (2 * half),),
        in_specs=[
            blocks(0, 0), blocks(1, 0), blocks(0, 1), blocks(1, 1),
            _const_spec(cos_m.shape),
            _const_spec(sin_m.shape),
            _const_spec(flip_m.shape),
            _const_spec(sgn.shape),
            _layer_spec((1, FOURIER_WIDTH), layer),
        ],
        out_specs=pl.BlockSpec((2 * half, FOURIER_WIDTH), lambda b: (b, 0)),
        out_shape=jax.ShapeDtypeStruct((t, FOURIER_WIDTH), BF16),
        scratch_shapes=[pltpu.VMEM((half, FOURIER_WIDTH), BF16)] * 3,
        compiler_params=_params("arbitrary"),
        name="pos_dft_sym",
    )(ab, ab, ab, ab, cos_m, sin_m, flip_m, sgn, g_four)


def _out_proj_kernel(a_ref, f_ref, w_ref, x_ref, mod_ref, gpost_ref, gpre_ref, *rest, next_mod):
    if next_mod:
        (c_ref, wm_ref, bm_ref), rest = rest[:3], rest[3:]
        _mod_block(c_ref, wm_ref, bm_ref, rest[2])
    xo_ref, h_ref = rest[:2]
    mod = mod_ref[0]
    gate_g = mod[2:3] * gpost_ref[...]
    scale_g = (1.0 + mod[4:5]) * gpre_ref[...]
    rows = x_ref.shape[0] // OUT_PROJ_SUB_BLOCKS

    def matmuls(s):
        r = slice(s * rows, (s + 1) * rows)
        return (jnp.dot(a_ref[r, :], w_ref[:ATTN_WIDTH, :], preferred_element_type=F32)
                + jnp.dot(f_ref[r, :], w_ref[ATTN_WIDTH:, :], preferred_element_type=F32))

    pending = matmuls(0)
    for s in range(OUT_PROJ_SUB_BLOCKS):
        r = slice(s * rows, (s + 1) * rows)
        mix = pending
        if s + 1 < OUT_PROJ_SUB_BLOCKS:
            pending = matmuls(s + 1)
        xn = x_ref[r, :] + _rms(mix, gate_g)
        xo_ref[r, :] = xn
        h_ref[r, :] = (_rms(xn, scale_g) + mod[3:4]).astype(BF16)


def _out_proj(a, f, w_out, x2d, mod, g_post, g_pre_ffn, *, layer, mod_row, tm, next_mod=None):
    t, d = x2d.shape
    n_steps = t // tm
    in_specs = [
        pl.BlockSpec((tm, ATTN_WIDTH), lambda i: (i, 0)),
        pl.BlockSpec((tm, FOURIER_WIDTH), lambda i: (i, 0)),
        _const_spec((ATTN_WIDTH + FOURIER_WIDTH, d)),
        pl.BlockSpec((tm, d), lambda i: (i, 0)),
        _mod_spec(d, mod_row),
        _layer_spec((1, d), layer),
        _layer_spec((1, d), layer),
    ]
    out_specs = [pl.BlockSpec((tm, d), lambda i: (i, 0)), pl.BlockSpec((tm, d), lambda i: (i, 0))]
    out_shape = [jax.ShapeDtypeStruct((t, d), F32), jax.ShapeDtypeStruct((t, d), BF16)]
    args = [a, f, w_out, x2d, mod, g_post, g_pre_ffn]
    if next_mod is not None:
        mod_in, mod_out, mod_shape = _mod_block_specs(d, next_mod[1].shape[-1], layer + 1, n_steps)
        in_specs += mod_in
        out_specs.append(mod_out)
        out_shape.append(mod_shape)
        args += list(next_mod)
    return pl.pallas_call(
        functools.partial(_out_proj_kernel, next_mod=next_mod is not None),
        grid=(n_steps,),
        in_specs=in_specs,
        out_specs=out_specs,
        out_shape=out_shape,
        compiler_params=_params("arbitrary"),
        name="out_proj",
    )(*args)


CONV_HALO = 8


def _ffn_up_kernel(h_ref, wg_ref, wv_ref, cw_ref, cb_ref, o_ref, *, seg_len):
    rows, tn = o_ref.shape
    chunk = rows // FFN_UP_ROW_CHUNKS
    pos = lax.broadcasted_iota(jnp.int32, (rows, 1), 0) % seg_len
    first, final = pos == 0, pos == seg_len - 1
    cw = cw_ref[...]
    cb = cb_ref[...]

    def span(s):
        return max(s * chunk - CONV_HALO, 0), min((s + 1) * chunk + CONV_HALO, rows)

    def matmuls(s):
        lo, hi = span(s)
        h = h_ref[lo:hi, :]
        return (jnp.dot(h, wg_ref[...], preferred_element_type=F32),
                jnp.dot(h, wv_ref[...], preferred_element_type=F32))

    def epilogue(s, gate, val):
        lo, hi = span(s)
        prev = jnp.where(first[lo:hi], 0.0, pltpu.roll(gate, 1, axis=0))
        nxt = jnp.where(final[lo:hi], 0.0, pltpu.roll(gate, hi - lo - 1, axis=0))
        c = prev * cw[0:1] + gate * cw[1:2] + nxt * cw[2:3] + cb
        act = 0.5 * c * (1.0 + jnp.tanh(math.sqrt(2.0 / math.pi) * (c + 0.044715 * (c * c * c))))
        own = slice(s * chunk - lo, (s + 1) * chunk - lo)
        o_ref[s * chunk:(s + 1) * chunk, :] = (act * val)[own].astype(BF16)

    pending = matmuls(0)
    for s in range(FFN_UP_ROW_CHUNKS):
        ready = pending
        if s + 1 < FFN_UP_ROW_CHUNKS:
            pending = matmuls(s + 1)
        epilogue(s, *ready)


def _ffn_up(h, w_up, conv_w, conv_b, *, layer, seg_len, tm, tn):
    t, d = h.shape
    ffn = conv_w.shape[-1]
    n_col = ffn // tn
    return pl.pallas_call(
        functools.partial(_ffn_up_kernel, seg_len=seg_len),
        grid=(t // tm, n_col),
        in_specs=[
            pl.BlockSpec((tm, d), lambda i, j: (i, 0)),
            pl.BlockSpec((d, tn), lambda i, j: (0, j)),
            pl.BlockSpec((d, tn), lambda i, j: (0, j + n_col)),
            pl.BlockSpec((None, 3, tn), lambda i, j: (layer, 0, j)),
            pl.BlockSpec((None, 1, tn), lambda i, j: (layer, 0, j)),
        ],
        out_specs=pl.BlockSpec((tm, tn), lambda i, j: (i, j)),
        out_shape=jax.ShapeDtypeStruct((t, ffn), BF16),
        compiler_params=_params("arbitrary", "arbitrary"),
        name="ffn_up",
    )(h, w_up, w_up, conv_w, conv_b)


LANES = 128


def _ffn_down_skew_kernel(u_ref, w_ref, x_ref, mod_ref, gpost_ref, xo_ref, y_ref, inv_ref, ssq_ref):
    i, j = pl.program_id(0), pl.program_id(1)
    n_tiles = pl.num_programs(0) - 1
    n_col, _, tn = y_ref.shape

    @pl.when((i == 0) & (j == 0))
    def _():
        y_ref[...] = jnp.zeros_like(y_ref)
        inv_ref[...] = jnp.zeros_like(inv_ref)

    @pl.when(j == 0)
    def _():
        ssq_ref[...] = jnp.zeros_like(ssq_ref)

    def finish_previous():
        gate_g = mod_ref[0][N_MOD - 1:N_MOD] * gpost_ref[...]
        xo_ref[...] = x_ref[...] + (y_ref[j] * inv_ref[...]) * gate_g

    @pl.when(i < n_tiles)
    def _():
        finish_previous()
        y = jnp.dot(u_ref[...], w_ref[...], preferred_element_type=F32)
        y_ref[j] = y
        sq = y * y
        ssq_ref[...] += functools.reduce(
            jnp.add, [sq[:, c * LANES:(c + 1) * LANES] for c in range(tn // LANES)])

    @pl.when(i == n_tiles)
    def _():
        finish_previous()

    @pl.when((j == n_col - 1) & (i < n_tiles))
    def _():
        ssq = jnp.sum(ssq_ref[...], axis=-1, keepdims=True)
        inv_ref[...] = lax.rsqrt(ssq / (n_col * tn) + EPS)


def _ffn_down_skew(u, w_down, x2d, mod, g_post, *, layer, mod_row, tm, tn):
    t, d = x2d.shape
    ffn = u.shape[1]
    n_tiles = t // tm
    n_col = d // tn
    prev = lambda i: jnp.maximum(i - 1, 0)
    col = lambda i, j: jnp.where(i > 0, j, 0)
    return pl.pallas_call(
        _ffn_down_skew_kernel,
        grid=(n_tiles + 1, n_col),
        in_specs=[
            pl.BlockSpec((tm, ffn), lambda i, j: (jnp.minimum(i, n_tiles - 1), 0)),
            pl.BlockSpec((ffn, tn), lambda i, j: (0, jnp.where(i < n_tiles, j, n_col - 1))),
            pl.BlockSpec((tm, tn), lambda i, j: (prev(i), col(i, j))),
            pl.BlockSpec((1, N_MOD, tn), lambda i, j: (mod_row(prev(i)), 0, col(i, j))),
            pl.BlockSpec((None, 1, tn), lambda i, j: (layer, 0, col(i, j))),
        ],
        out_specs=pl.BlockSpec((tm, tn), lambda i, j: (prev(i), col(i, j))),
        out_shape=jax.ShapeDtypeStruct((t, d), F32),
        scratch_shapes=[pltpu.VMEM((n_col, tm, tn), F32), pltpu.VMEM((tm, 1), F32),
                        pltpu.VMEM((tm, LANES), F32)],
        compiler_params=_params("arbitrary", "arbitrary"),
        name="ffn_down_skew",
    )(u, w_down, x2d, mod, g_post)


def _rope_tables(n):
    pos = np.arange(n)
    n_pairs_axis = HEAD_DIM // 4
    freqs = ROPE_THETA ** (-np.arange(n_pairs_axis, dtype=np.float64) / n_pairs_axis)
    ang = np.concatenate([(pos // GRID_W)[:, None] * freqs, (pos % GRID_W)[:, None] * freqs], axis=-1)
    c, s = np.cos(ang), np.sin(ang)
    return (np.concatenate([c, c], axis=-1).astype(np.float32),
            np.concatenate([-s, s], axis=-1).astype(np.float32))


def _dft_matrices(n):
    idx = np.arange(n)
    ang = ((idx[:, None] * idx[None, :]) % n) * (2.0 * math.pi / n)
    return np.cos(ang) * n ** -0.5, np.sin(ang) * n ** -0.5


def _sym_dft_tables(n):
    half = n // 2
    k = np.arange(half + DFT_PAD_ROWS)[:, None]
    ang = ((k * np.arange(half)[None, :]) % n) * (2.0 * math.pi / n)
    scale = n ** -0.5
    valid = k <= half
    cos_m = jnp.asarray(np.where(valid, np.cos(ang) * scale, 0.0), F32).astype(BF16)
    sin_m = jnp.asarray(np.sin(ang[:half]) * scale, F32).astype(BF16)
    sgn = np.where(valid, (1 - 2 * (k % 2)) * scale, 0.0).astype(np.float32)
    i = np.arange(FLIP_BLOCK)
    flip_m = (i[:, None] + i[None, :] == FLIP_BLOCK).astype(BF16)
    return cos_m, sin_m, flip_m, sgn


def _deinterleave(v):
    return jnp.concatenate([v[..., 0::2], v[..., 1::2]], axis=-1)


def kernel(x, c, ctx, c_ctx, w_mod, b_mod, g_pre_mix, g_post_mix, g_pre_ffn, g_post_ffn,
           w_in, q_norm, k_norm, w_four, g_attn_out, g_four_out, w_out,
           w_up, conv_w, conv_b, w_down):
    b, n, d = x.shape
    ctx_len = ctx.shape[1]
    depth = w_mod.shape[0]

    perm_matrix = _deinterleave(np.eye(HEAD_DIM, dtype=BF16))
    w_in_b = _prep_w_in(w_in, perm_matrix)
    qn = _deinterleave(q_norm).reshape(depth, 1, HEAD_DIM)
    kn = _deinterleave(k_norm).reshape(depth, 1, HEAD_DIM)
    cos_t, sin_t = _rope_tables(n)
    dft_tables = _sym_dft_tables(n)
    cos_c, sin_c = (jnp.asarray(m, F32).astype(BF16) for m in _dft_matrices(ctx_len))
    cs_ch = np.stack(_dft_matrices(FOURIER_GROUP)).astype(np.float32)
    m_fold = _fold_channel_dft(cs_ch, w_four)
    rows3 = lambda v: v.reshape(depth, 1, -1)
    g_pre_mix, g_post_mix, g_pre_ffn, g_post_ffn, g_attn_out, g_four_out, conv_b, b_mod = map(
        rows3, (g_pre_mix, g_post_mix, g_pre_ffn, g_post_ffn, g_attn_out, g_four_out, conv_b, b_mod))

    cc = jnp.zeros((MOD_ROWS, d), F32).at[:b].set(c).at[b].set(c_ctx)
    as_table = lambda m: m.reshape(MOD_ROWS, N_MOD, d)
    mod = as_table(_mod_proj(cc, w_mod, b_mod, layer=0))

    xl = x.reshape(b * n, d)
    xc = ctx.reshape(b * ctx_len, d)
    tm = ROW_TILE
    lat_row = lambda i: i // (n // tm)
    ctx_row = lambda i: b

    for i in range(depth):
        last = i == depth - 1
        q_l, k_l, v_l, ab_l, w_up_b = _in_proj(
            xl, mod, g_pre_mix, w_in_b, qn, kn, cos_t, sin_t, m_fold,
            layer=i, mod_row=lat_row, seq=n, rope=True, tm=tm, cast=(w_up,))
        if last:
            k_c, v_c = _ctx_kv(xc, mod, g_pre_mix, w_in_b, kn, layer=i, mod_row=ctx_row, tm=tm)
        else:
            q_c, k_c, v_c, ab_c = _in_proj(
                xc, mod, g_pre_mix, w_in_b, qn, kn, cos_t, sin_t, m_fold,
                layer=i, mod_row=ctx_row, seq=ctx_len, rope=False, tm=tm)

        attn_l, w_out_b, w_down_b = _attention(
            q_l, [(k_l, v_l, n), (k_c, v_c, ctx_len)], g_attn_out,
            layer=i, q_seq=n, tq=ATTN_Q_TILE, cast=(w_out, w_down))
        four_l = _pos_dft_sym(dft_tables, ab_l, g_four_out, layer=i)
        xl, h_l, *next_mod = _out_proj(
            attn_l, four_l, w_out_b, xl, mod, g_post_mix, g_pre_ffn, layer=i, mod_row=lat_row, tm=tm,
            next_mod=None if last else (cc, w_mod, b_mod))
        u_l = _ffn_up(h_l, w_up_b, conv_w, conv_b, layer=i, seg_len=n, tm=n, tn=FFN_COL_TILE)
        xl = _ffn_down_skew(u_l, w_down_b, xl, mod, g_post_ffn, layer=i,
                            mod_row=lambda r: r // (n // FFN_DOWN_ROW_TILE),
                            tm=FFN_DOWN_ROW_TILE, tn=FFN_COL_TILE)

        if not last:
            attn_c, = _attention(q_c, [(k_c, v_c, ctx_len)], g_attn_out,
                                 layer=i, q_seq=ctx_len, tq=ctx_len)
            four_c = _pos_dft(cos_c, sin_c, ab_c, g_four_out, layer=i, tm=ctx_len)
            xc, h_c = _out_proj(attn_c, four_c, w_out_b, xc, mod, g_post_mix, g_pre_ffn,
                                layer=i, mod_row=ctx_row, tm=tm)
            u_c = _ffn_up(h_c, w_up_b, conv_w, conv_b, layer=i, seg_len=ctx_len,
                          tm=b * ctx_len, tn=FFN_COL_TILE)
            xc = _ffn_down_skew(u_c, w_down_b, xc, mod, g_post_ffn, layer=i, mod_row=ctx_row,
                                tm=FFN_DOWN_ROW_TILE, tn=FFN_COL_TILE)
            mod = as_table(next_mod[0])

    return xl.reshape(b, n, d)
```

```python
import functools
import math

import jax
import jax.numpy as jnp
import numpy as np
from jax import lax
from jax.experimental import pallas as pl
from jax.experimental.pallas import tpu as pltpu

GRID_W = 64
HEAD_DIM = 128
N_Q_HEADS = 8
N_KV_HEADS = 2
Q_PER_KV = N_Q_HEADS // N_KV_HEADS
FOURIER_GROUP = 128
N_FOURIER_GROUPS = 8
ATTN_WIDTH = N_Q_HEADS * HEAD_DIM
KV_WIDTH = N_KV_HEADS * HEAD_DIM
FOURIER_WIDTH = N_FOURIER_GROUPS * FOURIER_GROUP
Q_END = ATTN_WIDTH
K_END = Q_END + KV_WIDTH
V_END = K_END + KV_WIDTH
IN_WIDTH = V_END + FOURIER_WIDTH
ROPE_THETA = 10000.0
EPS = 1e-6
N_MOD = 6
Q_SCALE = HEAD_DIM ** -0.5 * math.log2(math.e)
MOD_ROWS = 16

VMEM_LIMIT_BYTES = 56 * 1024 * 1024

ROW_TILE = 512
ATTN_Q_TILE = 512
ATTN_KEY_CHUNK = 512
FFN_COL_TILE = 512
FFN_DOWN_ROW_TILE = 1024
FFN_UP_ROW_CHUNKS = 2
OUT_PROJ_SUB_BLOCKS = 4
MOD_COL_TILE = 1024
PREP_ROW_TILE = 256

F32 = jnp.float32
BF16 = jnp.bfloat16


def _params(*semantics):
    return pltpu.CompilerParams(dimension_semantics=semantics, vmem_limit_bytes=VMEM_LIMIT_BYTES)


def _rms(x, g):
    ms = jnp.mean(x * x, axis=-1, keepdims=True)
    return x * lax.rsqrt(ms + EPS) * g


def _const_spec(shape):
    return pl.BlockSpec(shape, lambda *_: (0,) * len(shape), pipeline_mode=pl.Buffered(1))


def _layer_spec(shape, layer):
    zeros = (0,) * len(shape)
    return pl.BlockSpec((None,) + shape, lambda *_: (layer,) + zeros, pipeline_mode=pl.Buffered(1))


def _side_casts(sources, layer, n_steps, step_of):
    in_specs, out_specs, out_shapes = [], [], []
    for src in sources:
        _, r, c = src.shape
        rows = r // n_steps
        in_specs.append(pl.BlockSpec((None, rows, c), lambda *g: (layer, step_of(*g), 0)))
        out_specs.append(pl.BlockSpec((rows, c), lambda *g: (step_of(*g), 0)))
        out_shapes.append(jax.ShapeDtypeStruct((r, c), BF16))
    return in_specs, out_specs, out_shapes


def _run_side_casts(src_refs, dst_refs):
    for src, dst in zip(src_refs, dst_refs, strict=True):
        dst[...] = src[...].astype(BF16)


def _mod_spec(d, row_of):
    return pl.BlockSpec((1, N_MOD, d), lambda i, *_: (row_of(i), 0, 0))


def _mod_block(c_ref, w_ref, b_ref, o_ref):
    c = c_ref[...]
    s = (c * jax.nn.sigmoid(c)).astype(BF16)
    o_ref[...] = jnp.dot(s, w_ref[...].astype(BF16), preferred_element_type=F32) + b_ref[...]


def _mod_block_specs(d, width, layer, n_steps):
    tn = width // n_steps
    in_specs = [
        pl.BlockSpec((MOD_ROWS, d), lambda i, *_: (0, 0)),
        pl.BlockSpec((None, d, tn), lambda i, *_: (layer, 0, i)),
        pl.BlockSpec((None, 1, tn), lambda i, *_: (layer, 0, i)),
    ]
    out_spec = pl.BlockSpec((MOD_ROWS, tn), lambda i, *_: (0, i))
    return in_specs, out_spec, jax.ShapeDtypeStruct((MOD_ROWS, width), F32)


def _mod_proj(cc, w_mod, b_mod, *, layer):
    _, d, width = w_mod.shape
    n_steps = width // MOD_COL_TILE
    in_specs, out_spec, out_shape = _mod_block_specs(d, width, layer, n_steps)
    return pl.pallas_call(
        _mod_block,
        grid=(n_steps,),
        in_specs=in_specs,
        out_specs=out_spec,
        out_shape=out_shape,
        compiler_params=_params("arbitrary"),
        name="mod_proj",
    )(cc, w_mod, b_mod)


def _fold_kernel(cs_ref, w_ref, o_ref):
    for g in range(w_ref.shape[1]):
        w = w_ref[0, g]
        o_ref[0, g, :, :FOURIER_GROUP] = jnp.dot(
            cs_ref[0], w, preferred_element_type=F32, precision=lax.Precision.HIGHEST).astype(BF16)
        o_ref[0, g, :, FOURIER_GROUP:] = jnp.dot(
            cs_ref[1], w, preferred_element_type=F32, precision=lax.Precision.HIGHEST).astype(BF16)


def _fold_channel_dft(cs, w_four):
    depth, groups, c, _ = w_four.shape
    return pl.pallas_call(
        _fold_kernel,
        grid=(depth,),
        in_specs=[
            pl.BlockSpec((2, c, c), lambda l: (0, 0, 0)),
            pl.BlockSpec((1, groups, c, c), lambda l: (l, 0, 0, 0)),
        ],
        out_specs=pl.BlockSpec((1, groups, c, 2 * c), lambda l: (l, 0, 0, 0)),
        out_shape=jax.ShapeDtypeStruct((depth, groups, c, 2 * c), BF16),
        compiler_params=_params("arbitrary"),
        name="fold_channel_dft",
    )(cs, w_four)


def _prep_w_in_kernel(w_ref, p_ref, o_ref):
    for head in range(N_Q_HEADS + N_KV_HEADS):
        cols = slice(head * HEAD_DIM, (head + 1) * HEAD_DIM)
        o_ref[0, :, cols] = jnp.dot(w_ref[0, :, cols].astype(BF16), p_ref[...],
                                    preferred_element_type=F32).astype(BF16)
    o_ref[0, :, K_END:] = w_ref[0, :, K_END:].astype(BF16)


def _prep_w_in(w_in, perm_matrix):
    depth, d, width = w_in.shape
    rows = PREP_ROW_TILE
    return pl.pallas_call(
        _prep_w_in_kernel,
        grid=(depth, d // rows),
        in_specs=[
            pl.BlockSpec((1, rows, width), lambda l, j: (l, j, 0)),
            pl.BlockSpec((HEAD_DIM, HEAD_DIM), lambda l, j: (0, 0)),
        ],
        out_specs=pl.BlockSpec((1, rows, width), lambda l, j: (l, j, 0)),
        out_shape=jax.ShapeDtypeStruct((depth, d, width), BF16),
        compiler_params=_params("arbitrary", "arbitrary"),
        name="prep_w_in",
    )(w_in, perm_matrix)


def _in_proj_kernel(x_ref, mod_ref, gpre_ref, w_ref, qn_ref, kn_ref, cos_ref, sin_ref, m_ref, *rest,
                    rope, n_cast):
    cast_src, (q_ref, k_ref, v_ref, ab_ref), cast_dst = rest[:n_cast], rest[n_cast:n_cast + 4], rest[n_cast + 4:]
    _run_side_casts(cast_src, cast_dst)
    mod = mod_ref[0]
    h = _rms(x_ref[...], (1.0 + mod[1:2]) * gpre_ref[...]) + mod[0:1]
    p = jnp.dot(h.astype(BF16), w_ref[...], preferred_element_type=F32)

    def head(col, gain, post_scale):
        y = _rms(p[:, col:col + HEAD_DIM], gain)
        if rope:
            y = y * cos_ref[...] + pltpu.roll(y, HEAD_DIM // 2, axis=1) * sin_ref[...]
        if post_scale != 1.0:
            y = y * post_scale
        return y.astype(BF16)

    for i in range(N_Q_HEADS):
        q_ref[:, i * HEAD_DIM:(i + 1) * HEAD_DIM] = head(i * HEAD_DIM, qn_ref[...], Q_SCALE)
    for i in range(N_KV_HEADS):
        k_ref[:, i * HEAD_DIM:(i + 1) * HEAD_DIM] = head(Q_END + i * HEAD_DIM, kn_ref[...], 1.0)
    v_ref[...] = p[:, K_END:V_END].astype(BF16)
    for g in range(N_FOURIER_GROUPS):
        lo = V_END + g * FOURIER_GROUP
        ab = jnp.dot(p[:, lo:lo + FOURIER_GROUP].astype(BF16), m_ref[g], preferred_element_type=F32)
        ab_ref[:, g * FOURIER_GROUP:(g + 1) * FOURIER_GROUP] = ab[:, :FOURIER_GROUP].astype(BF16)
        ab_ref[:, FOURIER_WIDTH + g * FOURIER_GROUP:FOURIER_WIDTH + (g + 1) * FOURIER_GROUP] = (
            ab[:, FOURIER_GROUP:].astype(BF16))


def _in_proj(x2d, mod, g_pre, w_in, qn, kn, cos_t, sin_t, m_fold, *, layer, mod_row, seq, rope, tm,
             cast=()):
    t, d = x2d.shape
    per_seq = seq // tm
    rope_map = (lambda i: (i % per_seq, 0)) if rope else (lambda i: (0, 0))
    cast_in, cast_out, cast_shapes = _side_casts(cast, layer, t // tm, lambda i: i)
    return pl.pallas_call(
        functools.partial(_in_proj_kernel, rope=rope, n_cast=len(cast)),
        grid=(t // tm,),
        in_specs=[
            pl.BlockSpec((tm, d), lambda i: (i, 0)),
            _mod_spec(d, mod_row),
            _layer_spec((1, d), layer),
            _layer_spec((d, IN_WIDTH), layer),
            _layer_spec((1, HEAD_DIM), layer),
            _layer_spec((1, HEAD_DIM), layer),
            pl.BlockSpec((tm, HEAD_DIM), rope_map),
            pl.BlockSpec((tm, HEAD_DIM), rope_map),
            _layer_spec((N_FOURIER_GROUPS, FOURIER_GROUP, 2 * FOURIER_GROUP), layer),
        ] + cast_in,
        out_specs=[
            pl.BlockSpec((tm, ATTN_WIDTH), lambda i: (i, 0)),
            pl.BlockSpec((tm, KV_WIDTH), lambda i: (i, 0)),
            pl.BlockSpec((tm, KV_WIDTH), lambda i: (i, 0)),
            pl.BlockSpec((tm, 2 * FOURIER_WIDTH), lambda i: (i, 0)),
        ] + cast_out,
        out_shape=[
            jax.ShapeDtypeStruct((t, ATTN_WIDTH), BF16),
            jax.ShapeDtypeStruct((t, KV_WIDTH), BF16),
            jax.ShapeDtypeStruct((t, KV_WIDTH), BF16),
            jax.ShapeDtypeStruct((t, 2 * FOURIER_WIDTH), BF16),
        ] + cast_shapes,
        compiler_params=_params("arbitrary"),
        name="in_proj",
    )(x2d, mod, g_pre, w_in, qn, kn, cos_t, sin_t, m_fold, *cast)


def _ctx_kv_kernel(x_ref, mod_ref, gpre_ref, w_ref, kn_ref, k_ref, v_ref):
    mod = mod_ref[0]
    h = _rms(x_ref[...], (1.0 + mod[1:2]) * gpre_ref[...]) + mod[0:1]
    p = jnp.dot(h.astype(BF16), w_ref[...], preferred_element_type=F32)
    for i in range(N_KV_HEADS):
        cols = slice(i * HEAD_DIM, (i + 1) * HEAD_DIM)
        k_ref[:, cols] = _rms(p[:, cols], kn_ref[...]).astype(BF16)
    v_ref[...] = p[:, KV_WIDTH:].astype(BF16)


def _ctx_kv(x2d, mod, g_pre, w_in, kn, *, layer, mod_row, tm):
    t, d = x2d.shape
    assert Q_END % (2 * KV_WIDTH) == 0
    return pl.pallas_call(
        _ctx_kv_kernel,
        grid=(t // tm,),
        in_specs=[
            pl.BlockSpec((tm, d), lambda i: (i, 0)),
            _mod_spec(d, mod_row),
            _layer_spec((1, d), layer),
            pl.BlockSpec((None, d, 2 * KV_WIDTH), lambda i: (layer, 0, Q_END // (2 * KV_WIDTH)),
                         pipeline_mode=pl.Buffered(1)),
            _layer_spec((1, HEAD_DIM), layer),
        ],
        out_specs=[pl.BlockSpec((tm, KV_WIDTH), lambda i: (i, 0))] * 2,
        out_shape=[jax.ShapeDtypeStruct((t, KV_WIDTH), BF16)] * 2,
        compiler_params=_params("arbitrary"),
        name="ctx_kv",
    )(x2d, mod, g_pre, w_in, kn)


def _attn_kernel(*refs, n_seg, n_cast):
    q_ref = refs[0]
    kv_refs = refs[1:1 + 2 * n_seg]
    g_ref = refs[1 + 2 * n_seg]
    rest = refs[2 + 2 * n_seg:]
    cast_src, o_ref, cast_dst, acc_ref = rest[:n_cast], rest[n_cast], rest[n_cast + 1:-1], rest[-1]
    _run_side_casts(cast_src, cast_dst)

    chunks = []
    for i in range(n_seg):
        total = kv_refs[2 * i].shape[0]
        for lo in range(0, total, ATTN_KEY_CHUNK):
            chunks.append((kv_refs[2 * i], kv_refs[2 * i + 1], lo, min(ATTN_KEY_CHUNK, total - lo)))

    def chunk_scores(h, c):
        k_ref, _, lo, size = chunks[c]
        kv = (h // Q_PER_KV) * HEAD_DIM
        return lax.dot_general(k_ref[lo:lo + size, kv:kv + HEAD_DIM], q_ref[:, h * HEAD_DIM:(h + 1) * HEAD_DIM],
                               (((1,), (1,)), ((), ())), preferred_element_type=F32)

    def chunk_output(h, c, p):
        _, v_ref, lo, size = chunks[c]
        kv = (h // Q_PER_KV) * HEAD_DIM
        return lax.dot_general(v_ref[lo:lo + size, kv:kv + HEAD_DIM], p,
                               (((0,), (0,)), ((), ())), preferred_element_type=F32)

    def col_max(m, s):
        cm = jnp.max(s, axis=0, keepdims=True)
        return cm if m is None else jnp.maximum(m, cm)

    scores, m = [], None
    for c in range(len(chunks)):
        scores.append(chunk_scores(0, c))
        m = col_max(m, scores[-1])
    for h in range(N_Q_HEADS):
        next_scores, next_m = [], None
        denom, out_t = None, None
        for c in range(len(chunks)):
            if h + 1 < N_Q_HEADS:
                next_scores.append(chunk_scores(h + 1, c))
                next_m = col_max(next_m, next_scores[-1])
            p = jnp.exp2(scores[c] - m)
            ps = jnp.sum(p, axis=0, keepdims=True)
            po = chunk_output(h, c, p.astype(BF16))
            denom = ps if denom is None else denom + ps
            out_t = po if out_t is None else out_t + po
        acc_ref[:, h * HEAD_DIM:(h + 1) * HEAD_DIM] = (out_t / denom).T
        scores, m = next_scores, next_m
    o_ref[...] = _rms(acc_ref[...], g_ref[...]).astype(BF16)


def _attention(q, segments, g_attn, *, layer, q_seq, tq, cast=()):
    t = q.shape[0]
    per_seq = q_seq // tq
    in_specs = [pl.BlockSpec((tq, ATTN_WIDTH), lambda b, j: (b * per_seq + j, 0))]
    args = [q]
    for k, v, rows in segments:
        in_specs += [pl.BlockSpec((rows, KV_WIDTH), lambda b, j: (b, 0))] * 2
        args += [k, v]
    in_specs.append(_layer_spec((1, ATTN_WIDTH), layer))
    args.append(g_attn)
    cast_in, cast_out, cast_shapes = _side_casts(cast, layer, t // tq, lambda b, j: b * per_seq + j)
    return pl.pallas_call(
        functools.partial(_attn_kernel, n_seg=len(segments), n_cast=len(cast)),
        grid=(t // q_seq, per_seq),
        in_specs=in_specs + cast_in,
        out_specs=[pl.BlockSpec((tq, ATTN_WIDTH), lambda b, j: (b * per_seq + j, 0))] + cast_out,
        out_shape=[jax.ShapeDtypeStruct((t, ATTN_WIDTH), BF16)] + cast_shapes,
        scratch_shapes=[pltpu.VMEM((tq, ATTN_WIDTH), F32)],
        compiler_params=_params("arbitrary", "arbitrary"),
        name="attention",
    )(*args, *cast)


def _pos_dft_kernel(c_ref, s_ref, a_ref, b_ref, g_ref, o_ref):
    y = (jnp.dot(c_ref[...], a_ref[...], preferred_element_type=F32)
         - jnp.dot(s_ref[...], b_ref[...], preferred_element_type=F32))
    o_ref[...] = _rms(y, g_ref[...]).astype(BF16)


def _pos_dft(cos_m, sin_m, ab, g_four, *, layer, tm):
    seq = cos_m.shape[0]
    t = ab.shape[0]
    per_seq = seq // tm
    return pl.pallas_call(
        _pos_dft_kernel,
        grid=(t // seq, per_seq),
        in_specs=[
            pl.BlockSpec((tm, seq), lambda b, j: (j, 0)),
            pl.BlockSpec((tm, seq), lambda b, j: (j, 0)),
            pl.BlockSpec((seq, FOURIER_WIDTH), lambda b, j: (b, 0)),
            pl.BlockSpec((seq, FOURIER_WIDTH), lambda b, j: (b, 1)),
            _layer_spec((1, FOURIER_WIDTH), layer),
        ],
        out_specs=pl.BlockSpec((tm, FOURIER_WIDTH), lambda b, j: (b * per_seq + j, 0)),
        out_shape=jax.ShapeDtypeStruct((t, FOURIER_WIDTH), BF16),
        compiler_params=_params("arbitrary", "arbitrary"),
        name="pos_dft",
    )(cos_m, sin_m, ab, ab, g_four)


FLIP_BLOCK = 256
DFT_PAD_ROWS = 16


def _pos_dft_sym_kernel(alo_ref, ahi_ref, blo_ref, bhi_ref, c_ref, s_ref, q_ref, sgn_ref, g_ref,
                        o_ref, e_ref, d_ref, h_ref):
    half = alo_ref.shape[0]
    blk = q_ref.shape[0]
    nblk = half // blk
    qm = q_ref[...]
    is_row0 = lax.broadcasted_iota(jnp.int32, (blk, 1), 0) == 0

    def flipped_block(src_ref, bt, first_row):
        lo = (nblk - 1 - bt) * blk
        y = jnp.dot(qm, src_ref[lo:lo + blk, :], preferred_element_type=F32)
        if bt > 0:
            first_row = src_ref[lo + blk:lo + blk + 1, :].astype(F32)
        return jnp.where(is_row0, first_row, y)

    for bt in range(nblk):
        r = slice(bt * blk, (bt + 1) * blk)
        e_ref[r, :] = (alo_ref[r, :].astype(F32) + flipped_block(ahi_ref, bt, 0.0)).astype(BF16)
        d_ref[r, :] = (blo_ref[r, :].astype(F32) - flipped_block(bhi_ref, bt, 0.0)).astype(BF16)

    a_mid = ahi_ref[0:1, :].astype(F32)
    g = g_ref[...]
    mid_row = None
    for kb in range(nblk):
        r = slice(kb * blk, (kb + 1) * blk)
        rows = blk + (DFT_PAD_ROWS if kb == nblk - 1 else 0)
        cr = slice(kb * blk, kb * blk + rows)
        yc = jnp.dot(c_ref[cr, :], e_ref[...], preferred_element_type=F32) + sgn_ref[cr, :] * a_mid
        ys = jnp.dot(s_ref[r, :], d_ref[...], preferred_element_type=F32)
        o_ref[r, :] = _rms(yc[:blk] - ys, g).astype(BF16)
        h_ref[r, :] = _rms(yc[:blk] + ys, g).astype(BF16)
        if kb == nblk - 1:
            mid_row = _rms(yc[blk:blk + 1], g)

    for bs in range(nblk):
        r = slice(half + bs * blk, half + (bs + 1) * blk)
        o_ref[r, :] = flipped_block(h_ref, bs, mid_row).astype(BF16)


def _pos_dft_sym(tables, ab, g_four, *, layer):
    cos_m, sin_m, flip_m, sgn = tables
    half = sin_m.shape[0]
    t = ab.shape[0]
    blocks = lambda row, col: pl.BlockSpec((half, FOURIER_WIDTH), lambda b: (2 * b + row, col))
    return pl.pallas_call(
        _pos_dft_sym_kernel,
        grid=(t // (2 * half),),
        in_specs=[
            blocks(0, 0), blocks(1, 0), blocks(0, 1), blocks(1, 1),
            _const_spec(cos_m.shape),
            _const_spec(sin_m.shape),
            _const_spec(flip_m.shape),
            _const_spec(sgn.shape),
            _layer_spec((1, FOURIER_WIDTH), layer),
        ],
        out_specs=pl.BlockSpec((2 * half, FOURIER_WIDTH), lambda b: (b, 0)),
        out_shape=jax.ShapeDtypeStruct((t, FOURIER_WIDTH), BF16),
        scratch_shapes=[pltpu.VMEM((half, FOURIER_WIDTH), BF16)] * 3,
        compiler_params=_params("arbitrary"),
        name="pos_dft_sym",
    )(ab, ab, ab, ab, cos_m, sin_m, flip_m, sgn, g_four)


def _out_proj_kernel(a_ref, f_ref, w_ref, x_ref, mod_ref, gpost_ref, gpre_ref, *rest, next_mod):
    if next_mod:
        (c_ref, wm_ref, bm_ref), rest = rest[:3], rest[3:]
        _mod_block(c_ref, wm_ref, bm_ref, rest[2])
    xo_ref, h_ref = rest[:2]
    mod = mod_ref[0]
    gate_g = mod[2:3] * gpost_ref[...]
    scale_g = (1.0 + mod[4:5]) * gpre_ref[...]
    rows = x_ref.shape[0] // OUT_PROJ_SUB_BLOCKS

    def matmuls(s):
        r = slice(s * rows, (s + 1) * rows)
        return (jnp.dot(a_ref[r, :], w_ref[:ATTN_WIDTH, :], preferred_element_type=F32)
                + jnp.dot(f_ref[r, :], w_ref[ATTN_WIDTH:, :], preferred_element_type=F32))

    pending = matmuls(0)
    for s in range(OUT_PROJ_SUB_BLOCKS):
        r = slice(s * rows, (s + 1) * rows)
        mix = pending
        if s + 1 < OUT_PROJ_SUB_BLOCKS:
            pending = matmuls(s + 1)
        xn = x_ref[r, :] + _rms(mix, gate_g)
        xo_ref[r, :] = xn
        h_ref[r, :] = (_rms(xn, scale_g) + mod[3:4]).astype(BF16)


def _out_proj(a, f, w_out, x2d, mod, g_post, g_pre_ffn, *, layer, mod_row, tm, next_mod=None):
    t, d = x2d.shape
    n_steps = t // tm
    in_specs = [
        pl.BlockSpec((tm, ATTN_WIDTH), lambda i: (i, 0)),
        pl.BlockSpec((tm, FOURIER_WIDTH), lambda i: (i, 0)),
        _const_spec((ATTN_WIDTH + FOURIER_WIDTH, d)),
        pl.BlockSpec((tm, d), lambda i: (i, 0)),
        _mod_spec(d, mod_row),
        _layer_spec((1, d), layer),
        _layer_spec((1, d), layer),
    ]
    out_specs = [pl.BlockSpec((tm, d), lambda i: (i, 0)), pl.BlockSpec((tm, d), lambda i: (i, 0))]
    out_shape = [jax.ShapeDtypeStruct((t, d), F32), jax.ShapeDtypeStruct((t, d), BF16)]
    args = [a, f, w_out, x2d, mod, g_post, g_pre_ffn]
    if next_mod is not None:
        mod_in, mod_out, mod_shape = _mod_block_specs(d, next_mod[1].shape[-1], layer + 1, n_steps)
        in_specs += mod_in
        out_specs.append(mod_out)
        out_shape.append(mod_shape)
        args += list(next_mod)
    return pl.pallas_call(
        functools.partial(_out_proj_kernel, next_mod=next_mod is not None),
        grid=(n_steps,),
        in_specs=in_specs,
        out_specs=out_specs,
        out_shape=out_shape,
        compiler_params=_params("arbitrary"),
        name="out_proj",
    )(*args)


CONV_HALO = 8


def _ffn_up_kernel(h_ref, wg_ref, wv_ref, cw_ref, cb_ref, o_ref, *, seg_len):
    rows, tn = o_ref.shape
    chunk = rows // FFN_UP_ROW_CHUNKS
    pos = lax.broadcasted_iota(jnp.int32, (rows, 1), 0) % seg_len
    first, final = pos == 0, pos == seg_len - 1
    cw = cw_ref[...]
    cb = cb_ref[...]

    def span(s):
        return max(s * chunk - CONV_HALO, 0), min((s + 1) * chunk + CONV_HALO, rows)

    def matmuls(s):
        lo, hi = span(s)
        h = h_ref[lo:hi, :]
        return (jnp.dot(h, wg_ref[...], preferred_element_type=F32),
                jnp.dot(h, wv_ref[...], preferred_element_type=F32))

    def epilogue(s, gate, val):
        lo, hi = span(s)
        prev = jnp.where(first[lo:hi], 0.0, pltpu.roll(gate, 1, axis=0))
        nxt = jnp.where(final[lo:hi], 0.0, pltpu.roll(gate, hi - lo - 1, axis=0))
        c = prev * cw[0:1] + gate * cw[1:2] + nxt * cw[2:3] + cb
        act = 0.5 * c * (1.0 + jnp.tanh(math.sqrt(2.0 / math.pi) * (c + 0.044715 * (c * c * c))))
        own = slice(s * chunk - lo, (s + 1) * chunk - lo)
        o_ref[s * chunk:(s + 1) * chunk, :] = (act * val)[own].astype(BF16)

    pending = matmuls(0)
    for s in range(FFN_UP_ROW_CHUNKS):
        ready = pending
        if s + 1 < FFN_UP_ROW_CHUNKS:
            pending = matmuls(s + 1)
        epilogue(s, *ready)


def _ffn_up(h, w_up, conv_w, conv_b, *, layer, seg_len, tm, tn):
    t, d = h.shape
    ffn = conv_w.shape[-1]
    n_col = ffn // tn
    return pl.pallas_call(
        functools.partial(_ffn_up_kernel, seg_len=seg_len),
        grid=(t // tm, n_col),
        in_specs=[
            pl.BlockSpec((tm, d), lambda i, j: (i, 0)),
            pl.BlockSpec((d, tn), lambda i, j: (0, j)),
            pl.BlockSpec((d, tn), lambda i, j: (0, j + n_col)),
            pl.BlockSpec((None, 3, tn), lambda i, j: (layer, 0, j)),
            pl.BlockSpec((None, 1, tn), lambda i, j: (layer, 0, j)),
        ],
        out_specs=pl.BlockSpec((tm, tn), lambda i, j: (i, j)),
        out_shape=jax.ShapeDtypeStruct((t, ffn), BF16),
        compiler_params=_params("arbitrary", "arbitrary"),
        name="ffn_up",
    )(h, w_up, w_up, conv_w, conv_b)


LANES = 128


def _ffn_down_skew_kernel(u_ref, w_ref, x_ref, mod_ref, gpost_ref, xo_ref, y_ref, inv_ref, ssq_ref):
    i, j = pl.program_id(0), pl.program_id(1)
    n_tiles = pl.num_programs(0) - 1
    n_col, _, tn = y_ref.shape

    @pl.when((i == 0) & (j == 0))
    def _():
        y_ref[...] = jnp.zeros_like(y_ref)
        inv_ref[...] = jnp.zeros_like(inv_ref)

    @pl.when(j == 0)
    def _():
        ssq_ref[...] = jnp.zeros_like(ssq_ref)

    def finish_previous():
        gate_g = mod_ref[0][N_MOD - 1:N_MOD] * gpost_ref[...]
        xo_ref[...] = x_ref[...] + (y_ref[j] * inv_ref[...]) * gate_g

    @pl.when(i < n_tiles)
    def _():
        finish_previous()
        y = jnp.dot(u_ref[...], w_ref[...], preferred_element_type=F32)
        y_ref[j] = y
        sq = y * y
        ssq_ref[...] += functools.reduce(
            jnp.add, [sq[:, c * LANES:(c + 1) * LANES] for c in range(tn // LANES)])

    @pl.when(i == n_tiles)
    def _():
        finish_previous()

    @pl.when((j == n_col - 1) & (i < n_tiles))
    def _():
        ssq = jnp.sum(ssq_ref[...], axis=-1, keepdims=True)
        inv_ref[...] = lax.rsqrt(ssq / (n_col * tn) + EPS)


def _ffn_down_skew(u, w_down, x2d, mod, g_post, *, layer, mod_row, tm, tn):
    t, d = x2d.shape
    ffn = u.shape[1]
    n_tiles = t // tm
    n_col = d // tn
    prev = lambda i: jnp.maximum(i - 1, 0)
    col = lambda i, j: jnp.where(i > 0, j, 0)
    return pl.pallas_call(
        _ffn_down_skew_kernel,
        grid=(n_tiles + 1, n_col),
        in_specs=[
            pl.BlockSpec((tm, ffn), lambda i, j: (jnp.minimum(i, n_tiles - 1), 0)),
            pl.BlockSpec((ffn, tn), lambda i, j: (0, jnp.where(i < n_tiles, j, n_col - 1))),
            pl.BlockSpec((tm, tn), lambda i, j: (prev(i), col(i, j))),
            pl.BlockSpec((1, N_MOD, tn), lambda i, j: (mod_row(prev(i)), 0, col(i, j))),
            pl.BlockSpec((None, 1, tn), lambda i, j: (layer, 0, col(i, j))),
        ],
        out_specs=pl.BlockSpec((tm, tn), lambda i, j: (prev(i), col(i, j))),
        out_shape=jax.ShapeDtypeStruct((t, d), F32),
        scratch_shapes=[pltpu.VMEM((n_col, tm, tn), F32), pltpu.VMEM((tm, 1), F32),
                        pltpu.VMEM((tm, LANES), F32)],
        compiler_params=_params("arbitrary", "arbitrary"),
        name="ffn_down_skew",
    )(u, w_down, x2d, mod, g_post)


def _rope_tables(n):
    pos = np.arange(n)
    n_pairs_axis = HEAD_DIM // 4
    freqs = ROPE_THETA ** (-np.arange(n_pairs_axis, dtype=np.float64) / n_pairs_axis)
    ang = np.concatenate([(pos // GRID_W)[:, None] * freqs, (pos % GRID_W)[:, None] * freqs], axis=-1)
    c, s = np.cos(ang), np.sin(ang)
    return (np.concatenate([c, c], axis=-1).astype(np.float32),
            np.concatenate([-s, s], axis=-1).astype(np.float32))


def _dft_matrices(n):
    idx = np.arange(n)
    ang = ((idx[:, None] * idx[None, :]) % n) * (2.0 * math.pi / n)
    return np.cos(ang) * n ** -0.5, np.sin(ang) * n ** -0.5


def _sym_dft_tables(n):
    half = n // 2
    k = np.arange(half + DFT_PAD_ROWS)[:, None]
    ang = ((k * np.arange(half)[None, :]) % n) * (2.0 * math.pi / n)
    scale = n ** -0.5
    valid = k <= half
    cos_m = jnp.asarray(np.where(valid, np.cos(ang) * scale, 0.0), F32).astype(BF16)
    sin_m = jnp.asarray(np.sin(ang[:half]) * scale, F32).astype(BF16)
    sgn = np.where(valid, (1 - 2 * (k % 2)) * scale, 0.0).astype(np.float32)
    i = np.arange(FLIP_BLOCK)
    flip_m = (i[:, None] + i[None, :] == FLIP_BLOCK).astype(BF16)
    return cos_m, sin_m, flip_m, sgn


def _deinterleave(v):
    return jnp.concatenate([v[..., 0::2], v[..., 1::2]], axis=-1)


def kernel(x, c, ctx, c_ctx, w_mod, b_mod, g_pre_mix, g_post_mix, g_pre_ffn, g_post_ffn,
           w_in, q_norm, k_norm, w_four, g_attn_out, g_four_out, w_out,
           w_up, conv_w, conv_b, w_down):
    b, n, d = x.shape
    ctx_len = ctx.shape[1]
    depth = w_mod.shape[0]

    perm_matrix = _deinterleave(np.eye(HEAD_DIM, dtype=BF16))
    w_in_b = _prep_w_in(w_in, perm_matrix)
    qn = _deinterleave(q_norm).reshape(depth, 1, HEAD_DIM)
    kn = _deinterleave(k_norm).reshape(depth, 1, HEAD_DIM)
    cos_t, sin_t = _rope_tables(n)
    dft_tables = _sym_dft_tables(n)
    cos_c, sin_c = (jnp.asarray(m, F32).astype(BF16) for m in _dft_matrices(ctx_len))
    cs_ch = np.stack(_dft_matrices(FOURIER_GROUP)).astype(np.float32)
    m_fold = _fold_channel_dft(cs_ch, w_four)
    rows3 = lambda v: v.reshape(depth, 1, -1)
    g_pre_mix, g_post_mix, g_pre_ffn, g_post_ffn, g_attn_out, g_four_out, conv_b, b_mod = map(
        rows3, (g_pre_mix, g_post_mix, g_pre_ffn, g_post_ffn, g_attn_out, g_four_out, conv_b, b_mod))

    cc = jnp.zeros((MOD_ROWS, d), F32).at[:b].set(c).at[b].set(c_ctx)
    as_table = lambda m: m.reshape(MOD_ROWS, N_MOD, d)
    mod = as_table(_mod_proj(cc, w_mod, b_mod, layer=0))

    xl = x.reshape(b * n, d)
    xc = ctx.reshape(b * ctx_len, d)
    tm = ROW_TILE
    lat_row = lambda i: i // (n // tm)
    ctx_row = lambda i: b

    for i in range(depth):
        last = i == depth - 1
        q_l, k_l, v_l, ab_l, w_up_b = _in_proj(
            xl, mod, g_pre_mix, w_in_b, qn, kn, cos_t, sin_t, m_fold,
            layer=i, mod_row=lat_row, seq=n, rope=True, tm=tm, cast=(w_up,))
        if last:
            k_c, v_c = _ctx_kv(xc, mod, g_pre_mix, w_in_b, kn, layer=i, mod_row=ctx_row, tm=tm)
        else:
            q_c, k_c, v_c, ab_c = _in_proj(
                xc, mod, g_pre_mix, w_in_b, qn, kn, cos_t, sin_t, m_fold,
                layer=i, mod_row=ctx_row, seq=ctx_len, rope=False, tm=tm)

        attn_l, w_out_b, w_down_b = _attention(
            q_l, [(k_l, v_l, n), (k_c, v_c, ctx_len)], g_attn_out,
            layer=i, q_seq=n, tq=ATTN_Q_TILE, cast=(w_out, w_down))
        four_l = _pos_dft_sym(dft_tables, ab_l, g_four_out, layer=i)
        xl, h_l, *next_mod = _out_proj(
            attn_l, four_l, w_out_b, xl, mod, g_post_mix, g_pre_ffn, layer=i, mod_row=lat_row, tm=tm,
            next_mod=None if last else (cc, w_mod, b_mod))
        u_l = _ffn_up(h_l, w_up_b, conv_w, conv_b, layer=i, seg_len=n, tm=n, tn=FFN_COL_TILE)
        xl = _ffn_down_skew(u_l, w_down_b, xl, mod, g_post_ffn, layer=i,
                            mod_row=lambda r: r // (n // FFN_DOWN_ROW_TILE),
                            tm=FFN_DOWN_ROW_TILE, tn=FFN_COL_TILE)

        if not last:
            attn_c, = _attention(q_c, [(k_c, v_c, ctx_len)], g_attn_out,
                                 layer=i, q_seq=ctx_len, tq=ctx_len)
            four_c = _pos_dft(cos_c, sin_c, ab_c, g_four_out, layer=i, tm=ctx_len)
            xc, h_c = _out_proj(attn_c, four_c, w_out_b, xc, mod, g_post_mix, g_pre_ffn,
                                layer=i, mod_row=ctx_row, tm=tm)
            u_c = _ffn_up(h_c, w_up_b, conv_w, conv_b, layer=i, seg_len=ctx_len,
                          tm=b * ctx_len, tn=FFN_COL_TILE)
            xc = _ffn_down_skew(u_c, w_down_b, xc, mod, g_post_ffn, layer=i, mod_row=ctx_row,
                                tm=FFN_DOWN_ROW_TILE, tn=FFN_COL_TILE)
            mod = as_table(next_mod[0])

    return xl.reshape(b, n, d)
```

```python
import functools
import math

import jax
import jax.numpy as jnp
import numpy as np
from jax import lax
from jax.experimental import pallas as pl
from jax.experimental.pallas import tpu as pltpu

GRID_W = 64
HEAD_DIM = 128
N_Q_HEADS = 8
N_KV_HEADS = 2
Q_PER_KV = N_Q_HEADS // N_KV_HEADS
FOURIER_GROUP = 128
N_FOURIER_GROUPS = 8
ATTN_WIDTH = N_Q_HEADS * HEAD_DIM
KV_WIDTH = N_KV_HEADS * HEAD_DIM
FOURIER_WIDTH = N_FOURIER_GROUPS * FOURIER_GROUP
Q_END = ATTN_WIDTH
K_END = Q_END + KV_WIDTH
V_END = K_END + KV_WIDTH
IN_WIDTH = V_END + FOURIER_WIDTH
ROPE_THETA = 10000.0
EPS = 1e-6
N_MOD = 6
Q_SCALE = HEAD_DIM ** -0.5 * math.log2(math.e)
MOD_ROWS = 16

VMEM_LIMIT_BYTES = 56 * 1024 * 1024

ROW_TILE = 512
ATTN_Q_TILE = 1024
ATTN_KEY_CHUNK = 512
FFN_COL_TILE = 512
FFN_DOWN_ROW_TILE = 1024
W_DOWN_BUFFERS = 3
FFN_UP_ROW_CHUNKS = 2
OUT_PROJ_SUB_BLOCKS = 4
MOD_COL_TILE = 1024
PREP_ROW_TILE = 256

F32 = jnp.float32
BF16 = jnp.bfloat16


def _params(*semantics):
    return pltpu.CompilerParams(dimension_semantics=semantics, vmem_limit_bytes=VMEM_LIMIT_BYTES)


def _rms(x, g):
    ms = jnp.mean(x * x, axis=-1, keepdims=True)
    return x * lax.rsqrt(ms + EPS) * g


def _const_spec(shape):
    return pl.BlockSpec(shape, lambda *_: (0,) * len(shape), pipeline_mode=pl.Buffered(1))


def _layer_spec(shape, layer):
    zeros = (0,) * len(shape)
    return pl.BlockSpec((None,) + shape, lambda *_: (layer,) + zeros, pipeline_mode=pl.Buffered(1))


def _side_casts(sources, layer, n_steps, step_of):
    in_specs, out_specs, out_shapes = [], [], []
    for src in sources:
        _, r, c = src.shape
        rows = r // n_steps
        in_specs.append(pl.BlockSpec((None, rows, c), lambda *g: (layer, step_of(*g), 0)))
        out_specs.append(pl.BlockSpec((rows, c), lambda *g: (step_of(*g), 0)))
        out_shapes.append(jax.ShapeDtypeStruct((r, c), BF16))
    return in_specs, out_specs, out_shapes


def _run_side_casts(src_refs, dst_refs):
    for src, dst in zip(src_refs, dst_refs, strict=True):
        dst[...] = src[...].astype(BF16)


def _mod_spec(d, row_of):
    return pl.BlockSpec((1, N_MOD, d), lambda i, *_: (row_of(i), 0, 0))


def _mod_block(c_ref, w_ref, b_ref, o_ref):
    c = c_ref[...]
    s = (c * jax.nn.sigmoid(c)).astype(BF16)
    o_ref[...] = jnp.dot(s, w_ref[...].astype(BF16), preferred_element_type=F32) + b_ref[...]


def _mod_block_specs(d, width, layer, n_steps):
    tn = width // n_steps
    in_specs = [
        pl.BlockSpec((MOD_ROWS, d), lambda i, *_: (0, 0)),
        pl.BlockSpec((None, d, tn), lambda i, *_: (layer, 0, i)),
        pl.BlockSpec((None, 1, tn), lambda i, *_: (layer, 0, i)),
    ]
    out_spec = pl.BlockSpec((MOD_ROWS, tn), lambda i, *_: (0, i))
    return in_specs, out_spec, jax.ShapeDtypeStruct((MOD_ROWS, width), F32)


def _mod_proj(cc, w_mod, b_mod, *, layer):
    _, d, width = w_mod.shape
    n_steps = width // MOD_COL_TILE
    in_specs, out_spec, out_shape = _mod_block_specs(d, width, layer, n_steps)
    return pl.pallas_call(
        _mod_block,
        grid=(n_steps,),
        in_specs=in_specs,
        out_specs=out_spec,
        out_shape=out_shape,
        compiler_params=_params("arbitrary"),
        name="mod_proj",
    )(cc, w_mod, b_mod)


def _fold_kernel(cs_ref, w_ref, o_ref):
    for g in range(w_ref.shape[1]):
        w = w_ref[0, g]
        o_ref[0, g, :, :FOURIER_GROUP] = jnp.dot(
            cs_ref[0], w, preferred_element_type=F32, precision=lax.Precision.HIGHEST).astype(BF16)
        o_ref[0, g, :, FOURIER_GROUP:] = jnp.dot(
            cs_ref[1], w, preferred_element_type=F32, precision=lax.Precision.HIGHEST).astype(BF16)


def _fold_channel_dft(cs, w_four):
    depth, groups, c, _ = w_four.shape
    return pl.pallas_call(
        _fold_kernel,
        grid=(depth,),
        in_specs=[
            pl.BlockSpec((2, c, c), lambda l: (0, 0, 0)),
            pl.BlockSpec((1, groups, c, c), lambda l: (l, 0, 0, 0)),
        ],
        out_specs=pl.BlockSpec((1, groups, c, 2 * c), lambda l: (l, 0, 0, 0)),
        out_shape=jax.ShapeDtypeStruct((depth, groups, c, 2 * c), BF16),
        compiler_params=_params("arbitrary"),
        name="fold_channel_dft",
    )(cs, w_four)


def _prep_w_in_kernel(w_ref, p_ref, o_ref):
    for head in range(N_Q_HEADS + N_KV_HEADS):
        cols = slice(head * HEAD_DIM, (head + 1) * HEAD_DIM)
        o_ref[0, :, cols] = jnp.dot(w_ref[0, :, cols].astype(BF16), p_ref[...],
                                    preferred_element_type=F32).astype(BF16)
    o_ref[0, :, K_END:] = w_ref[0, :, K_END:].astype(BF16)


def _prep_w_in(w_in, perm_matrix):
    depth, d, width = w_in.shape
    rows = PREP_ROW_TILE
    return pl.pallas_call(
        _prep_w_in_kernel,
        grid=(depth, d // rows),
        in_specs=[
            pl.BlockSpec((1, rows, width), lambda l, j: (l, j, 0)),
            pl.BlockSpec((HEAD_DIM, HEAD_DIM), lambda l, j: (0, 0)),
        ],
        out_specs=pl.BlockSpec((1, rows, width), lambda l, j: (l, j, 0)),
        out_shape=jax.ShapeDtypeStruct((depth, d, width), BF16),
        compiler_params=_params("arbitrary", "arbitrary"),
        name="prep_w_in",
    )(w_in, perm_matrix)


def _in_proj_kernel(x_ref, mod_ref, gpre_ref, w_ref, qn_ref, kn_ref, cos_ref, sin_ref, m_ref, *rest,
                    rope, n_cast):
    cast_src, (q_ref, k_ref, v_ref, ab_ref), cast_dst = rest[:n_cast], rest[n_cast:n_cast + 4], rest[n_cast + 4:]
    _run_side_casts(cast_src, cast_dst)
    mod = mod_ref[0]
    h = _rms(x_ref[...], (1.0 + mod[1:2]) * gpre_ref[...]) + mod[0:1]
    p = jnp.dot(h.astype(BF16), w_ref[...], preferred_element_type=F32)

    def head(col, gain, post_scale):
        y = _rms(p[:, col:col + HEAD_DIM], gain)
        if rope:
            y = y * cos_ref[...] + pltpu.roll(y, HEAD_DIM // 2, axis=1) * sin_ref[...]
        if post_scale != 1.0:
            y = y * post_scale
        return y.astype(BF16)

    for i in range(N_Q_HEADS):
        q_ref[:, i * HEAD_DIM:(i + 1) * HEAD_DIM] = head(i * HEAD_DIM, qn_ref[...], Q_SCALE)
    for i in range(N_KV_HEADS):
        k_ref[:, i * HEAD_DIM:(i + 1) * HEAD_DIM] = head(Q_END + i * HEAD_DIM, kn_ref[...], 1.0)
    v_ref[...] = p[:, K_END:V_END].astype(BF16)
    for g in range(N_FOURIER_GROUPS):
        lo = V_END + g * FOURIER_GROUP
        ab = jnp.dot(p[:, lo:lo + FOURIER_GROUP].astype(BF16), m_ref[g], preferred_element_type=F32)
        ab_ref[:, g * FOURIER_GROUP:(g + 1) * FOURIER_GROUP] = ab[:, :FOURIER_GROUP].astype(BF16)
        ab_ref[:, FOURIER_WIDTH + g * FOURIER_GROUP:FOURIER_WIDTH + (g + 1) * FOURIER_GROUP] = (
            ab[:, FOURIER_GROUP:].astype(BF16))


def _in_proj(x2d, mod, g_pre, w_in, qn, kn, cos_t, sin_t, m_fold, *, layer, mod_row, seq, rope, tm,
             cast=()):
    t, d = x2d.shape
    per_seq = seq // tm
    rope_map = (lambda i: (i % per_seq, 0)) if rope else (lambda i: (0, 0))
    cast_in, cast_out, cast_shapes = _side_casts(cast, layer, t // tm, lambda i: i)
    return pl.pallas_call(
        functools.partial(_in_proj_kernel, rope=rope, n_cast=len(cast)),
        grid=(t // tm,),
        in_specs=[
            pl.BlockSpec((tm, d), lambda i: (i, 0)),
            _mod_spec(d, mod_row),
            _layer_spec((1, d), layer),
            _layer_spec((d, IN_WIDTH), layer),
            _layer_spec((1, HEAD_DIM), layer),
            _layer_spec((1, HEAD_DIM), layer),
            pl.BlockSpec((tm, HEAD_DIM), rope_map),
            pl.BlockSpec((tm, HEAD_DIM), rope_map),
            _layer_spec((N_FOURIER_GROUPS, FOURIER_GROUP, 2 * FOURIER_GROUP), layer),
        ] + cast_in,
        out_specs=[
            pl.BlockSpec((tm, ATTN_WIDTH), lambda i: (i, 0)),
            pl.BlockSpec((tm, KV_WIDTH), lambda i: (i, 0)),
            pl.BlockSpec((tm, KV_WIDTH), lambda i: (i, 0)),
            pl.BlockSpec((tm, 2 * FOURIER_WIDTH), lambda i: (i, 0)),
        ] + cast_out,
        out_shape=[
            jax.ShapeDtypeStruct((t, ATTN_WIDTH), BF16),
            jax.ShapeDtypeStruct((t, KV_WIDTH), BF16),
            jax.ShapeDtypeStruct((t, KV_WIDTH), BF16),
            jax.ShapeDtypeStruct((t, 2 * FOURIER_WIDTH), BF16),
        ] + cast_shapes,
        compiler_params=_params("arbitrary"),
        name="in_proj",
    )(x2d, mod, g_pre, w_in, qn, kn, cos_t, sin_t, m_fold, *cast)


def _ctx_kv_kernel(x_ref, mod_ref, gpre_ref, w_ref, kn_ref, k_ref, v_ref):
    mod = mod_ref[0]
    h = _rms(x_ref[...], (1.0 + mod[1:2]) * gpre_ref[...]) + mod[0:1]
    p = jnp.dot(h.astype(BF16), w_ref[...], preferred_element_type=F32)
    for i in range(N_KV_HEADS):
        cols = slice(i * HEAD_DIM, (i + 1) * HEAD_DIM)
        k_ref[:, cols] = _rms(p[:, cols], kn_ref[...]).astype(BF16)
    v_ref[...] = p[:, KV_WIDTH:].astype(BF16)


def _ctx_kv(x2d, mod, g_pre, w_in, kn, *, layer, mod_row, tm):
    t, d = x2d.shape
    assert Q_END % (2 * KV_WIDTH) == 0
    return pl.pallas_call(
        _ctx_kv_kernel,
        grid=(t // tm,),
        in_specs=[
            pl.BlockSpec((tm, d), lambda i: (i, 0)),
            _mod_spec(d, mod_row),
            _layer_spec((1, d), layer),
            pl.BlockSpec((None, d, 2 * KV_WIDTH), lambda i: (layer, 0, Q_END // (2 * KV_WIDTH)),
                         pipeline_mode=pl.Buffered(1)),
            _layer_spec((1, HEAD_DIM), layer),
        ],
        out_specs=[pl.BlockSpec((tm, KV_WIDTH), lambda i: (i, 0))] * 2,
        out_shape=[jax.ShapeDtypeStruct((t, KV_WIDTH), BF16)] * 2,
        compiler_params=_params("arbitrary"),
        name="ctx_kv",
    )(x2d, mod, g_pre, w_in, kn)


def _attn_kernel(*refs, n_seg, n_cast):
    q_ref = refs[0]
    kv_refs = refs[1:1 + 2 * n_seg]
    g_ref = refs[1 + 2 * n_seg]
    rest = refs[2 + 2 * n_seg:]
    cast_src, o_ref, cast_dst, acc_ref = rest[:n_cast], rest[n_cast], rest[n_cast + 1:-1], rest[-1]
    _run_side_casts(cast_src, cast_dst)

    chunks = []
    for i in range(n_seg):
        total = kv_refs[2 * i].shape[0]
        for lo in range(0, total, ATTN_KEY_CHUNK):
            chunks.append((kv_refs[2 * i], kv_refs[2 * i + 1], lo, min(ATTN_KEY_CHUNK, total - lo)))

    def chunk_scores(h, c):
        k_ref, _, lo, size = chunks[c]
        kv = (h // Q_PER_KV) * HEAD_DIM
        return lax.dot_general(k_ref[lo:lo + size, kv:kv + HEAD_DIM], q_ref[:, h * HEAD_DIM:(h + 1) * HEAD_DIM],
                               (((1,), (1,)), ((), ())), preferred_element_type=F32)

    def chunk_output(h, c, p):
        _, v_ref, lo, size = chunks[c]
        kv = (h // Q_PER_KV) * HEAD_DIM
        return lax.dot_general(v_ref[lo:lo + size, kv:kv + HEAD_DIM], p,
                               (((0,), (0,)), ((), ())), preferred_element_type=F32)

    def col_max(m, s):
        cm = jnp.max(s, axis=0, keepdims=True)
        return cm if m is None else jnp.maximum(m, cm)

    scores, m = [], None
    for c in range(len(chunks)):
        scores.append(chunk_scores(0, c))
        m = col_max(m, scores[-1])
    for h in range(N_Q_HEADS):
        next_scores, next_m = [], None
        denom, out_t = None, None
        for c in range(len(chunks)):
            if h + 1 < N_Q_HEADS:
                next_scores.append(chunk_scores(h + 1, c))
                next_m = col_max(next_m, next_scores[-1])
            p = jnp.exp2(scores[c] - m)
            ps = jnp.sum(p, axis=0, keepdims=True)
            po = chunk_output(h, c, p.astype(BF16))
            denom = ps if denom is None else denom + ps
            out_t = po if out_t is None else out_t + po
        acc_ref[:, h * HEAD_DIM:(h + 1) * HEAD_DIM] = (out_t / denom).T
        scores, m = next_scores, next_m
    o_ref[...] = _rms(acc_ref[...], g_ref[...]).astype(BF16)


def _attention(q, segments, g_attn, *, layer, q_seq, tq, cast=()):
    t = q.shape[0]
    per_seq = q_seq // tq
    in_specs = [pl.BlockSpec((tq, ATTN_WIDTH), lambda b, j: (b * per_seq + j, 0))]
    args = [q]
    for k, v, rows in segments:
        in_specs += [pl.BlockSpec((rows, KV_WIDTH), lambda b, j: (b, 0))] * 2
        args += [k, v]
    in_specs.append(_layer_spec((1, ATTN_WIDTH), layer))
    args.append(g_attn)
    cast_in, cast_out, cast_shapes = _side_casts(cast, layer, t // tq, lambda b, j: b * per_seq + j)
    return pl.pallas_call(
        functools.partial(_attn_kernel, n_seg=len(segments), n_cast=len(cast)),
        grid=(t // q_seq, per_seq),
        in_specs=in_specs + cast_in,
        out_specs=[pl.BlockSpec((tq, ATTN_WIDTH), lambda b, j: (b * per_seq + j, 0))] + cast_out,
        out_shape=[jax.ShapeDtypeStruct((t, ATTN_WIDTH), BF16)] + cast_shapes,
        scratch_shapes=[pltpu.VMEM((tq, ATTN_WIDTH), F32)],
        compiler_params=_params("arbitrary", "arbitrary"),
        name="attention",
    )(*args, *cast)


def _pos_dft_kernel(c_ref, s_ref, a_ref, b_ref, g_ref, o_ref):
    y = (jnp.dot(c_ref[...], a_ref[...], preferred_element_type=F32)
         - jnp.dot(s_ref[...], b_ref[...], preferred_element_type=F32))
    o_ref[...] = _rms(y, g_ref[...]).astype(BF16)


def _pos_dft(cos_m, sin_m, ab, g_four, *, layer, tm):
    seq = cos_m.shape[0]
    t = ab.shape[0]
    per_seq = seq // tm
    return pl.pallas_call(
        _pos_dft_kernel,
        grid=(t // seq, per_seq),
        in_specs=[
            pl.BlockSpec((tm, seq), lambda b, j: (j, 0)),
            pl.BlockSpec((tm, seq), lambda b, j: (j, 0)),
            pl.BlockSpec((seq, FOURIER_WIDTH), lambda b, j: (b, 0)),
            pl.BlockSpec((seq, FOURIER_WIDTH), lambda b, j: (b, 1)),
            _layer_spec((1, FOURIER_WIDTH), layer),
        ],
        out_specs=pl.BlockSpec((tm, FOURIER_WIDTH), lambda b, j: (b * per_seq + j, 0)),
        out_shape=jax.ShapeDtypeStruct((t, FOURIER_WIDTH), BF16),
        compiler_params=_params("arbitrary", "arbitrary"),
        name="pos_dft",
    )(cos_m, sin_m, ab, ab, g_four)


FLIP_BLOCK = 256
DFT_PAD_ROWS = 16


def _pos_dft_sym_kernel(alo_ref, ahi_ref, blo_ref, bhi_ref, c_ref, s_ref, q_ref, sgn_ref, g_ref,
                        o_ref, e_ref, d_ref, h_ref):
    half = alo_ref.shape[0]
    blk = q_ref.shape[0]
    nblk = half // blk
    qm = q_ref[...]
    is_row0 = lax.broadcasted_iota(jnp.int32, (blk, 1), 0) == 0

    def flipped_block(src_ref, bt, first_row):
        lo = (nblk - 1 - bt) * blk
        y = jnp.dot(qm, src_ref[lo:lo + blk, :], preferred_element_type=F32)
        if bt > 0:
            first_row = src_ref[lo + blk:lo + blk + 1, :].astype(F32)
        return jnp.where(is_row0, first_row, y)

    for bt in range(nblk):
        r = slice(bt * blk, (bt + 1) * blk)
        e_ref[r, :] = (alo_ref[r, :].astype(F32) + flipped_block(ahi_ref, bt, 0.0)).astype(BF16)
        d_ref[r, :] = (blo_ref[r, :].astype(F32) - flipped_block(bhi_ref, bt, 0.0)).astype(BF16)

    a_mid = ahi_ref[0:1, :].astype(F32)
    g = g_ref[...]
    mid_row = None
    for kb in range(nblk):
        r = slice(kb * blk, (kb + 1) * blk)
        rows = blk + (DFT_PAD_ROWS if kb == nblk - 1 else 0)
        cr = slice(kb * blk, kb * blk + rows)
        yc = jnp.dot(c_ref[cr, :], e_ref[...], preferred_element_type=F32) + sgn_ref[cr, :] * a_mid
        ys = jnp.dot(s_ref[r, :], d_ref[...], preferred_element_type=F32)
        o_ref[r, :] = _rms(yc[:blk] - ys, g).astype(BF16)
        h_ref[r, :] = _rms(yc[:blk] + ys, g).astype(BF16)
        if kb == nblk - 1:
            mid_row = _rms(yc[blk:blk + 1], g)

    for bs in range(nblk):
        r = slice(half + bs * blk, half + (bs + 1) * blk)
        o_ref[r, :] = flipped_block(h_ref, bs, mid_row).astype(BF16)


def _pos_dft_sym(tables, ab, g_four, *, layer):
    cos_m, sin_m, flip_m, sgn = tables
    half = sin_m.shape[0]
    t = ab.shape[0]
    blocks = lambda row, col: pl.BlockSpec((half, FOURIER_WIDTH), lambda b: (2 * b + row, col))
    return pl.pallas_call(
        _pos_dft_sym_kernel,
        grid=(t // (2 * half),),
        in_specs=[
            blocks(0, 0), blocks(1, 0), blocks(0, 1), blocks(1, 1),
            _const_spec(cos_m.shape),
            _const_spec(sin_m.shape),
            _const_spec(flip_m.shape),
            _const_spec(sgn.shape),
            _layer_spec((1, FOURIER_WIDTH), layer),
        ],
        out_specs=pl.BlockSpec((2 * half, FOURIER_WIDTH), lambda b: (b, 0)),
        out_shape=jax.ShapeDtypeStruct((t, FOURIER_WIDTH), BF16),
        scratch_shapes=[pltpu.VMEM((half, FOURIER_WIDTH), BF16)] * 3,
        compiler_params=_params("arbitrary"),
        name="pos_dft_sym",
    )(ab, ab, ab, ab, cos_m, sin_m, flip_m, sgn, g_four)


def _out_proj_kernel(a_ref, f_ref, w_ref, x_ref, mod_ref, gpost_ref, gpre_ref, *rest, next_mod):
    if next_mod:
        (c_ref, wm_ref, bm_ref), rest = rest[:3], rest[3:]
        _mod_block(c_ref, wm_ref, bm_ref, rest[2])
    xo_ref, h_ref = rest[:2]
    mod = mod_ref[0]
    gate_g = mod[2:3] * gpost_ref[...]
    scale_g = (1.0 + mod[4:5]) * gpre_ref[...]
    rows = x_ref.shape[0] // OUT_PROJ_SUB_BLOCKS

    def matmuls(s):
        r = slice(s * rows, (s + 1) * rows)
        return (jnp.dot(a_ref[r, :], w_ref[:ATTN_WIDTH, :], preferred_element_type=F32)
                + jnp.dot(f_ref[r, :], w_ref[ATTN_WIDTH:, :], preferred_element_type=F32))

    pending = matmuls(0)
    for s in range(OUT_PROJ_SUB_BLOCKS):
        r = slice(s * rows, (s + 1) * rows)
        mix = pending
        if s + 1 < OUT_PROJ_SUB_BLOCKS:
            pending = matmuls(s + 1)
        xn = x_ref[r, :] + _rms(mix, gate_g)
        xo_ref[r, :] = xn
        h_ref[r, :] = (_rms(xn, scale_g) + mod[3:4]).astype(BF16)


def _out_proj(a, f, w_out, x2d, mod, g_post, g_pre_ffn, *, layer, mod_row, tm, next_mod=None):
    t, d = x2d.shape
    n_steps = t // tm
    in_specs = [
        pl.BlockSpec((tm, ATTN_WIDTH), lambda i: (i, 0)),
        pl.BlockSpec((tm, FOURIER_WIDTH), lambda i: (i, 0)),
        _const_spec((ATTN_WIDTH + FOURIER_WIDTH, d)),
        pl.BlockSpec((tm, d), lambda i: (i, 0)),
        _mod_spec(d, mod_row),
        _layer_spec((1, d), layer),
        _layer_spec((1, d), layer),
    ]
    out_specs = [pl.BlockSpec((tm, d), lambda i: (i, 0)), pl.BlockSpec((tm, d), lambda i: (i, 0))]
    out_shape = [jax.ShapeDtypeStruct((t, d), F32), jax.ShapeDtypeStruct((t, d), BF16)]
    args = [a, f, w_out, x2d, mod, g_post, g_pre_ffn]
    if next_mod is not None:
        mod_in, mod_out, mod_shape = _mod_block_specs(d, next_mod[1].shape[-1], layer + 1, n_steps)
        in_specs += mod_in
        out_specs.append(mod_out)
        out_shape.append(mod_shape)
        args += list(next_mod)
    return pl.pallas_call(
        functools.partial(_out_proj_kernel, next_mod=next_mod is not None),
        grid=(n_steps,),
        in_specs=in_specs,
        out_specs=out_specs,
        out_shape=out_shape,
        compiler_params=_params("arbitrary"),
        name="out_proj",
    )(*args)


CONV_HALO = 8


def _ffn_up_kernel(h_ref, wg_ref, wv_ref, cw_ref, cb_ref, o_ref, *, seg_len):
    rows, tn = o_ref.shape
    chunk = rows // FFN_UP_ROW_CHUNKS
    pos = lax.broadcasted_iota(jnp.int32, (rows, 1), 0) % seg_len
    first, final = pos == 0, pos == seg_len - 1
    cw = cw_ref[...]
    cb = cb_ref[...]

    def span(s):
        return max(s * chunk - CONV_HALO, 0), min((s + 1) * chunk + CONV_HALO, rows)

    def matmuls(s):
        lo, hi = span(s)
        h = h_ref[lo:hi, :]
        return (jnp.dot(h, wg_ref[...], preferred_element_type=F32),
                jnp.dot(h, wv_ref[...], preferred_element_type=F32))

    def epilogue(s, gate, val):
        lo, hi = span(s)
        prev = jnp.where(first[lo:hi], 0.0, pltpu.roll(gate, 1, axis=0))
        nxt = jnp.where(final[lo:hi], 0.0, pltpu.roll(gate, hi - lo - 1, axis=0))
        c = prev * cw[0:1] + gate * cw[1:2] + nxt * cw[2:3] + cb
        act = 0.5 * c * (1.0 + jnp.tanh(math.sqrt(2.0 / math.pi) * (c + 0.044715 * (c * c * c))))
        own = slice(s * chunk - lo, (s + 1) * chunk - lo)
        o_ref[s * chunk:(s + 1) * chunk, :] = (act * val)[own].astype(BF16)

    pending = matmuls(0)
    for s in range(FFN_UP_ROW_CHUNKS):
        ready = pending
        if s + 1 < FFN_UP_ROW_CHUNKS:
            pending = matmuls(s + 1)
        epilogue(s, *ready)


def _ffn_up(h, w_up, conv_w, conv_b, *, layer, seg_len, tm, tn):
    t, d = h.shape
    ffn = conv_w.shape[-1]
    n_col = ffn // tn
    return pl.pallas_call(
        functools.partial(_ffn_up_kernel, seg_len=seg_len),
        grid=(t // tm, n_col),
        in_specs=[
            pl.BlockSpec((tm, d), lambda i, j: (i, 0)),
            pl.BlockSpec((d, tn), lambda i, j: (0, j)),
            pl.BlockSpec((d, tn), lambda i, j: (0, j + n_col)),
            pl.BlockSpec((None, 3, tn), lambda i, j: (layer, 0, j)),
            pl.BlockSpec((None, 1, tn), lambda i, j: (layer, 0, j)),
        ],
        out_specs=pl.BlockSpec((tm, tn), lambda i, j: (i, j)),
        out_shape=jax.ShapeDtypeStruct((t, ffn), BF16),
        compiler_params=_params("arbitrary", "arbitrary"),
        name="ffn_up",
    )(h, w_up, w_up, conv_w, conv_b)


LANES = 128


def _ffn_down_skew_kernel(u_ref, w_hbm, x_ref, mod_ref, gpost_ref, xo_ref, y_ref, inv_ref, ssq_ref,
                          w_buf, w_sem, *, n_tiles):
    i, j = pl.program_id(0), pl.program_id(1)
    n_col, _, tn = y_ref.shape
    step = i * n_col + j
    n_steps = n_tiles * n_col
    ahead = W_DOWN_BUFFERS - 1

    def w_copy(s):
        col = pl.multiple_of((s % n_col) * tn, tn)
        slot = s % W_DOWN_BUFFERS
        return pltpu.make_async_copy(w_hbm.at[:, pl.ds(col, tn)], w_buf.at[slot], w_sem.at[slot])

    @pl.when((i == 0) & (j == 0))
    def _():
        for s in range(min(ahead, n_col)):
            w_copy(s).start()
        y_ref[...] = jnp.zeros_like(y_ref)
        inv_ref[...] = jnp.zeros_like(inv_ref)

    @pl.when(j == 0)
    def _():
        ssq_ref[...] = jnp.zeros_like(ssq_ref)

    def finish_previous():
        gate_g = mod_ref[0][N_MOD - 1:N_MOD] * gpost_ref[...]
        xo_ref[...] = x_ref[...] + (y_ref[j] * inv_ref[...]) * gate_g

    @pl.when((i < n_tiles) & (step + ahead < n_steps))
    def _():
        w_copy(step + ahead).start()

    @pl.when(i < n_tiles)
    def _():
        finish_previous()
        w_copy(step).wait()
        y = jnp.dot(u_ref[...], w_buf[step % W_DOWN_BUFFERS], preferred_element_type=F32)
        y_ref[j] = y
        sq = y * y
        ssq_ref[...] += functools.reduce(
            jnp.add, [sq[:, c * LANES:(c + 1) * LANES] for c in range(tn // LANES)])

    @pl.when(i == n_tiles)
    def _():
        finish_previous()

    @pl.when((j == n_col - 1) & (i < n_tiles))
    def _():
        ssq = jnp.sum(ssq_ref[...], axis=-1, keepdims=True)
        inv_ref[...] = lax.rsqrt(ssq / (n_col * tn) + EPS)


def _ffn_down_skew(u, w_down, x2d, mod, g_post, *, layer, mod_row, tm, tn):
    t, d = x2d.shape
    ffn = u.shape[1]
    n_tiles = t // tm
    n_col = d // tn
    prev = lambda i: jnp.maximum(i - 1, 0)
    col = lambda i, j: jnp.where(i > 0, j, 0)
    return pl.pallas_call(
        functools.partial(_ffn_down_skew_kernel, n_tiles=n_tiles),
        grid=(n_tiles + 1, n_col),
        in_specs=[
            pl.BlockSpec((tm, ffn), lambda i, j: (jnp.minimum(i, n_tiles - 1), 0)),
            pl.BlockSpec(memory_space=pl.ANY),
            pl.BlockSpec((tm, tn), lambda i, j: (prev(i), col(i, j))),
            pl.BlockSpec((1, N_MOD, tn), lambda i, j: (mod_row(prev(i)), 0, col(i, j))),
            pl.BlockSpec((None, 1, tn), lambda i, j: (layer, 0, col(i, j))),
        ],
        out_specs=pl.BlockSpec((tm, tn), lambda i, j: (prev(i), col(i, j))),
        out_shape=jax.ShapeDtypeStruct((t, d), F32),
        scratch_shapes=[pltpu.VMEM((n_col, tm, tn), F32), pltpu.VMEM((tm, 1), F32),
                        pltpu.VMEM((tm, LANES), F32),
                        pltpu.VMEM((W_DOWN_BUFFERS, ffn, tn), BF16),
                        pltpu.SemaphoreType.DMA((W_DOWN_BUFFERS,))],
        compiler_params=_params("arbitrary", "arbitrary"),
        name="ffn_down_skew",
    )(u, w_down, x2d, mod, g_post)


def _rope_tables(n):
    pos = np.arange(n)
    n_pairs_axis = HEAD_DIM // 4
    freqs = ROPE_THETA ** (-np.arange(n_pairs_axis, dtype=np.float64) / n_pairs_axis)
    ang = np.concatenate([(pos // GRID_W)[:, None] * freqs, (pos % GRID_W)[:, None] * freqs], axis=-1)
    c, s = np.cos(ang), np.sin(ang)
    return (np.concatenate([c, c], axis=-1).astype(np.float32),
            np.concatenate([-s, s], axis=-1).astype(np.float32))


def _dft_matrices(n):
    idx = np.arange(n)
    ang = ((idx[:, None] * idx[None, :]) % n) * (2.0 * math.pi / n)
    return np.cos(ang) * n ** -0.5, np.sin(ang) * n ** -0.5


def _sym_dft_tables(n):
    half = n // 2
    k = np.arange(half + DFT_PAD_ROWS)[:, None]
    ang = ((k * np.arange(half)[None, :]) % n) * (2.0 * math.pi / n)
    scale = n ** -0.5
    valid = k <= half
    cos_m = jnp.asarray(np.where(valid, np.cos(ang) * scale, 0.0), F32).astype(BF16)
    sin_m = jnp.asarray(np.sin(ang[:half]) * scale, F32).astype(BF16)
    sgn = np.where(valid, (1 - 2 * (k % 2)) * scale, 0.0).astype(np.float32)
    i = np.arange(FLIP_BLOCK)
    flip_m = (i[:, None] + i[None, :] == FLIP_BLOCK).astype(BF16)
    return cos_m, sin_m, flip_m, sgn


def _deinterleave(v):
    return jnp.concatenate([v[..., 0::2], v[..., 1::2]], axis=-1)


def kernel(x, c, ctx, c_ctx, w_mod, b_mod, g_pre_mix, g_post_mix, g_pre_ffn, g_post_ffn,
           w_in, q_norm, k_norm, w_four, g_attn_out, g_four_out, w_out,
           w_up, conv_w, conv_b, w_down):
    b, n, d = x.shape
    ctx_len = ctx.shape[1]
    depth = w_mod.shape[0]

    perm_matrix = _deinterleave(np.eye(HEAD_DIM, dtype=BF16))
    w_in_b = _prep_w_in(w_in, perm_matrix)
    qn = _deinterleave(q_norm).reshape(depth, 1, HEAD_DIM)
    kn = _deinterleave(k_norm).reshape(depth, 1, HEAD_DIM)
    cos_t, sin_t = _rope_tables(n)
    dft_tables = _sym_dft_tables(n)
    cos_c, sin_c = (jnp.asarray(m, F32).astype(BF16) for m in _dft_matrices(ctx_len))
    cs_ch = np.stack(_dft_matrices(FOURIER_GROUP)).astype(np.float32)
    m_fold = _fold_channel_dft(cs_ch, w_four)
    rows3 = lambda v: v.reshape(depth, 1, -1)
    g_pre_mix, g_post_mix, g_pre_ffn, g_post_ffn, g_attn_out, g_four_out, conv_b, b_mod = map(
        rows3, (g_pre_mix, g_post_mix, g_pre_ffn, g_post_ffn, g_attn_out, g_four_out, conv_b, b_mod))

    cc = jnp.zeros((MOD_ROWS, d), F32).at[:b].set(c).at[b].set(c_ctx)
    as_table = lambda m: m.reshape(MOD_ROWS, N_MOD, d)
    mod = as_table(_mod_proj(cc, w_mod, b_mod, layer=0))

    xl = x.reshape(b * n, d)
    xc = ctx.reshape(b * ctx_len, d)
    tm = ROW_TILE
    lat_row = lambda i: i // (n // tm)
    ctx_row = lambda i: b

    for i in range(depth):
        last = i == depth - 1
        q_l, k_l, v_l, ab_l, w_up_b = _in_proj(
            xl, mod, g_pre_mix, w_in_b, qn, kn, cos_t, sin_t, m_fold,
            layer=i, mod_row=lat_row, seq=n, rope=True, tm=tm, cast=(w_up,))
        if last:
            k_c, v_c = _ctx_kv(xc, mod, g_pre_mix, w_in_b, kn, layer=i, mod_row=ctx_row, tm=tm)
        else:
            q_c, k_c, v_c, ab_c = _in_proj(
                xc, mod, g_pre_mix, w_in_b, qn, kn, cos_t, sin_t, m_fold,
                layer=i, mod_row=ctx_row, seq=ctx_len, rope=False, tm=tm)

        attn_l, w_out_b, w_down_b = _attention(
            q_l, [(k_l, v_l, n), (k_c, v_c, ctx_len)], g_attn_out,
            layer=i, q_seq=n, tq=ATTN_Q_TILE, cast=(w_out, w_down))
        four_l = _pos_dft_sym(dft_tables, ab_l, g_four_out, layer=i)
        xl, h_l, *next_mod = _out_proj(
            attn_l, four_l, w_out_b, xl, mod, g_post_mix, g_pre_ffn, layer=i, mod_row=lat_row, tm=tm,
            next_mod=None if last else (cc, w_mod, b_mod))
        u_l = _ffn_up(h_l, w_up_b, conv_w, conv_b, layer=i, seg_len=n, tm=n, tn=FFN_COL_TILE)
        xl = _ffn_down_skew(u_l, w_down_b, xl, mod, g_post_ffn, layer=i,
                            mod_row=lambda r: r // (n // FFN_DOWN_ROW_TILE),
                            tm=FFN_DOWN_ROW_TILE, tn=FFN_COL_TILE)

        if not last:
            attn_c, = _attention(q_c, [(k_c, v_c, ctx_len)], g_attn_out,
                                 layer=i, q_seq=ctx_len, tq=ctx_len)
            four_c = _pos_dft(cos_c, sin_c, ab_c, g_four_out, layer=i, tm=ctx_len)
            xc, h_c = _out_proj(attn_c, four_c, w_out_b, xc, mod, g_post_mix, g_pre_ffn,
                                layer=i, mod_row=ctx_row, tm=tm)
            u_c = _ffn_up(h_c, w_up_b, conv_w, conv_b, layer=i, seg_len=ctx_len,
                          tm=b * ctx_len, tn=FFN_COL_TILE)
            xc = _ffn_down_skew(u_c, w_down_b, xc, mod, g_post_ffn, layer=i, mod_row=ctx_row,
                                tm=FFN_DOWN_ROW_TILE, tn=FFN_COL_TILE)
            mod = as_table(next_mod[0])

    return xl.reshape(b, n, d)
```
